```python
import jax, jax.numpy as jnp
from jax import lax
import numpy as np

D_MODEL = 1024
BATCH = 8
SEQ = 2048
DEPTH = 1
DEC_BATCH = 128
DEC_SEQ = 8
PAST_LEN = 16384
PAGE_SIZE = 128

HEAD_DIM = 64
N_HEADS = D_MODEL // HEAD_DIM
RWKV_W = N_HEADS * HEAD_DIM
D_DECAY = 64
D_AAA = 64
D_GATE = 128
SHIFT_W = 3 * RWKV_W + D_DECAY + D_AAA + D_GATE
CONV_CH = D_MODEL
CONV_WIDTH = 31
IN_W = SHIFT_W + 2 * CONV_CH + 2 * D_MODEL
N_EXPERTS = 32
TOP_K = 4
D_FF = D_MODEL
SWIGLU_LIMIT = 7.0
SWIGLU_ALPHA = 1.702
MOE_BLOCK = 256
RMS_EPS = 1e-6
LN_EPS = 1e-5
GN_EPS = 64e-5

kernel_name = 'rwkv7_conformer_gated_moe_step'

F32 = jnp.float32


def rms_norm(x, g):
    xf = x.astype(F32)
    y = xf * lax.rsqrt(jnp.mean(xf * xf, axis=-1, keepdims=True) + RMS_EPS)
    return (y * g.astype(F32)).astype(x.dtype)


def rwkv7_time_mix(p, wkv0, shift0, mu_shift, w0, w2, a0, a2, g2, k_k, k_a, r_k, lnx_g, lnx_b, w_out_a):
    B, L, _ = p.shape
    pf = p.astype(F32)
    prev = jnp.concatenate([shift0.astype(F32)[:, None], pf[:, :-1]], axis=1)
    h = pf + (prev - pf) * mu_shift.astype(F32)
    r, k, v, wd, ad, gd = jnp.split(
        h, [RWKV_W, 2 * RWKV_W, 3 * RWKV_W, 3 * RWKV_W + D_DECAY, 3 * RWKV_W + D_DECAY + D_AAA], axis=-1)
    w_log = -jax.nn.softplus(-(w0.astype(F32) + jnp.tanh(wd) @ w2.astype(F32))) - 0.5
    decay = jnp.exp(-jnp.exp(w_log))
    a = jax.nn.sigmoid(a0.astype(F32) + ad @ a2.astype(F32))
    g = jax.nn.sigmoid(gd) @ g2.astype(F32)
    heads = lambda t: t.reshape(B, L, N_HEADS, HEAD_DIM)
    kk = heads(k * k_k.astype(F32))
    kk = kk * lax.rsqrt(jnp.maximum(jnp.sum(kk * kk, axis=-1, keepdims=True), 1e-24))
    k = k * (1.0 + (a - 1.0) * k_a.astype(F32))
    r, k, v, a, decay = heads(r), heads(k), heads(v), heads(a), heads(decay)

    def step(S, inp):
        r_t, w_t, k_t, v_t, kk_t, a_t = inp
        sa = jnp.einsum('bhij,bhj->bhi', S, -kk_t)
        S = (S * w_t[:, :, None, :] + sa[..., None] * (kk_t * a_t)[:, :, None, :]
             + v_t[..., None] * k_t[:, :, None, :])
        y_t = jnp.einsum('bhij,bhj->bhi', S, r_t)
        return S, y_t

    tm = lambda t: jnp.moveaxis(t, 1, 0)
    S_fin, y = lax.scan(step, wkv0.astype(F32), (tm(r), tm(decay), tm(k), tm(v), tm(kk), tm(a)))
    y = jnp.moveaxis(y, 0, 1)
    mu = jnp.mean(y, axis=-1, keepdims=True)
    var = jnp.mean(jnp.square(y - mu), axis=-1, keepdims=True)
    y = ((y - mu) * lax.rsqrt(var + GN_EPS) * lnx_g.astype(F32).reshape(N_HEADS, HEAD_DIM)
         + lnx_b.astype(F32).reshape(N_HEADS, HEAD_DIM))
    y = y + jnp.sum(r * k * r_k.astype(F32), axis=-1, keepdims=True) * v
    out = (y.reshape(B, L, RWKV_W) * g) @ w_out_a.astype(F32)
    return out.astype(p.dtype), S_fin.astype(wkv0.dtype), p[:, -1]


def conformer_conv(p_glu, conv0, b_glu, w_dw, b_dw, ln_conv_g, ln_conv_b, w_pw2, b_pw2):
    z = p_glu + b_glu
    u = z[..., :CONV_CH] * jax.nn.sigmoid(z[..., CONV_CH:])
    full = jnp.concatenate([conv0.astype(u.dtype), u], axis=1)
    c = lax.conv_general_dilated(full, w_dw.astype(u.dtype), (1,), 'VALID',
                                 dimension_numbers=('NWC', 'WIO', 'NWC'),
                                 feature_group_count=CONV_CH) + b_dw
    cf = c.astype(F32)
    mu = jnp.mean(cf, axis=-1, keepdims=True)
    var = jnp.mean(jnp.square(cf - mu), axis=-1, keepdims=True)
    cf = (cf - mu) * lax.rsqrt(var + LN_EPS) * ln_conv_g.astype(F32) + ln_conv_b.astype(F32)
    act = cf * jax.nn.sigmoid(cf)
    out = act @ w_pw2.astype(F32) + b_pw2.astype(F32)
    return out.astype(p_glu.dtype), full[:, -(CONV_WIDTH - 1):]


def moe_ffn(t, w_router, b_router, w_up, b_up, w_down, b_down):
    T = t.shape[0]
    logits = (t @ w_router).astype(F32) + b_router.astype(F32)
    top_v, top_e = lax.top_k(logits, TOP_K)
    gates = jax.nn.softmax(top_v, axis=-1)
    n_assign = T * TOP_K
    flat_e = top_e.reshape(-1)
    flat_tok = jnp.arange(n_assign, dtype=jnp.int32) // TOP_K
    order = jnp.argsort(flat_e)
    se, stok, sg = flat_e[order], flat_tok[order], gates.reshape(-1)[order]
    counts = jnp.bincount(flat_e, length=N_EXPERTS)
    padded = (counts + MOE_BLOCK - 1) // MOE_BLOCK * MOE_BLOCK
    pad_end = jnp.cumsum(padded)
    pad_start = pad_end - padded
    start = jnp.cumsum(counts) - counts
    dest = pad_start[se] + jnp.arange(n_assign) - start[se]
    n_blocks = -(-n_assign // MOE_BLOCK) + N_EXPERTS
    n_rows = n_blocks * MOE_BLOCK
    slot_tok = jnp.full((n_rows,), T, jnp.int32).at[dest].set(stok)
    slot_gate = jnp.zeros((n_rows,), F32).at[dest].set(sg)
    blk_start = jnp.arange(n_blocks) * MOE_BLOCK
    blk_e = jnp.minimum(jnp.searchsorted(pad_end, blk_start, side='right'), N_EXPERTS - 1)
    t_pad = jnp.concatenate([t, jnp.zeros((1, t.shape[1]), t.dtype)], axis=0)

    def expert_block(args):
        tok, e = args
        xb = t_pad[tok]
        hb = (xb @ w_up[e] + b_up[e]).astype(F32)
        glu = jnp.minimum(hb[:, :D_FF], SWIGLU_LIMIT)
        lin = jnp.clip(hb[:, D_FF:], -SWIGLU_LIMIT, SWIGLU_LIMIT)
        act = glu * jax.nn.sigmoid(SWIGLU_ALPHA * glu) * (lin + 1.0)
        return act @ w_down[e].astype(F32) + b_down[e].astype(F32)

    out = lax.map(expert_block, (slot_tok.reshape(n_blocks, MOE_BLOCK), blk_e))
    y = jnp.zeros((T + 1, t.shape[1]), F32).at[slot_tok].add(out.reshape(n_rows, -1) * slot_gate[:, None])
    return y[:T].astype(t.dtype)


def decoder_layer(x, wkv0, shift0, conv0, prm):
    B, L, D = x.shape
    xn = rms_norm(x, prm['norm1_g'])
    proj = xn @ prm['w_in']
    p_rwkv, p_glu, p_ga, p_gb = jnp.split(
        proj, [SHIFT_W, SHIFT_W + 2 * CONV_CH, SHIFT_W + 2 * CONV_CH + D_MODEL], axis=-1)
    y_a, wkv_new, shift_new = rwkv7_time_mix(
        p_rwkv, wkv0, shift0, prm['mu_shift'], prm['w0'], prm['w2'], prm['a0'], prm['a2'], prm['g2'],
        prm['k_k'], prm['k_a'], prm['r_k'], prm['lnx_g'], prm['lnx_b'], prm['w_out_a'])
    y_b, conv_new = conformer_conv(p_glu, conv0, prm['b_glu'], prm['w_dw'], prm['b_dw'],
                                   prm['ln_conv_g'], prm['ln_conv_b'], prm['w_pw2'], prm['b_pw2'])
    merged = jax.nn.sigmoid(p_ga) * y_a + jax.nn.sigmoid(p_gb) * y_b
    h = x + merged @ prm['w_o']
    hn = rms_norm(h, prm['norm2_g']).reshape(B * L, D)
    h = h + moe_ffn(hn, prm['w_router'], prm['b_router'], prm['w_up'], prm['b_up'],
                    prm['w_down'], prm['b_down']).reshape(B, L, D)
    return rms_norm(h, prm['norm_f_g']), wkv_new, shift_new, conv_new


def setup_inputs(seed: int = 0) -> dict:
    key = jax.random.key(seed)
    ks = jax.random.split(key, 40)
    nrm = lambda i, shape, s: jax.random.normal(ks[i], shape, F32) * s
    return {
        'x_prompt': nrm(0, (BATCH, SEQ, D_MODEL), 1.0),
        'x_sample': nrm(1, (DEC_BATCH, DEC_SEQ, D_MODEL), 1.0),
        'state_wkv': nrm(2, (DEC_BATCH, N_HEADS, HEAD_DIM, HEAD_DIM), 0.3),
        'state_shift': nrm(3, (DEC_BATCH, SHIFT_W), 1.0),
        'state_conv': nrm(4, (DEC_BATCH, CONV_WIDTH - 1, CONV_CH), 0.5),
        'norm1_g': 1.0 + nrm(5, (D_MODEL,), 0.05),
        'w_in': nrm(6, (D_MODEL, IN_W), D_MODEL ** -0.5),
        'b_glu': nrm(7, (2 * CONV_CH,), 0.02),
        'mu_shift': jax.random.uniform(ks[8], (SHIFT_W,), F32),
        'w0': nrm(9, (RWKV_W,), 0.5),
        'w2': nrm(10, (D_DECAY, RWKV_W), 0.1),
        'a0': nrm(11, (RWKV_W,), 0.3),
        'a2': nrm(12, (D_AAA, RWKV_W), 0.1),
        'g2': nrm(13, (D_GATE, RWKV_W), D_GATE ** -0.5),
        'k_k': 0.85 + nrm(14, (RWKV_W,), 0.05),
        'k_a': 1.0 + nrm(15, (RWKV_W,), 0.05),
        'r_k': nrm(16, (N_HEADS, HEAD_DIM), 0.1),
        'lnx_g': 1.0 + nrm(17, (RWKV_W,), 0.05),
        'lnx_b': nrm(18, (RWKV_W,), 0.02),
        'w_out_a': nrm(19, (RWKV_W, D_MODEL), RWKV_W ** -0.5),
        'w_dw': nrm(20, (CONV_WIDTH, 1, CONV_CH), CONV_WIDTH ** -0.5),
        'b_dw': nrm(21, (CONV_CH,), 0.02),
        'ln_conv_g': 1.0 + nrm(22, (CONV_CH,), 0.05),
        'ln_conv_b': nrm(23, (CONV_CH,), 0.02),
        'w_pw2': nrm(24, (CONV_CH, D_MODEL), CONV_CH ** -0.5),
        'b_pw2': nrm(25, (D_MODEL,), 0.02),
        'w_o': nrm(26, (D_MODEL, D_MODEL), D_MODEL ** -0.5),
        'norm2_g': 1.0 + nrm(27, (D_MODEL,), 0.05),
        'w_router': nrm(28, (D_MODEL, N_EXPERTS), D_MODEL ** -0.5),
        'b_router': nrm(29, (N_EXPERTS,), 0.01),
        'w_up': nrm(30, (N_EXPERTS, D_MODEL, 2 * D_FF), D_MODEL ** -0.5),
        'b_up': nrm(31, (N_EXPERTS, 2 * D_FF), 0.02),
        'w_down': nrm(32, (N_EXPERTS, D_FF, D_MODEL), D_FF ** -0.5),
        'b_down': nrm(33, (N_EXPERTS, D_MODEL), 0.02),
        'norm_f_g': 1.0 + nrm(34, (D_MODEL,), 0.05),
    }


def reference(x_prompt, x_sample, state_wkv, state_shift, state_conv,
              norm1_g, w_in, b_glu, mu_shift, w0, w2, a0, a2, g2, k_k, k_a, r_k, lnx_g, lnx_b, w_out_a,
              w_dw, b_dw, ln_conv_g, ln_conv_b, w_pw2, b_pw2, w_o,
              norm2_g, w_router, b_router, w_up, b_up, w_down, b_down, norm_f_g):
    prm = dict(norm1_g=norm1_g, w_in=w_in, b_glu=b_glu, mu_shift=mu_shift, w0=w0, w2=w2, a0=a0, a2=a2,
               g2=g2, k_k=k_k, k_a=k_a, r_k=r_k, lnx_g=lnx_g, lnx_b=lnx_b, w_out_a=w_out_a,
               w_dw=w_dw, b_dw=b_dw, ln_conv_g=ln_conv_g, ln_conv_b=ln_conv_b, w_pw2=w_pw2, b_pw2=b_pw2,
               w_o=w_o, norm2_g=norm2_g, w_router=w_router, b_router=b_router, w_up=w_up, b_up=b_up,
               w_down=w_down, b_down=b_down, norm_f_g=norm_f_g)
    bp = x_prompt.shape[0]
    y_p, wkv_p, shift_p, conv_p = x_prompt, None, None, None
    wkv_p = jnp.zeros((bp, N_HEADS, HEAD_DIM, HEAD_DIM), x_prompt.dtype)
    shift_p = jnp.zeros((bp, SHIFT_W), x_prompt.dtype)
    conv_p = jnp.zeros((bp, CONV_WIDTH - 1, CONV_CH), x_prompt.dtype)
    y_s, wkv_s, shift_s, conv_s = x_sample, state_wkv, state_shift, state_conv
    for _ in range(DEPTH):
        y_p, wkv_p, shift_p, conv_p = decoder_layer(y_p, wkv_p, shift_p, conv_p, prm)
        y_s, wkv_s, shift_s, conv_s = decoder_layer(y_s, wkv_s, shift_s, conv_s, prm)
    return (y_p, y_s, wkv_p, shift_p, conv_p, wkv_s, shift_s, conv_s)
```

```python
import functools

import jax
import jax.numpy as jnp
from jax import lax
from jax.experimental import pallas as pl
from jax.experimental.pallas import tpu as pltpu

F32 = jnp.float32
BF16 = jnp.bfloat16
HIGHEST = lax.Precision.HIGHEST

HEAD_DIM = 64
D_DECAY = 64
D_AAA = 64
D_GATE = 128
CONV_WIDTH = 31
N_EXPERTS = 32
TOP_K = 4
SWIGLU_LIMIT = 7.0
SWIGLU_ALPHA = 1.702
RMS_EPS = 1e-6
LN_EPS = 1e-5
GN_EPS = 64e-5

LANES = 128
HIST_ROWS = 32
ROW_BLOCK = 256
EXPERT_BLOCK = 256
HEADS_PER_ITER = 4
VMEM_LIMIT = 56 * 1024 * 1024


def _dot_hi(a, b):
    return jnp.dot(a, b, precision=HIGHEST, preferred_element_type=F32)


def _dot_nt(a, b):
    return lax.dot_general(a, b, (((1,), (1,)), ((), ())), precision=HIGHEST,
                           preferred_element_type=F32)


def _dot_tn(a, b):
    return lax.dot_general(a, b, (((0,), (0,)), ((), ())), precision=HIGHEST,
                           preferred_element_type=F32)


def _dot_bf16(a, b):
    return jnp.dot(a.astype(BF16), b, preferred_element_type=F32)


def _sigmoid(x):
    return 1.0 / (1.0 + jnp.exp(-x))


def _rms_norm(x, g):
    return x * lax.rsqrt(jnp.mean(x * x, axis=-1, keepdims=True) + RMS_EPS) * g


def _inproj_kernel(x_ref, g_ref, w_ref, o_rwkv, o_glu, o_gate, *, shift_w, glu_w):
    xb = _rms_norm(x_ref[...], g_ref[...]).astype(BF16)
    o_rwkv[...] = jnp.dot(xb, w_ref[:, :shift_w], preferred_element_type=F32)
    o_glu[...] = jnp.dot(xb, w_ref[:, shift_w:shift_w + glu_w], preferred_element_type=F32)
    o_gate[...] = jnp.dot(xb, w_ref[:, shift_w + glu_w:], preferred_element_type=F32)


def _in_proj(x2d, norm_g, w_in_bf16, shift_w, glu_w):
    t, d = x2d.shape
    in_w = w_in_bf16.shape[1]
    gate_w = in_w - shift_w - glu_w
    tm = ROW_BLOCK
    return pl.pallas_call(
        functools.partial(_inproj_kernel, shift_w=shift_w, glu_w=glu_w),
        out_shape=(jax.ShapeDtypeStruct((t, shift_w), F32),
                   jax.ShapeDtypeStruct((t, glu_w), F32),
                   jax.ShapeDtypeStruct((t, gate_w), F32)),
        grid=(t // tm,),
        in_specs=[pl.BlockSpec((tm, d), lambda i: (i, 0)),
                  pl.BlockSpec((1, d), lambda i: (0, 0)),
                  pl.BlockSpec((d, in_w), lambda i: (0, 0), pipeline_mode=pl.Buffered(1))],
        out_specs=(pl.BlockSpec((tm, shift_w), lambda i: (i, 0)),
                   pl.BlockSpec((tm, glu_w), lambda i: (i, 0)),
                   pl.BlockSpec((tm, gate_w), lambda i: (i, 0))),
        compiler_params=pltpu.CompilerParams(dimension_semantics=("arbitrary",),
                                             vmem_limit_bytes=VMEM_LIMIT),
        name="in_proj",
    )(x2d, norm_g.reshape(1, d), w_in_bf16)


def _rwkv_kernel(p_ref, shift0_ref, wkv0_ref, mu_ref, w0_ref, w2_ref, a0_ref, a2_ref, g2_ref,
                 kk_ref, ka_ref, rk_ref, lng_ref, lnb_ref,
                 yg_ref, wkv_ref, shift_ref,
                 pbuf, at_s, rt_s, kh_s, bh_s, v_s, y_s, out_s, *, chunk, n_heads):
    c = pl.program_id(1)
    rw = n_heads * HEAD_DIM

    @pl.when(c == 0)
    def _():
        pbuf[7:8, :] = shift0_ref[0]
        wkv_ref[...] = wkv0_ref[...]

    p = p_ref[...]
    pbuf[8:8 + chunk, :] = p
    prev = pbuf[7:7 + chunk, :]
    last = p[chunk - 1:chunk, :]
    pbuf[7:8, :] = last
    shift_ref[0] = last

    h = p + (prev - p) * mu_ref[...]
    r = h[:, 0:rw]
    k = h[:, rw:2 * rw]
    v = h[:, 2 * rw:3 * rw]
    o = 3 * rw
    wd = h[:, o:o + D_DECAY]
    ad = h[:, o + D_DECAY:o + D_DECAY + D_AAA]
    gd = h[:, o + D_DECAY + D_AAA:o + D_DECAY + D_AAA + D_GATE]

    z = -(w0_ref[...] + _dot_hi(jnp.tanh(wd), w2_ref[...]))
    softplus = jnp.maximum(z, 0.0) + jnp.log(1.0 + jnp.exp(-jnp.abs(z)))
    logw = -jnp.exp(-softplus - 0.5)
    ti = lax.broadcasted_iota(jnp.int32, (chunk, chunk), 0)
    si = lax.broadcasted_iota(jnp.int32, (chunk, chunk), 1)
    incl = ti >= si
    strict = ti > si
    cs = _dot_hi(incl.astype(F32), logw)
    w_incl = jnp.exp(cs)
    w_excl = jnp.exp(cs - logw)
    w_inv = jnp.exp(-cs)

    a = _sigmoid(a0_ref[...] + _dot_hi(ad, a2_ref[...]))
    g = _dot_hi(_sigmoid(gd), g2_ref[...])
    kk = k * kk_ref[...]
    k2 = k * (1.0 + (a - 1.0) * ka_ref[...])
    rt = r * w_incl
    kh = k2 * w_inv
    rk2 = r * k2 * rk_ref[...]

    for hd in range(n_heads):
        sl = slice(hd * HEAD_DIM, (hd + 1) * HEAD_DIM)
        kk_h = kk[:, sl]
        kk_h = kk_h * lax.rsqrt(jnp.maximum(jnp.sum(kk_h * kk_h, axis=-1, keepdims=True), 1e-24))
        at_s[hd] = kk_h * w_excl[:, sl]
        bh_s[hd] = kk_h * a[:, sl] * w_inv[:, sl]
        rt_s[hd] = rt[:, sl]
        kh_s[hd] = kh[:, sl]
        v_s[hd] = v[:, sl]

    eye = (ti == si).astype(F32)
    n_doublings = max(0, (chunk - 1).bit_length() - 1)

    def head_group(gi, carry):
        for j in range(HEADS_PER_ITER):
            hd = gi * HEADS_PER_ITER + j
            at_h = at_s[hd]
            rt_h = rt_s[hd]
            kh_h = kh_s[hd]
            bh_h = bh_s[hd]
            v_h = v_s[hd]
            s0 = wkv_ref[0, hd]
            ar = jnp.concatenate([at_h, rt_h], axis=0)
            gk = _dot_nt(ar, kh_h)
            gb = _dot_nt(ar, bh_h)
            x0 = _dot_nt(ar, s0)
            a_ak = jnp.where(strict, gk[:chunk], 0.0)
            a_rk = jnp.where(incl, gk[chunk:], 0.0)
            a_ab = jnp.where(strict, gb[:chunk], 0.0)
            a_rb = jnp.where(incl, gb[chunk:], 0.0)
            rhs = x0[:chunk] + _dot_hi(a_ak, v_h)
            nmat = -a_ab
            tinv = eye + nmat
            pw = nmat
            for _ in range(n_doublings):
                pw = _dot_hi(pw, pw)
                tinv = tinv + _dot_hi(tinv, pw)
            u = _dot_hi(tinv, rhs)
            y_s[hd] = x0[chunk:] + _dot_hi(a_rk, v_h) - _dot_hi(a_rb, u)
            s_new = s0 + _dot_tn(v_h, kh_h) - _dot_tn(u, bh_h)
            wkv_ref[0, hd] = s_new
        return carry

    lax.fori_loop(0, n_heads // HEADS_PER_ITER, head_group, 0)

    w_last = w_incl[chunk - 1:chunk, :]
    for hd in range(n_heads):
        sl = slice(hd * HEAD_DIM, (hd + 1) * HEAD_DIM)
        wkv_ref[0, hd] = wkv_ref[0, hd] * w_last[:, sl]
        y = y_s[hd]
        mu = jnp.mean(y, axis=-1, keepdims=True)
        yc = y - mu
        var = jnp.mean(yc * yc, axis=-1, keepdims=True)
        yn = yc * lax.rsqrt(var + GN_EPS) * lng_ref[:, sl] + lnb_ref[:, sl]
        bonus = jnp.sum(rk2[:, sl], axis=-1, keepdims=True) * v[:, sl]
        out_s[:, sl] = (yn + bonus) * g[:, sl]
    yg_ref[...] = out_s[...].astype(BF16)


def _rwkv(p_rwkv, shift0, wkv0, prm, batch, seq, chunk):
    t, shift_w = p_rwkv.shape
    n_heads = wkv0.shape[1]
    rw = n_heads * HEAD_DIM
    nc = seq // chunk
    row = lambda a: a.reshape(1, -1)
    const = lambda shape: pl.BlockSpec(shape, lambda b, c: (0,) * len(shape))
    head_buf = pltpu.VMEM((n_heads, chunk, HEAD_DIM), F32)
    yg, wkv, shift = pl.pallas_call(
        functools.partial(_rwkv_kernel, chunk=chunk, n_heads=n_heads),
        out_shape=(jax.ShapeDtypeStruct((t, rw), BF16),
                   jax.ShapeDtypeStruct(wkv0.shape, F32),
                   jax.ShapeDtypeStruct((batch, 1, shift_w), F32)),
        grid=(batch, nc),
        in_specs=[pl.BlockSpec((chunk, shift_w), lambda b, c: (b * nc + c, 0)),
                  pl.BlockSpec((1, 1, shift_w), lambda b, c: (b, 0, 0)),
                  pl.BlockSpec((1, n_heads, HEAD_DIM, HEAD_DIM), lambda b, c: (b, 0, 0, 0)),
                  const((1, shift_w)), const((1, rw)), const((D_DECAY, rw)), const((1, rw)),
                  const((D_AAA, rw)), const((D_GATE, rw)), const((1, rw)), const((1, rw)),
                  const((1, rw)), const((1, rw)), const((1, rw))],
        out_specs=(pl.BlockSpec((chunk, rw), lambda b, c: (b * nc + c, 0)),
                   pl.BlockSpec((1, n_heads, HEAD_DIM, HEAD_DIM), lambda b, c: (b, 0, 0, 0)),
                   pl.BlockSpec((1, 1, shift_w), lambda b, c: (b, 0, 0))),
        scratch_shapes=[pltpu.VMEM((8 + chunk, shift_w), F32),
                        head_buf, head_buf, head_buf, head_buf, head_buf, head_buf,
                        pltpu.VMEM((chunk, rw), F32)],
        compiler_params=pltpu.CompilerParams(dimension_semantics=("arbitrary", "arbitrary"),
                                             vmem_limit_bytes=VMEM_LIMIT),
        name="rwkv",
    )(p_rwkv, shift0.reshape(batch, 1, shift_w), wkv0,
      row(prm["mu_shift"]), row(prm["w0"]), prm["w2"], row(prm["a0"]), prm["a2"], prm["g2"],
      row(prm["k_k"]), row(prm["k_a"]), row(prm["r_k"]), row(prm["lnx_g"]), row(prm["lnx_b"]))
    return yg, wkv, shift.reshape(batch, shift_w)


def _mix_kernel(x_ref, pglu_ref, pg_ref, yg_ref, conv0_ref, bglu_ref, wdw_ref, bdw_ref,
                lncg_ref, lncb_ref, wpw2_ref, bpw2_ref, wouta_ref, wo_ref, n2g_ref, wr_ref, br_ref,
                h_ref, hn_ref, tope_ref, gate_ref, convn_ref, full_ref, *, nb, tc):
    c = pl.program_id(1)
    ch = x_ref.shape[1]
    hist = CONV_WIDTH - 1
    lo = HIST_ROWS - hist

    @pl.when(c == 0)
    def _():
        full_ref[:, lo:HIST_ROWS, :] = conv0_ref[...]

    @pl.when(c > 0)
    def _():
        full_ref[:, 0:HIST_ROWS, :] = full_ref[:, tc:tc + HIST_ROWS, :]

    z = pglu_ref[...] + bglu_ref[...]
    u = z[:, :ch] * _sigmoid(z[:, ch:])
    full_ref[:, HIST_ROWS:HIST_ROWS + tc, :] = u.reshape(nb, tc, ch)
    convn_ref[...] = full_ref[:, tc + lo:tc + HIST_ROWS, :]

    acc = jnp.zeros((nb, tc, ch), F32)
    for w in range(CONV_WIDTH):
        acc = acc + full_ref[:, lo + w:lo + w + tc, :] * wdw_ref[w:w + 1, :]
    cf = acc.reshape(nb * tc, ch) + bdw_ref[...]
    mu = jnp.mean(cf, axis=-1, keepdims=True)
    cc = cf - mu
    var = jnp.mean(cc * cc, axis=-1, keepdims=True)
    cn = cc * lax.rsqrt(var + LN_EPS) * lncg_ref[...] + lncb_ref[...]
    act = cn * _sigmoid(cn)
    y_b = _dot_bf16(act, wpw2_ref[...]) + bpw2_ref[...]
    y_a = jnp.dot(yg_ref[...], wouta_ref[...], preferred_element_type=F32)
    pg = pg_ref[...]
    merged = _sigmoid(pg[:, :ch]) * y_a + _sigmoid(pg[:, ch:]) * y_b
    h = x_ref[...] + _dot_bf16(merged, wo_ref[...])
    h_ref[...] = h
    hn = _rms_norm(h, n2g_ref[...])
    hn_ref[...] = hn

    logits = _dot_hi(hn, wr_ref[...]) + br_ref[...]
    lane = lax.broadcasted_iota(jnp.int32, logits.shape, 1)
    vals, idxs = [], []
    l = logits
    for _ in range(TOP_K):
        m = jnp.max(l, axis=-1, keepdims=True)
        idx = jnp.min(jnp.where(l == m, lane, LANES), axis=-1, keepdims=True)
        vals.append(m)
        idxs.append(idx)
        l = jnp.where(lane == idx, -jnp.inf, l)
    exps = [jnp.exp(vk - vals[0]) for vk in vals]
    denom = exps[0]
    for e in exps[1:]:
        denom = denom + e
    te = jnp.zeros(logits.shape, jnp.int32)
    tg = jnp.zeros(logits.shape, F32)
    for kx in range(TOP_K):
        te = jnp.where(lane == kx, idxs[kx], te)
        tg = jnp.where(lane == kx, exps[kx] / denom, tg)
    tope_ref[...] = te
    gate_ref[...] = tg


def _mix(x2d, p_glu, p_gate, yg, conv0, prm, batch, seq, nb, tc):
    t, d = x2d.shape
    ch = conv0.shape[2]
    hist = CONV_WIDTH - 1
    nc = seq // tc
    rows = nb * tc
    row = lambda a: a.reshape(1, -1)
    const = lambda shape: pl.BlockSpec(shape, lambda b, c: (0,) * len(shape))
    tok = lambda width: pl.BlockSpec((rows, width), lambda b, c: (b * nc + c, 0))
    wr = jnp.zeros((d, LANES), F32).at[:, :N_EXPERTS].set(prm["w_router"])
    br = jnp.full((1, LANES), -1e30, F32).at[0, :N_EXPERTS].set(prm["b_router"])
    return pl.pallas_call(
        functools.partial(_mix_kernel, nb=nb, tc=tc),
        out_shape=(jax.ShapeDtypeStruct((t, d), F32),
                   jax.ShapeDtypeStruct((t, d), F32),
                   jax.ShapeDtypeStruct((t, LANES), jnp.int32),
                   jax.ShapeDtypeStruct((t, LANES), F32),
                   jax.ShapeDtypeStruct((batch, hist, ch), F32)),
        grid=(batch // nb, nc),
        in_specs=[tok(d), tok(2 * ch), tok(2 * d), tok(d),
                  pl.BlockSpec((nb, hist, ch), lambda b, c: (b, 0, 0)),
                  const((1, 2 * ch)), const((CONV_WIDTH, ch)), const((1, ch)), const((1, ch)),
                  const((1, ch)), const((ch, d)), const((1, d)), const((d, d)), const((d, d)),
                  const((1, d)), const((d, LANES)), const((1, LANES))],
        out_specs=(tok(d), tok(d), tok(LANES), tok(LANES),
                   pl.BlockSpec((nb, hist, ch), lambda b, c: (b, 0, 0))),
        scratch_shapes=[pltpu.VMEM((nb, HIST_ROWS + tc, ch), F32)],
        compiler_params=pltpu.CompilerParams(dimension_semantics=("arbitrary", "arbitrary"),
                                             vmem_limit_bytes=VMEM_LIMIT),
        name="mix",
    )(x2d, p_glu, p_gate, yg, conv0,
      row(prm["b_glu"]), prm["w_dw"].reshape(CONV_WIDTH, ch), row(prm["b_dw"]),
      row(prm["ln_conv_g"]), row(prm["ln_conv_b"]), prm["w_pw2"].astype(BF16), row(prm["b_pw2"]),
      prm["w_out_a"].astype(BF16), prm["w_o"].astype(BF16), row(prm["norm2_g"]), wr, br)


def _expert_kernel(blk_e_ref, nused_ref, tok_ref, hn_hbm, wup_ref, bup_ref, wdn_ref, bdn_ref,
                   out_ref, xbuf, wup_bf, wdn_bf, sem):
    i = pl.program_id(0)
    rows = xbuf.shape[0]
    d_ff = wdn_ref.shape[1]

    @pl.when(i >= nused_ref[0])
    def _():
        out_ref[...] = jnp.zeros(out_ref.shape, F32)

    @pl.when(i < nused_ref[0])
    def _():
        def row_copy(r, tok):
            return pltpu.make_async_copy(hn_hbm.at[pl.ds(tok, 1), :], xbuf.at[pl.ds(r, 1), :], sem)

        def issue(r, carry):
            row_copy(r, tok_ref[0, 0, r]).start()
            return carry

        lax.fori_loop(0, rows, issue, 0)

        prev_e = blk_e_ref[jnp.maximum(i - 1, 0)]
        @pl.when(jnp.logical_or(i == 0, blk_e_ref[i] != prev_e))
        def _():
            wup_bf[...] = wup_ref[0].astype(BF16)
            wdn_bf[...] = wdn_ref[0].astype(BF16)

        def drain(r, carry):
            row_copy(r, 0).wait()
            return carry

        lax.fori_loop(0, rows, drain, 0)

        hb = jnp.dot(xbuf[...].astype(BF16), wup_bf[...], preferred_element_type=F32) + bup_ref[0]
        glu = jnp.minimum(hb[:, :d_ff], SWIGLU_LIMIT)
        lin = jnp.clip(hb[:, d_ff:], -SWIGLU_LIMIT, SWIGLU_LIMIT)
        act = glu * _sigmoid(SWIGLU_ALPHA * glu) * (lin + 1.0)
        out_ref[...] = jnp.dot(act.astype(BF16), wdn_bf[...], preferred_element_type=F32) + bdn_ref[0]


def _experts(hn, slot_tok, blk_e, n_used, w_up, b_up, w_down, b_down):
    t, d = hn.shape
    n_blocks = blk_e.shape[0]
    rows = EXPERT_BLOCK
    n_exp, _, ff2 = w_up.shape
    d_ff = w_down.shape[1]
    grid_spec = pltpu.PrefetchScalarGridSpec(
        num_scalar_prefetch=2,
        grid=(n_blocks,),
        in_specs=[pl.BlockSpec((1, 1, rows), lambda i, be, nu: (i, 0, 0), memory_space=pltpu.SMEM),
                  pl.BlockSpec(memory_space=pl.ANY),
                  pl.BlockSpec((1, d, ff2), lambda i, be, nu: (be[i], 0, 0)),
                  pl.BlockSpec((1, 1, ff2), lambda i, be, nu: (be[i], 0, 0)),
                  pl.BlockSpec((1, d_ff, d), lambda i, be, nu: (be[i], 0, 0)),
                  pl.BlockSpec((1, 1, d), lambda i, be, nu: (be[i], 0, 0))],
        out_specs=pl.BlockSpec((rows, d), lambda i, be, nu: (i, 0)),
        scratch_shapes=[pltpu.VMEM((rows, d), F32),
                        pltpu.VMEM((d, ff2), BF16),
                        pltpu.VMEM((d_ff, d), BF16),
                        pltpu.SemaphoreType.DMA],
    )
    return pl.pallas_call(
        _expert_kernel,
        out_shape=jax.ShapeDtypeStruct((n_blocks * rows, d), F32),
        grid_spec=grid_spec,
        compiler_params=pltpu.CompilerParams(dimension_semantics=("arbitrary",),
                                             vmem_limit_bytes=VMEM_LIMIT),
        name="experts",
    )(blk_e, n_used, slot_tok.reshape(n_blocks, 1, rows), hn,
      w_up, b_up.reshape(n_exp, 1, ff2), w_down, b_down.reshape(n_exp, 1, d))


def _combine_kernel(dest_ref, gate_ref, h_ref, ys_hbm, g_ref, out_ref, buf, sem):
    rows = h_ref.shape[0]

    def row_copy(r, kx, src_row):
        return pltpu.make_async_copy(ys_hbm.at[pl.ds(src_row, 1), :],
                                     buf.at[kx, pl.ds(r, 1), :], sem)

    def issue(r, carry):
        for kx in range(TOP_K):
            row_copy(r, kx, dest_ref[0, 0, r * TOP_K + kx]).start()
        return carry

    lax.fori_loop(0, rows, issue, 0)

    def drain(r, carry):
        for kx in range(TOP_K):
            row_copy(r, kx, 0).wait()
        return carry

    lax.fori_loop(0, rows, drain, 0)

    gates = gate_ref[...]
    y = buf[0] * gates[:, 0:1]
    for kx in range(1, TOP_K):
        y = y + buf[kx] * gates[:, kx:kx + 1]
    out_ref[...] = _rms_norm(h_ref[...] + y, g_ref[...])


def _combine(dest, gates, h, ys, norm_g):
    t, d = h.shape
    rows = ROW_BLOCK
    nblk = t // rows
    return pl.pallas_call(
        _combine_kernel,
        out_shape=jax.ShapeDtypeStruct((t, d), F32),
        grid=(nblk,),
        in_specs=[pl.BlockSpec((1, 1, rows * TOP_K), lambda i: (i, 0, 0), memory_space=pltpu.SMEM),
                  pl.BlockSpec((rows, LANES), lambda i: (i, 0)),
                  pl.BlockSpec((rows, d), lambda i: (i, 0)),
                  pl.BlockSpec(memory_space=pl.ANY),
                  pl.BlockSpec((1, d), lambda i: (0, 0))],
        out_specs=pl.BlockSpec((rows, d), lambda i: (i, 0)),
        scratch_shapes=[pltpu.VMEM((TOP_K, rows, d), F32), pltpu.SemaphoreType.DMA],
        compiler_params=pltpu.CompilerParams(dimension_semantics=("arbitrary",),
                                             vmem_limit_bytes=VMEM_LIMIT),
        name="combine",
    )(dest.reshape(nblk, 1, rows * TOP_K), gates, h, ys, norm_g.reshape(1, d))


def _routing_tables(top_e):
    t = top_e.shape[0]
    n_assign = t * TOP_K
    flat_e = top_e.reshape(-1)
    onehot = (flat_e[:, None] == jnp.arange(N_EXPERTS, dtype=jnp.int32)[None, :]).astype(jnp.int32)
    csum = jnp.cumsum(onehot, axis=0)
    rank = jnp.sum((csum - onehot) * onehot, axis=1)
    counts = csum[-1]
    padded = (counts + EXPERT_BLOCK - 1) // EXPERT_BLOCK * EXPERT_BLOCK
    pad_end = jnp.cumsum(padded)
    pad_start = pad_end - padded
    dest = (pad_start[flat_e] + rank).astype(jnp.int32)
    n_blocks = -(-n_assign // EXPERT_BLOCK) + N_EXPERTS
    flat_tok = jnp.arange(n_assign, dtype=jnp.int32) // TOP_K
    slot_tok = jnp.zeros((n_blocks * EXPERT_BLOCK,), jnp.int32).at[dest].set(flat_tok)
    blk_start = jnp.arange(n_blocks, dtype=jnp.int32) * EXPERT_BLOCK
    blk_e = jnp.minimum(jnp.searchsorted(pad_end, blk_start, side="right"),
                        N_EXPERTS - 1).astype(jnp.int32)
    n_used = (pad_end[-1:] // EXPERT_BLOCK).astype(jnp.int32)
    return dest, slot_tok, blk_e, n_used


def _mixers(x, wkv0, shift0, conv0, prm, w_in_bf16, chunk, nb, tc):
    batch, seq, d = x.shape
    shift_w = shift0.shape[1]
    glu_w = 2 * conv0.shape[2]
    x2d = x.reshape(batch * seq, d)
    p_rwkv, p_glu, p_gate = _in_proj(x2d, prm["norm1_g"], w_in_bf16, shift_w, glu_w)
    yg, wkv_new, shift_new = _rwkv(p_rwkv, shift0, wkv0, prm, batch, seq, chunk)
    h, hn, top_e, gates, conv_new = _mix(x2d, p_glu, p_gate, yg, conv0, prm, batch, seq, nb, tc)
    return h, hn, top_e, gates, wkv_new, shift_new, conv_new


def kernel(x_prompt, x_sample, state_wkv, state_shift, state_conv, norm1_g, w_in, b_glu, mu_shift, w0, w2, a0, a2, g2, k_k, k_a, r_k, lnx_g, lnx_b, w_out_a, w_dw, b_dw, ln_conv_g, ln_conv_b, w_pw2, b_pw2, w_o, norm2_g, w_router, b_router, w_up, b_up, w_down, b_down, norm_f_g):
    prm = dict(norm1_g=norm1_g, b_glu=b_glu, mu_shift=mu_shift, w0=w0, w2=w2, a0=a0, a2=a2,
               g2=g2, k_k=k_k, k_a=k_a, r_k=r_k, lnx_g=lnx_g, lnx_b=lnx_b, w_out_a=w_out_a,
               w_dw=w_dw, b_dw=b_dw, ln_conv_g=ln_conv_g, ln_conv_b=ln_conv_b, w_pw2=w_pw2,
               b_pw2=b_pw2, w_o=w_o, norm2_g=norm2_g, w_router=w_router, b_router=b_router)
    bp, lp, d = x_prompt.shape
    bs, ls, _ = x_sample.shape
    n_heads = state_wkv.shape[1]
    shift_w = state_shift.shape[1]
    ch = state_conv.shape[2]
    w_in_bf16 = w_in.astype(BF16)

    zeros_wkv = jnp.zeros((bp, n_heads, HEAD_DIM, HEAD_DIM), F32)
    zeros_shift = jnp.zeros((bp, shift_w), F32)
    zeros_conv = jnp.zeros((bp, CONV_WIDTH - 1, ch), F32)
    hp, hnp, tep, gp, wkv_p, shift_p, conv_p = _mixers(
        x_prompt, zeros_wkv, zeros_shift, zeros_conv, prm, w_in_bf16,
        chunk=min(64, lp), nb=1, tc=min(ROW_BLOCK, lp))
    hs, hns, tes, gs, wkv_s, shift_s, conv_s = _mixers(
        x_sample, state_wkv, state_shift, state_conv, prm, w_in_bf16,
        chunk=ls, nb=ROW_BLOCK // ls, tc=ls)

    h = jnp.concatenate([hp, hs], axis=0)
    hn = jnp.concatenate([hnp, hns], axis=0)
    top_e = jnp.concatenate([tep, tes], axis=0)[:, :TOP_K]
    gates = jnp.concatenate([gp, gs], axis=0)
    dest, slot_tok, blk_e, n_used = _routing_tables(top_e)
    ys = _experts(hn, slot_tok, blk_e, n_used, w_up, b_up, w_down, b_down)
    y = _combine(dest, gates, h, ys, norm_f_g)
    tp = bp * lp
    return (y[:tp].reshape(bp, lp, d), y[tp:].reshape(bs, ls, d),
            wkv_p, shift_p, conv_p, wkv_s, shift_s, conv_s)
```

```python
import functools

import jax
import jax.numpy as jnp
from jax import lax
from jax.experimental import pallas as pl
from jax.experimental.pallas import tpu as pltpu

F32 = jnp.float32
BF16 = jnp.bfloat16
HIGHEST = lax.Precision.HIGHEST

HEAD_DIM = 64
D_DECAY = 64
D_AAA = 64
D_GATE = 128
CONV_WIDTH = 31
N_EXPERTS = 32
TOP_K = 4
SWIGLU_LIMIT = 7.0
SWIGLU_ALPHA = 1.702
RMS_EPS = 1e-6
LN_EPS = 1e-5
GN_EPS = 64e-5

LANES = 128
HIST_ROWS = 32
ROW_BLOCK = 256
EXPERT_BLOCK = 256
GROUP_HEADS = 4
GROUP_W = GROUP_HEADS * HEAD_DIM
VMEM_LIMIT = 56 * 1024 * 1024


def _dot_hi(a, b):
    return jnp.dot(a, b, precision=HIGHEST, preferred_element_type=F32)


def _dot_bf16(a, b):
    return jnp.dot(a.astype(BF16), b, preferred_element_type=F32)


def _sigmoid(x):
    return 1.0 / (1.0 + jnp.exp(-x))


def _rms_norm(x, g):
    return x * lax.rsqrt(jnp.mean(x * x, axis=-1, keepdims=True) + RMS_EPS) * g


def _inproj_kernel(x_ref, g_ref, w_ref, o_rwkv, o_glu, o_gate, *, shift_w, glu_w):
    xb = _rms_norm(x_ref[...], g_ref[...]).astype(BF16)
    o_rwkv[...] = jnp.dot(xb, w_ref[:, :shift_w], preferred_element_type=F32)
    o_glu[...] = jnp.dot(xb, w_ref[:, shift_w:shift_w + glu_w], preferred_element_type=F32)
    o_gate[...] = jnp.dot(xb, w_ref[:, shift_w + glu_w:], preferred_element_type=F32)


def _in_proj(x2d, norm_g, w_in_bf16, shift_w, glu_w):
    t, d = x2d.shape
    in_w = w_in_bf16.shape[1]
    gate_w = in_w - shift_w - glu_w
    tm = ROW_BLOCK
    return pl.pallas_call(
        functools.partial(_inproj_kernel, shift_w=shift_w, glu_w=glu_w),
        out_shape=(jax.ShapeDtypeStruct((t, shift_w), F32),
                   jax.ShapeDtypeStruct((t, glu_w), F32),
                   jax.ShapeDtypeStruct((t, gate_w), F32)),
        grid=(t // tm,),
        in_specs=[pl.BlockSpec((tm, d), lambda i: (i, 0)),
                  pl.BlockSpec((1, d), lambda i: (0, 0)),
                  pl.BlockSpec((d, in_w), lambda i: (0, 0), pipeline_mode=pl.Buffered(1))],
        out_specs=(pl.BlockSpec((tm, shift_w), lambda i: (i, 0)),
                   pl.BlockSpec((tm, glu_w), lambda i: (i, 0)),
                   pl.BlockSpec((tm, gate_w), lambda i: (i, 0))),
        compiler_params=pltpu.CompilerParams(dimension_semantics=("arbitrary",),
                                             vmem_limit_bytes=VMEM_LIMIT),
        name="in_proj",
    )(x2d, norm_g.reshape(1, d), w_in_bf16)


NN = (((1,), (0,)), ((), ()))
NT = (((1,), (1,)), ((), ()))
TN = (((0,), (0,)), ((), ()))


def _split2(x):
    hi = x.astype(BF16).astype(F32)
    return hi, x - hi


def _parts(x):
    hi, lo = _split2(x)
    return hi.astype(BF16), lo.astype(BF16)


def _mm(a, b_parts, dims=NN):
    m = a.shape[0]
    a_hi, a_lo = _split2(a)
    a_st = jnp.concatenate([a_hi, a_lo], axis=0).astype(BF16)
    b_hi, b_lo = b_parts
    r = lax.dot_general(a_st, b_hi, dims, preferred_element_type=F32)
    return r[:m] + r[m:] + lax.dot_general(a_st[:m], b_lo, dims, preferred_element_type=F32)


def _block_diag_parts(x, mask):
    hi, lo = _split2(x)
    tile = lambda q: jnp.where(mask, jnp.concatenate([q] * GROUP_HEADS, axis=0), 0.0).astype(BF16)
    return tile(hi), tile(lo)


def _rwkv_kernel(p_ref, shift0_ref, wkv0_ref, mu_ref, w0_ref, w2_ref, a0_ref, a2_ref, g2_ref,
                 kk_ref, ka_ref, rk_ref, lng_ref, lnb_ref,
                 yg_ref, wkv_ref, shift_ref,
                 pbuf, state, *, chunk, n_heads):
    c = pl.program_id(1)
    n_chunks = pl.num_programs(1)
    rw = n_heads * HEAD_DIM
    n_groups = rw // GROUP_W
    cw = GROUP_HEADS * chunk
    log2c = chunk.bit_length() - 1
    n_doublings = log2c - 1

    @pl.when(c == 0)
    def _():
        pbuf[7:8, :] = shift0_ref[0]
        state[...] = jnp.zeros(state.shape, F32)
        for hd in range(n_heads):
            g, j = divmod(hd, GROUP_HEADS)
            blk = slice(j * HEAD_DIM, (j + 1) * HEAD_DIM)
            state[g, blk, blk] = wkv0_ref[0, hd]

    p = p_ref[...]
    pbuf[8:8 + chunk, :] = p
    prev = pbuf[7:7 + chunk, :]
    last = p[chunk - 1:chunk, :]
    pbuf[7:8, :] = last
    shift_ref[0] = last

    h = p + (prev - p) * mu_ref[...]
    r = h[:, 0:rw]
    k = h[:, rw:2 * rw]
    v = h[:, 2 * rw:3 * rw]
    o = 3 * rw
    wd = h[:, o:o + D_DECAY]
    ad = h[:, o + D_DECAY:o + D_DECAY + D_AAA]
    gd = h[:, o + D_DECAY + D_AAA:o + D_DECAY + D_AAA + D_GATE]

    def iota(shape, dim):
        return lax.broadcasted_iota(jnp.int32, shape, dim)

    tri = (iota((chunk, chunk), 0) >= iota((chunk, chunk), 1)).astype(BF16)
    t4 = iota((chunk, cw), 0)
    s4 = jnp.bitwise_and(iota((chunk, cw), 1), chunk - 1)
    strict4 = t4 > s4
    incl4 = t4 >= s4
    eye4 = (t4 == s4).astype(F32)
    mask_v = (iota((cw, GROUP_W), 0) >> log2c) == (iota((cw, GROUP_W), 1) >> 6)
    mask_p = (iota((cw, cw), 0) >> log2c) == (iota((cw, cw), 1) >> log2c)
    mask_s = (iota((GROUP_W, GROUP_W), 0) >> 6) == (iota((GROUP_W, GROUP_W), 1) >> 6)
    ones_bd = mask_s.astype(BF16)

    def seg_sum(x):
        xs = jnp.concatenate([x[:, g * GROUP_W:(g + 1) * GROUP_W] for g in range(n_groups)], axis=0)
        hi, rest = _split2(xs)
        mid, lo = _split2(rest)
        st = jnp.concatenate([hi, mid, lo], axis=0).astype(BF16)
        rr = jnp.dot(st, ones_bd, preferred_element_type=F32)
        m = xs.shape[0]
        s = rr[:m] + rr[m:2 * m] + rr[2 * m:]
        return jnp.concatenate([s[g * chunk:(g + 1) * chunk] for g in range(n_groups)], axis=1)

    z = -(w0_ref[...] + _mm(jnp.tanh(wd), _parts(w2_ref[...])))
    softplus = jnp.maximum(z, 0.0) + jnp.log(1.0 + jnp.exp(-jnp.abs(z)))
    logw = -jnp.exp(-softplus - 0.5)
    l_hi, l_rest = _split2(logw)
    l_mid, l_lo = _split2(l_rest)
    cs = (jnp.dot(tri, l_hi.astype(BF16), preferred_element_type=F32)
          + jnp.dot(tri, l_mid.astype(BF16), preferred_element_type=F32)
          + jnp.dot(tri, l_lo.astype(BF16), preferred_element_type=F32))
    w_incl = jnp.exp(cs)
    w_excl = jnp.exp(cs - logw)
    w_inv = jnp.exp(-cs)
    w_last = w_incl[chunk - 1:chunk, :]

    a = _sigmoid(a0_ref[...] + _mm(ad, _parts(a2_ref[...])))
    g = _mm(_sigmoid(gd), _parts(g2_ref[...]))
    kk = k * kk_ref[...]
    kk = kk * lax.rsqrt(jnp.maximum(seg_sum(kk * kk), 1e-24))
    k2 = k * (1.0 + (a - 1.0) * ka_ref[...])
    at_f = kk * w_excl
    rt_f = r * w_incl
    kh_f = k2 * w_inv
    bh_f = kk * a * w_inv
    khw_f = kh_f * w_last
    bhw_f = bh_f * w_last

    groups = range(n_groups)
    gsl = [slice(gi * GROUP_W, (gi + 1) * GROUP_W) for gi in groups]
    v_g = [v[:, gs] for gs in gsl]
    ar = [jnp.concatenate([at_f[:, gs], rt_f[:, gs]], axis=0) for gs in gsl]
    gk = [_mm(ar[gi], _block_diag_parts(kh_f[:, gsl[gi]], mask_v), NT) for gi in groups]
    gb = [_mm(ar[gi], _block_diag_parts(bh_f[:, gsl[gi]], mask_v), NT) for gi in groups]
    s_old = [state[gi] for gi in groups]
    x0 = [_mm(ar[gi], _parts(s_old[gi]), NT) for gi in groups]
    a_ak = [jnp.where(strict4, gk[gi][:chunk], 0.0) for gi in groups]
    a_rk = [jnp.where(incl4, gk[gi][chunk:], 0.0) for gi in groups]
    a_ab = [jnp.where(strict4, gb[gi][:chunk], 0.0) for gi in groups]
    a_rb = [jnp.where(incl4, gb[gi][chunk:], 0.0) for gi in groups]
    av = [_mm(jnp.concatenate([a_ak[gi], a_rk[gi]], axis=0), _block_diag_parts(v_g[gi], mask_v))
          for gi in groups]
    rhs = [x0[gi][:chunk] + av[gi][:chunk] for gi in groups]
    nmat = [-a_ab[gi] for gi in groups]
    tinv = [eye4 + nmat[gi] for gi in groups]
    pw = [_mm(nmat[gi], _block_diag_parts(nmat[gi], mask_p)) for gi in groups]
    for i in range(n_doublings):
        pbd = [_block_diag_parts(pw[gi], mask_p) for gi in groups]
        if i == n_doublings - 1:
            tinv = [tinv[gi] + _mm(tinv[gi], pbd[gi]) for gi in groups]
        else:
            both = [_mm(jnp.concatenate([pw[gi], tinv[gi]], axis=0), pbd[gi]) for gi in groups]
            pw = [both[gi][:chunk] for gi in groups]
            tinv = [tinv[gi] + both[gi][chunk:] for gi in groups]
    u = [_mm(tinv[gi], _block_diag_parts(rhs[gi], mask_v)) for gi in groups]
    ys = [x0[gi][chunk:] + av[gi][chunk:] - _mm(a_rb[gi], _block_diag_parts(u[gi], mask_v))
          for gi in groups]
    for gi in groups:
        gs = gsl[gi]
        l_hi, l_lo = _split2(jnp.concatenate([v_g[gi], -u[gi]], axis=0))
        r_hi, r_lo = _split2(jnp.concatenate([khw_f[:, gs], bhw_f[:, gs]], axis=0))
        lhs = jnp.concatenate([l_hi, l_lo, l_hi], axis=0).astype(BF16)
        rhs3 = jnp.concatenate([r_hi, r_hi, r_lo], axis=0).astype(BF16)
        delta = lax.dot_general(lhs, rhs3, TN, preferred_element_type=F32)
        state[gi] = s_old[gi] * w_last[:, gs] + jnp.where(mask_s, delta, 0.0)

    y = jnp.concatenate(ys, axis=1)
    inv_n = 1.0 / HEAD_DIM
    yc = y - seg_sum(y) * inv_n
    var = seg_sum(yc * yc) * inv_n
    yn = yc * lax.rsqrt(var + GN_EPS) * lng_ref[...] + lnb_ref[...]
    bonus = seg_sum(r * k2 * rk_ref[...]) * v
    yg_ref[...] = ((yn + bonus) * g).astype(BF16)

    @pl.when(c == n_chunks - 1)
    def _():
        for hd in range(n_heads):
            gq, j = divmod(hd, GROUP_HEADS)
            blk = slice(j * HEAD_DIM, (j + 1) * HEAD_DIM)
            wkv_ref[0, hd] = state[gq, blk, blk]


def _rwkv(p_rwkv, shift0, wkv0, prm, batch, seq, chunk):
    t, shift_w = p_rwkv.shape
    n_heads = wkv0.shape[1]
    rw = n_heads * HEAD_DIM
    assert chunk >= 4 and chunk & (chunk - 1) == 0 and seq % chunk == 0 and rw % GROUP_W == 0
    nc = seq // chunk
    row = lambda a: a.reshape(1, -1)
    const = lambda shape: pl.BlockSpec(shape, lambda b, c: (0,) * len(shape))
    yg, wkv, shift = pl.pallas_call(
        functools.partial(_rwkv_kernel, chunk=chunk, n_heads=n_heads),
        out_shape=(jax.ShapeDtypeStruct((t, rw), BF16),
                   jax.ShapeDtypeStruct(wkv0.shape, F32),
                   jax.ShapeDtypeStruct((batch, 1, shift_w), F32)),
        grid=(batch, nc),
        in_specs=[pl.BlockSpec((chunk, shift_w), lambda b, c: (b * nc + c, 0)),
                  pl.BlockSpec((1, 1, shift_w), lambda b, c: (b, 0, 0)),
                  pl.BlockSpec((1, n_heads, HEAD_DIM, HEAD_DIM), lambda b, c: (b, 0, 0, 0)),
                  const((1, shift_w)), const((1, rw)), const((D_DECAY, rw)), const((1, rw)),
                  const((D_AAA, rw)), const((D_GATE, rw)), const((1, rw)), const((1, rw)),
                  const((1, rw)), const((1, rw)), const((1, rw))],
        out_specs=(pl.BlockSpec((chunk, rw), lambda b, c: (b * nc + c, 0)),
                   pl.BlockSpec((1, n_heads, HEAD_DIM, HEAD_DIM), lambda b, c: (b, 0, 0, 0)),
                   pl.BlockSpec((1, 1, shift_w), lambda b, c: (b, 0, 0))),
        scratch_shapes=[pltpu.VMEM((8 + chunk, shift_w), F32),
                        pltpu.VMEM((rw // GROUP_W, GROUP_W, GROUP_W), F32)],
        compiler_params=pltpu.CompilerParams(dimension_semantics=("arbitrary", "arbitrary"),
                                             vmem_limit_bytes=VMEM_LIMIT),
        name="rwkv",
    )(p_rwkv, shift0.reshape(batch, 1, shift_w), wkv0,
      row(prm["mu_shift"]), row(prm["w0"]), prm["w2"], row(prm["a0"]), prm["a2"], prm["g2"],
      row(prm["k_k"]), row(prm["k_a"]), row(prm["r_k"]), row(prm["lnx_g"]), row(prm["lnx_b"]))
    return yg, wkv, shift.reshape(batch, shift_w)


def _mix_kernel(x_ref, pglu_ref, pg_ref, yg_ref, conv0_ref, bglu_ref, wdw_ref, bdw_ref,
                lncg_ref, lncb_ref, wpw2_ref, bpw2_ref, wouta_ref, wo_ref, n2g_ref, wr_ref, br_ref,
                h_ref, hn_ref, tope_ref, gate_ref, convn_ref, full_ref, *, nb, tc):
    c = pl.program_id(1)
    ch = x_ref.shape[1]
    hist = CONV_WIDTH - 1
    lo = HIST_ROWS - hist

    @pl.when(c == 0)
    def _():
        full_ref[:, lo:HIST_ROWS, :] = conv0_ref[...]

    @pl.when(c > 0)
    def _():
        full_ref[:, 0:HIST_ROWS, :] = full_ref[:, tc:tc + HIST_ROWS, :]

    z = pglu_ref[...] + bglu_ref[...]
    u = z[:, :ch] * _sigmoid(z[:, ch:])
    full_ref[:, HIST_ROWS:HIST_ROWS + tc, :] = u.reshape(nb, tc, ch)
    convn_ref[...] = full_ref[:, tc + lo:tc + HIST_ROWS, :]

    acc = jnp.zeros((nb, tc, ch), F32)
    for w in range(CONV_WIDTH):
        acc = acc + full_ref[:, lo + w:lo + w + tc, :] * wdw_ref[w:w + 1, :]
    cf = acc.reshape(nb * tc, ch) + bdw_ref[...]
    mu = jnp.mean(cf, axis=-1, keepdims=True)
    cc = cf - mu
    var = jnp.mean(cc * cc, axis=-1, keepdims=True)
    cn = cc * lax.rsqrt(var + LN_EPS) * lncg_ref[...] + lncb_ref[...]
    act = cn * _sigmoid(cn)
    y_b = _dot_bf16(act, wpw2_ref[...]) + bpw2_ref[...]
    y_a = jnp.dot(yg_ref[...], wouta_ref[...], preferred_element_type=F32)
    pg = pg_ref[...]
    merged = _sigmoid(pg[:, :ch]) * y_a + _sigmoid(pg[:, ch:]) * y_b
    h = x_ref[...] + _dot_bf16(merged, wo_ref[...])
    h_ref[...] = h
    hn = _rms_norm(h, n2g_ref[...])
    hn_ref[...] = hn

    logits = _dot_hi(hn, wr_ref[...]) + br_ref[...]
    lane = lax.broadcasted_iota(jnp.int32, logits.shape, 1)
    vals, idxs = [], []
    l = logits
    for _ in range(TOP_K):
        m = jnp.max(l, axis=-1, keepdims=True)
        idx = jnp.min(jnp.where(l == m, lane, LANES), axis=-1, keepdims=True)
        vals.append(m)
        idxs.append(idx)
        l = jnp.where(lane == idx, -jnp.inf, l)
    exps = [jnp.exp(vk - vals[0]) for vk in vals]
    denom = exps[0]
    for e in exps[1:]:
        denom = denom + e
    te = jnp.zeros(logits.shape, jnp.int32)
    tg = jnp.zeros(logits.shape, F32)
    for kx in range(TOP_K):
        te = jnp.where(lane == kx, idxs[kx], te)
        tg = jnp.where(lane == kx, exps[kx] / denom, tg)
    tope_ref[...] = te
    gate_ref[...] = tg


def _mix(x2d, p_glu, p_gate, yg, conv0, prm, batch, seq, nb, tc):
    t, d = x2d.shape
    ch = conv0.shape[2]
    hist = CONV_WIDTH - 1
    nc = seq // tc
    rows = nb * tc
    row = lambda a: a.reshape(1, -1)
    const = lambda shape: pl.BlockSpec(shape, lambda b, c: (0,) * len(shape))
    tok = lambda width: pl.BlockSpec((rows, width), lambda b, c: (b * nc + c, 0))
    wr = jnp.zeros((d, LANES), F32).at[:, :N_EXPERTS].set(prm["w_router"])
    br = jnp.full((1, LANES), -1e30, F32).at[0, :N_EXPERTS].set(prm["b_router"])
    return pl.pallas_call(
        functools.partial(_mix_kernel, nb=nb, tc=tc),
        out_shape=(jax.ShapeDtypeStruct((t, d), F32),
                   jax.ShapeDtypeStruct((t, d), F32),
                   jax.ShapeDtypeStruct((t, LANES), jnp.int32),
                   jax.ShapeDtypeStruct((t, LANES), F32),
                   jax.ShapeDtypeStruct((batch, hist, ch), F32)),
        grid=(batch // nb, nc),
        in_specs=[tok(d), tok(2 * ch), tok(2 * d), tok(d),
                  pl.BlockSpec((nb, hist, ch), lambda b, c: (b, 0, 0)),
                  const((1, 2 * ch)), const((CONV_WIDTH, ch)), const((1, ch)), const((1, ch)),
                  const((1, ch)), const((ch, d)), const((1, d)), const((d, d)), const((d, d)),
                  const((1, d)), const((d, LANES)), const((1, LANES))],
        out_specs=(tok(d), tok(d), tok(LANES), tok(LANES),
                   pl.BlockSpec((nb, hist, ch), lambda b, c: (b, 0, 0))),
        scratch_shapes=[pltpu.VMEM((nb, HIST_ROWS + tc, ch), F32)],
        compiler_params=pltpu.CompilerParams(dimension_semantics=("arbitrary", "arbitrary"),
                                             vmem_limit_bytes=VMEM_LIMIT),
        name="mix",
    )(x2d, p_glu, p_gate, yg, conv0,
      row(prm["b_glu"]), prm["w_dw"].reshape(CONV_WIDTH, ch), row(prm["b_dw"]),
      row(prm["ln_conv_g"]), row(prm["ln_conv_b"]), prm["w_pw2"].astype(BF16), row(prm["b_pw2"]),
      prm["w_out_a"].astype(BF16), prm["w_o"].astype(BF16), row(prm["norm2_g"]), wr, br)


def _expert_kernel(blk_e_ref, nused_ref, tok_ref, hn_hbm, wup_ref, bup_ref, wdn_ref, bdn_ref,
                   out_ref, xbuf, wup_bf, wdn_bf, sem):
    i = pl.program_id(0)
    rows = xbuf.shape[0]
    d_ff = wdn_ref.shape[1]

    @pl.when(i >= nused_ref[0])
    def _():
        out_ref[...] = jnp.zeros(out_ref.shape, F32)

    @pl.when(i < nused_ref[0])
    def _():
        def row_copy(r, tok):
            return pltpu.make_async_copy(hn_hbm.at[pl.ds(tok, 1), :], xbuf.at[pl.ds(r, 1), :], sem)

        def issue(r, carry):
            row_copy(r, tok_ref[0, 0, r]).start()
            return carry

        lax.fori_loop(0, rows, issue, 0)

        prev_e = blk_e_ref[jnp.maximum(i - 1, 0)]
        @pl.when(jnp.logical_or(i == 0, blk_e_ref[i] != prev_e))
        def _():
            wup_bf[...] = wup_ref[0].astype(BF16)
            wdn_bf[...] = wdn_ref[0].astype(BF16)

        def drain(r, carry):
            row_copy(r, 0).wait()
            return carry

        lax.fori_loop(0, rows, drain, 0)

        hb = jnp.dot(xbuf[...].astype(BF16), wup_bf[...], preferred_element_type=F32) + bup_ref[0]
        glu = jnp.minimum(hb[:, :d_ff], SWIGLU_LIMIT)
        lin = jnp.clip(hb[:, d_ff:], -SWIGLU_LIMIT, SWIGLU_LIMIT)
        act = glu * _sigmoid(SWIGLU_ALPHA * glu) * (lin + 1.0)
        out_ref[...] = jnp.dot(act.astype(BF16), wdn_bf[...], preferred_element_type=F32) + bdn_ref[0]


def _experts(hn, slot_tok, blk_e, n_used, w_up, b_up, w_down, b_down):
    t, d = hn.shape
    n_blocks = blk_e.shape[0]
    rows = EXPERT_BLOCK
    n_exp, _, ff2 = w_up.shape
    d_ff = w_down.shape[1]
    grid_spec = pltpu.PrefetchScalarGridSpec(
        num_scalar_prefetch=2,
        grid=(n_blocks,),
        in_specs=[pl.BlockSpec((1, 1, rows), lambda i, be, nu: (i, 0, 0), memory_space=pltpu.SMEM),
                  pl.BlockSpec(memory_space=pl.ANY),
                  pl.BlockSpec((1, d, ff2), lambda i, be, nu: (be[i], 0, 0)),
                  pl.BlockSpec((1, 1, ff2), lambda i, be, nu: (be[i], 0, 0)),
                  pl.BlockSpec((1, d_ff, d), lambda i, be, nu: (be[i], 0, 0)),
                  pl.BlockSpec((1, 1, d), lambda i, be, nu: (be[i], 0, 0))],
        out_specs=pl.BlockSpec((rows, d), lambda i, be, nu: (i, 0)),
        scratch_shapes=[pltpu.VMEM((rows, d), F32),
                        pltpu.VMEM((d, ff2), BF16),
                        pltpu.VMEM((d_ff, d), BF16),
                        pltpu.SemaphoreType.DMA],
    )
    return pl.pallas_call(
        _expert_kernel,
        out_shape=jax.ShapeDtypeStruct((n_blocks * rows, d), F32),
        grid_spec=grid_spec,
        compiler_params=pltpu.CompilerParams(dimension_semantics=("arbitrary",),
                                             vmem_limit_bytes=VMEM_LIMIT),
        name="experts",
    )(blk_e, n_used, slot_tok.reshape(n_blocks, 1, rows), hn,
      w_up, b_up.reshape(n_exp, 1, ff2), w_down, b_down.reshape(n_exp, 1, d))


def _combine_kernel(dest_ref, gate_ref, h_ref, ys_hbm, g_ref, out_ref, buf, sem):
    rows = h_ref.shape[0]

    def row_copy(r, kx, src_row):
        return pltpu.make_async_copy(ys_hbm.at[pl.ds(src_row, 1), :],
                                     buf.at[kx, pl.ds(r, 1), :], sem)

    def issue(r, carry):
        for kx in range(TOP_K):
            row_copy(r, kx, dest_ref[0, 0, r * TOP_K + kx]).start()
        return carry

    lax.fori_loop(0, rows, issue, 0)

    def drain(r, carry):
        for kx in range(TOP_K):
            row_copy(r, kx, 0).wait()
        return carry

    lax.fori_loop(0, rows, drain, 0)

    gates = gate_ref[...]
    y = buf[0] * gates[:, 0:1]
    for kx in range(1, TOP_K):
        y = y + buf[kx] * gates[:, kx:kx + 1]
    out_ref[...] = _rms_norm(h_ref[...] + y, g_ref[...])


def _combine(dest, gates, h, ys, norm_g):
    t, d = h.shape
    rows = ROW_BLOCK
    nblk = t // rows
    return pl.pallas_call(
        _combine_kernel,
        out_shape=jax.ShapeDtypeStruct((t, d), F32),
        grid=(nblk,),
        in_specs=[pl.BlockSpec((1, 1, rows * TOP_K), lambda i: (i, 0, 0), memory_space=pltpu.SMEM),
                  pl.BlockSpec((rows, LANES), lambda i: (i, 0)),
                  pl.BlockSpec((rows, d), lambda i: (i, 0)),
                  pl.BlockSpec(memory_space=pl.ANY),
                  pl.BlockSpec((1, d), lambda i: (0, 0))],
        out_specs=pl.BlockSpec((rows, d), lambda i: (i, 0)),
        scratch_shapes=[pltpu.VMEM((TOP_K, rows, d), F32), pltpu.SemaphoreType.DMA],
        compiler_params=pltpu.CompilerParams(dimension_semantics=("arbitrary",),
                                             vmem_limit_bytes=VMEM_LIMIT),
        name="combine",
    )(dest.reshape(nblk, 1, rows * TOP_K), gates, h, ys, norm_g.reshape(1, d))


def _routing_tables(top_e):
    t = top_e.shape[0]
    n_assign = t * TOP_K
    flat_e = top_e.reshape(-1)
    onehot = (flat_e[:, None] == jnp.arange(N_EXPERTS, dtype=jnp.int32)[None, :]).astype(jnp.int32)
    csum = jnp.cumsum(onehot, axis=0)
    rank = jnp.sum((csum - onehot) * onehot, axis=1)
    counts = csum[-1]
    padded = (counts + EXPERT_BLOCK - 1) // EXPERT_BLOCK * EXPERT_BLOCK
    pad_end = jnp.cumsum(padded)
    pad_start = pad_end - padded
    dest = (pad_start[flat_e] + rank).astype(jnp.int32)
    n_blocks = -(-n_assign // EXPERT_BLOCK) + N_EXPERTS
    flat_tok = jnp.arange(n_assign, dtype=jnp.int32) // TOP_K
    slot_tok = jnp.zeros((n_blocks * EXPERT_BLOCK,), jnp.int32).at[dest].set(flat_tok)
    blk_start = jnp.arange(n_blocks, dtype=jnp.int32) * EXPERT_BLOCK
    blk_e = jnp.minimum(jnp.searchsorted(pad_end, blk_start, side="right"),
                        N_EXPERTS - 1).astype(jnp.int32)
    n_used = (pad_end[-1:] // EXPERT_BLOCK).astype(jnp.int32)
    return dest, slot_tok, blk_e, n_used


def _mixers(x, wkv0, shift0, conv0, prm, w_in_bf16, chunk, nb, tc):
    batch, seq, d = x.shape
    shift_w = shift0.shape[1]
    glu_w = 2 * conv0.shape[2]
    x2d = x.reshape(batch * seq, d)
    p_rwkv, p_glu, p_gate = _in_proj(x2d, prm["norm1_g"], w_in_bf16, shift_w, glu_w)
    yg, wkv_new, shift_new = _rwkv(p_rwkv, shift0, wkv0, prm, batch, seq, chunk)
    h, hn, top_e, gates, conv_new = _mix(x2d, p_glu, p_gate, yg, conv0, prm, batch, seq, nb, tc)
    return h, hn, top_e, gates, wkv_new, shift_new, conv_new


def kernel(x_prompt, x_sample, state_wkv, state_shift, state_conv, norm1_g, w_in, b_glu, mu_shift, w0, w2, a0, a2, g2, k_k, k_a, r_k, lnx_g, lnx_b, w_out_a, w_dw, b_dw, ln_conv_g, ln_conv_b, w_pw2, b_pw2, w_o, norm2_g, w_router, b_router, w_up, b_up, w_down, b_down, norm_f_g):
    prm = dict(norm1_g=norm1_g, b_glu=b_glu, mu_shift=mu_shift, w0=w0, w2=w2, a0=a0, a2=a2,
               g2=g2, k_k=k_k, k_a=k_a, r_k=r_k, lnx_g=lnx_g, lnx_b=lnx_b, w_out_a=w_out_a,
               w_dw=w_dw, b_dw=b_dw, ln_conv_g=ln_conv_g, ln_conv_b=ln_conv_b, w_pw2=w_pw2,
               b_pw2=b_pw2, w_o=w_o, norm2_g=norm2_g, w_router=w_router, b_router=b_router)
    bp, lp, d = x_prompt.shape
    bs, ls, _ = x_sample.shape
    n_heads = state_wkv.shape[1]
    shift_w = state_shift.shape[1]
    ch = state_conv.shape[2]
    w_in_bf16 = w_in.astype(BF16)

    zeros_wkv = jnp.zeros((bp, n_heads, HEAD_DIM, HEAD_DIM), F32)
    zeros_shift = jnp.zeros((bp, shift_w), F32)
    zeros_conv = jnp.zeros((bp, CONV_WIDTH - 1, ch), F32)
    hp, hnp, tep, gp, wkv_p, shift_p, conv_p = _mixers(
        x_prompt, zeros_wkv, zeros_shift, zeros_conv, prm, w_in_bf16,
        chunk=min(64, lp), nb=1, tc=min(ROW_BLOCK, lp))
    hs, hns, tes, gs, wkv_s, shift_s, conv_s = _mixers(
        x_sample, state_wkv, state_shift, state_conv, prm, w_in_bf16,
        chunk=ls, nb=ROW_BLOCK // ls, tc=ls)

    h = jnp.concatenate([hp, hs], axis=0)
    hn = jnp.concatenate([hnp, hns], axis=0)
    top_e = jnp.concatenate([tep, tes], axis=0)[:, :TOP_K]
    gates = jnp.concatenate([gp, gs], axis=0)
    dest, slot_tok, blk_e, n_used = _routing_tables(top_e)
    ys = _experts(hn, slot_tok, blk_e, n_used, w_up, b_up, w_down, b_down)
    y = _combine(dest, gates, h, ys, norm_f_g)
    tp = bp * lp
    return (y[:tp].reshape(bp, lp, d), y[tp:].reshape(bs, ls, d),
            wkv_p, shift_p, conv_p, wkv_s, shift_s, conv_s)
```

```python
import functools

import jax
import jax.numpy as jnp
from jax import lax
from jax.experimental import pallas as pl
from jax.experimental.pallas import tpu as pltpu

F32 = jnp.float32
BF16 = jnp.bfloat16
HIGHEST = lax.Precision.HIGHEST

HEAD_DIM = 64
D_DECAY = 64
D_AAA = 64
D_GATE = 128
CONV_WIDTH = 31
N_EXPERTS = 32
TOP_K = 4
SWIGLU_LIMIT = 7.0
SWIGLU_ALPHA = 1.702
RMS_EPS = 1e-6
LN_EPS = 1e-5
GN_EPS = 64e-5

LANES = 128
SUBLANES = 8
LOG2_SUBLANES = 3
HIST_ROWS = 32
ROW_BLOCK = 256
EXPERT_BLOCK = 256
GROUP_HEADS = 4
GROUP_W = GROUP_HEADS * HEAD_DIM
LOG2_HEAD_DIM = HEAD_DIM.bit_length() - 1
VMEM_LIMIT = 56 * 1024 * 1024


def _dot_hi(a, b):
    return jnp.dot(a, b, precision=HIGHEST, preferred_element_type=F32)


def _dot_bf16(a, b):
    return jnp.dot(a.astype(BF16), b, preferred_element_type=F32)


def _sigmoid(x):
    return 1.0 / (1.0 + jnp.exp(-x))


def _rms_norm(x, g):
    return x * lax.rsqrt(jnp.mean(x * x, axis=-1, keepdims=True) + RMS_EPS) * g


def _inproj_kernel(x_ref, g_ref, w_ref, o_rwkv, o_glu, o_gate, *, shift_w, glu_w):
    xb = _rms_norm(x_ref[...], g_ref[...]).astype(BF16)
    o_rwkv[...] = jnp.dot(xb, w_ref[:, :shift_w], preferred_element_type=F32)
    o_glu[...] = jnp.dot(xb, w_ref[:, shift_w:shift_w + glu_w], preferred_element_type=F32)
    o_gate[...] = jnp.dot(xb, w_ref[:, shift_w + glu_w:], preferred_element_type=F32)


def _in_proj(x2d, norm_g, w_in_bf16, shift_w, glu_w):
    t, d = x2d.shape
    in_w = w_in_bf16.shape[1]
    gate_w = in_w - shift_w - glu_w
    tm = ROW_BLOCK
    return pl.pallas_call(
        functools.partial(_inproj_kernel, shift_w=shift_w, glu_w=glu_w),
        out_shape=(jax.ShapeDtypeStruct((t, shift_w), F32),
                   jax.ShapeDtypeStruct((t, glu_w), F32),
                   jax.ShapeDtypeStruct((t, gate_w), F32)),
        grid=(t // tm,),
        in_specs=[pl.BlockSpec((tm, d), lambda i: (i, 0)),
                  pl.BlockSpec((1, d), lambda i: (0, 0)),
                  pl.BlockSpec((d, in_w), lambda i: (0, 0), pipeline_mode=pl.Buffered(1))],
        out_specs=(pl.BlockSpec((tm, shift_w), lambda i: (i, 0)),
                   pl.BlockSpec((tm, glu_w), lambda i: (i, 0)),
                   pl.BlockSpec((tm, gate_w), lambda i: (i, 0))),
        compiler_params=pltpu.CompilerParams(dimension_semantics=("arbitrary",),
                                             vmem_limit_bytes=VMEM_LIMIT),
        name="in_proj",
    )(x2d, norm_g.reshape(1, d), w_in_bf16)


NN = (((1,), (0,)), ((), ()))
NT = (((1,), (1,)), ((), ()))
TN = (((0,), (0,)), ((), ()))


def _split2(x):
    hi = x.astype(BF16).astype(F32)
    return hi, x - hi


def _parts(x):
    hi, lo = _split2(x)
    return hi.astype(BF16), lo.astype(BF16)


def _mm(a, b_parts, dims=NN):
    m = a.shape[0]
    a_hi, a_lo = _split2(a)
    a_st = jnp.concatenate([a_hi, a_lo], axis=0).astype(BF16)
    b_hi, b_lo = b_parts
    r = lax.dot_general(a_st, b_hi, dims, preferred_element_type=F32)
    return r[:m] + r[m:] + lax.dot_general(a_st[:m], b_lo, dims, preferred_element_type=F32)


def _block_diag_parts(x, mask):
    hi, lo = _split2(x)
    tile = lambda q: jnp.where(mask, jnp.concatenate([q] * GROUP_HEADS, axis=0), 0.0).astype(BF16)
    return tile(hi), tile(lo)


def _rwkv_kernel(p_ref, shift0_ref, wkv0_ref, mu_ref, w0_ref, w2_ref, a0_ref, a2_ref, g2_ref,
                 kk_ref, ka_ref, rk_ref, lng_ref, lnb_ref,
                 yg_ref, wkv_ref, shift_ref,
                 pbuf, state, *, chunk, n_heads, n_chunks):
    c = pl.program_id(1)
    rw = n_heads * HEAD_DIM
    n_groups = rw // GROUP_W
    cw = GROUP_HEADS * chunk
    log2c = chunk.bit_length() - 1
    n_doublings = log2c - 1

    @pl.when(c == 0)
    def _():
        pbuf[7:8, :] = shift0_ref[0]
        state[...] = jnp.zeros(state.shape, F32)
        for hd in range(n_heads):
            g, j = divmod(hd, GROUP_HEADS)
            blk = slice(j * HEAD_DIM, (j + 1) * HEAD_DIM)
            state[g, blk, blk] = wkv0_ref[0, hd]

    p = p_ref[...]
    pbuf[8:8 + chunk, :] = p
    prev = pbuf[7:7 + chunk, :]
    last = p[chunk - 1:chunk, :]
    pbuf[7:8, :] = last
    shift_ref[0] = last

    h = p + (prev - p) * mu_ref[...]
    r = h[:, 0:rw]
    k = h[:, rw:2 * rw]
    v = h[:, 2 * rw:3 * rw]
    o = 3 * rw
    wd = h[:, o:o + D_DECAY]
    ad = h[:, o + D_DECAY:o + D_DECAY + D_AAA]
    gd = h[:, o + D_DECAY + D_AAA:o + D_DECAY + D_AAA + D_GATE]

    def iota(shape, dim):
        return lax.broadcasted_iota(jnp.int32, shape, dim)

    tri = (iota((chunk, chunk), 0) >= iota((chunk, chunk), 1)).astype(BF16)
    t4 = iota((chunk, cw), 0)
    s4 = jnp.bitwise_and(iota((chunk, cw), 1), chunk - 1)
    strict4 = t4 > s4
    incl4 = t4 >= s4
    eye4 = (t4 == s4).astype(F32)
    mask_v = (iota((cw, GROUP_W), 0) >> log2c) == (iota((cw, GROUP_W), 1) >> LOG2_HEAD_DIM)
    mask_p = (iota((cw, cw), 0) >> log2c) == (iota((cw, cw), 1) >> log2c)
    mask_s = ((iota((GROUP_W, GROUP_W), 0) >> LOG2_HEAD_DIM)
              == (iota((GROUP_W, GROUP_W), 1) >> LOG2_HEAD_DIM))
    ones_bd = mask_s.astype(BF16)

    def seg_sum(x):
        xs = jnp.concatenate([x[:, g * GROUP_W:(g + 1) * GROUP_W] for g in range(n_groups)], axis=0)
        hi, rest = _split2(xs)
        mid, lo = _split2(rest)
        st = jnp.concatenate([hi, mid, lo], axis=0).astype(BF16)
        rr = jnp.dot(st, ones_bd, preferred_element_type=F32)
        m = xs.shape[0]
        s = rr[:m] + rr[m:2 * m] + rr[2 * m:]
        return jnp.concatenate([s[g * chunk:(g + 1) * chunk] for g in range(n_groups)], axis=1)

    z = -(w0_ref[...] + _mm(jnp.tanh(wd), _parts(w2_ref[...])))
    softplus = jnp.maximum(z, 0.0) + jnp.log(1.0 + jnp.exp(-jnp.abs(z)))
    logw = -jnp.exp(-softplus - 0.5)
    l_hi, l_rest = _split2(logw)
    l_mid, l_lo = _split2(l_rest)
    cs = (jnp.dot(tri, l_hi.astype(BF16), preferred_element_type=F32)
          + jnp.dot(tri, l_mid.astype(BF16), preferred_element_type=F32)
          + jnp.dot(tri, l_lo.astype(BF16), preferred_element_type=F32))
    w_incl = jnp.exp(cs)
    w_excl = jnp.exp(cs - logw)
    w_inv = jnp.exp(-cs)
    w_last = w_incl[chunk - 1:chunk, :]

    a = _sigmoid(a0_ref[...] + _mm(ad, _parts(a2_ref[...])))
    g = _mm(_sigmoid(gd), _parts(g2_ref[...]))
    kk = k * kk_ref[...]
    kk = kk * lax.rsqrt(jnp.maximum(seg_sum(kk * kk), 1e-24))
    k2 = k * (1.0 + (a - 1.0) * ka_ref[...])
    at_f = kk * w_excl
    rt_f = r * w_incl
    kh_f = k2 * w_inv
    bh_f = kk * a * w_inv
    khw_f = kh_f * w_last
    bhw_f = bh_f * w_last

    groups = range(n_groups)
    gsl = [slice(gi * GROUP_W, (gi + 1) * GROUP_W) for gi in groups]
    v_g = [v[:, gs] for gs in gsl]
    ar = [jnp.concatenate([at_f[:, gs], rt_f[:, gs]], axis=0) for gs in gsl]
    gk = [_mm(ar[gi], _block_diag_parts(kh_f[:, gsl[gi]], mask_v), NT) for gi in groups]
    gb = [_mm(ar[gi], _block_diag_parts(bh_f[:, gsl[gi]], mask_v), NT) for gi in groups]
    s_old = [state[gi] for gi in groups]
    x0 = [_mm(ar[gi], _parts(s_old[gi]), NT) for gi in groups]
    a_ak = [jnp.where(strict4, gk[gi][:chunk], 0.0) for gi in groups]
    a_rk = [jnp.where(incl4, gk[gi][chunk:], 0.0) for gi in groups]
    a_ab = [jnp.where(strict4, gb[gi][:chunk], 0.0) for gi in groups]
    a_rb = [jnp.where(incl4, gb[gi][chunk:], 0.0) for gi in groups]
    av = [_mm(jnp.concatenate([a_ak[gi], a_rk[gi]], axis=0), _block_diag_parts(v_g[gi], mask_v))
          for gi in groups]
    rhs = [x0[gi][:chunk] + av[gi][:chunk] for gi in groups]
    nmat = [-a_ab[gi] for gi in groups]
    tinv = [eye4 + nmat[gi] for gi in groups]
    pw = [_mm(nmat[gi], _block_diag_parts(nmat[gi], mask_p)) for gi in groups]
    for i in range(n_doublings):
        pbd = [_block_diag_parts(pw[gi], mask_p) for gi in groups]
        if i == n_doublings - 1:
            tinv = [tinv[gi] + _mm(tinv[gi], pbd[gi]) for gi in groups]
        else:
            both = [_mm(jnp.concatenate([pw[gi], tinv[gi]], axis=0), pbd[gi]) for gi in groups]
            pw = [both[gi][:chunk] for gi in groups]
            tinv = [tinv[gi] + both[gi][chunk:] for gi in groups]
    u = [_mm(tinv[gi], _block_diag_parts(rhs[gi], mask_v)) for gi in groups]
    ys = [x0[gi][chunk:] + av[gi][chunk:] - _mm(a_rb[gi], _block_diag_parts(u[gi], mask_v))
          for gi in groups]
    for gi in groups:
        gs = gsl[gi]
        l_hi, l_lo = _split2(jnp.concatenate([v_g[gi], -u[gi]], axis=0))
        r_hi, r_lo = _split2(jnp.concatenate([khw_f[:, gs], bhw_f[:, gs]], axis=0))
        lhs = jnp.concatenate([l_hi, l_lo, l_hi], axis=0).astype(BF16)
        rhs3 = jnp.concatenate([r_hi, r_hi, r_lo], axis=0).astype(BF16)
        delta = lax.dot_general(lhs, rhs3, TN, preferred_element_type=F32)
        state[gi] = s_old[gi] * w_last[:, gs] + jnp.where(mask_s, delta, 0.0)

    y = jnp.concatenate(ys, axis=1)
    inv_n = 1.0 / HEAD_DIM
    yc = y - seg_sum(y) * inv_n
    var = seg_sum(yc * yc) * inv_n
    yn = yc * lax.rsqrt(var + GN_EPS) * lng_ref[...] + lnb_ref[...]
    bonus = seg_sum(r * k2 * rk_ref[...]) * v
    yg_ref[...] = ((yn + bonus) * g).astype(BF16)

    @pl.when(c == n_chunks - 1)
    def _():
        for hd in range(n_heads):
            gq, j = divmod(hd, GROUP_HEADS)
            blk = slice(j * HEAD_DIM, (j + 1) * HEAD_DIM)
            wkv_ref[0, hd] = state[gq, blk, blk]


def _rwkv(p_rwkv, shift0, wkv0, prm, batch, seq, chunk):
    t, shift_w = p_rwkv.shape
    n_heads = wkv0.shape[1]
    rw = n_heads * HEAD_DIM
    assert chunk >= 4 and chunk & (chunk - 1) == 0 and seq % chunk == 0 and rw % GROUP_W == 0
    nc = seq // chunk
    row = lambda a: a.reshape(1, -1)
    const = lambda shape: pl.BlockSpec(shape, lambda b, c: (0,) * len(shape))
    yg, wkv, shift = pl.pallas_call(
        functools.partial(_rwkv_kernel, chunk=chunk, n_heads=n_heads, n_chunks=nc),
        out_shape=(jax.ShapeDtypeStruct((t, rw), BF16),
                   jax.ShapeDtypeStruct(wkv0.shape, F32),
                   jax.ShapeDtypeStruct((batch, 1, shift_w), F32)),
        grid=(batch, nc),
        in_specs=[pl.BlockSpec((chunk, shift_w), lambda b, c: (b * nc + c, 0)),
                  pl.BlockSpec((1, 1, shift_w), lambda b, c: (b, 0, 0)),
                  pl.BlockSpec((1, n_heads, HEAD_DIM, HEAD_DIM), lambda b, c: (b, 0, 0, 0)),
                  const((1, shift_w)), const((1, rw)), const((D_DECAY, rw)), const((1, rw)),
                  const((D_AAA, rw)), const((D_GATE, rw)), const((1, rw)), const((1, rw)),
                  const((1, rw)), const((1, rw)), const((1, rw))],
        out_specs=(pl.BlockSpec((chunk, rw), lambda b, c: (b * nc + c, 0)),
                   pl.BlockSpec((1, n_heads, HEAD_DIM, HEAD_DIM), lambda b, c: (b, 0, 0, 0)),
                   pl.BlockSpec((1, 1, shift_w), lambda b, c: (b, 0, 0))),
        scratch_shapes=[pltpu.VMEM((8 + chunk, shift_w), F32),
                        pltpu.VMEM((rw // GROUP_W, GROUP_W, GROUP_W), F32)],
        compiler_params=pltpu.CompilerParams(dimension_semantics=("arbitrary", "arbitrary"),
                                             vmem_limit_bytes=VMEM_LIMIT),
        name="rwkv",
    )(p_rwkv, shift0.reshape(batch, 1, shift_w), wkv0,
      row(prm["mu_shift"]), row(prm["w0"]), prm["w2"], row(prm["a0"]), prm["a2"], prm["g2"],
      row(prm["k_k"]), row(prm["k_a"]), row(prm["r_k"]), row(prm["lnx_g"]), row(prm["lnx_b"]))
    return yg, wkv, shift.reshape(batch, shift_w)


def _mix_kernel(x_ref, pglu_ref, pg_ref, yg_ref, conv0_ref, bglu_ref, wdw_ref, bdw_ref,
                lncg_ref, lncb_ref, wpw2_ref, bpw2_ref, wouta_ref, wo_ref, n2g_ref, wr_ref, br_ref,
                cnt0_ref,
                h_ref, hn_ref, tope_ref, gate_ref, rank_ref, cnt_ref, convn_ref,
                full_ref, base_ref, *, nb, tc):
    c = pl.program_id(1)
    ch = x_ref.shape[1]
    hist = CONV_WIDTH - 1
    lo = HIST_ROWS - hist

    @pl.when(jnp.logical_and(pl.program_id(0) == 0, c == 0))
    def _():
        base_ref[...] = cnt0_ref[...]

    @pl.when(c == 0)
    def _():
        full_ref[:, lo:HIST_ROWS, :] = conv0_ref[...]

    @pl.when(c > 0)
    def _():
        full_ref[:, 0:HIST_ROWS, :] = full_ref[:, tc:tc + HIST_ROWS, :]

    z = pglu_ref[...] + bglu_ref[...]
    u = z[:, :ch] * _sigmoid(z[:, ch:])
    full_ref[:, HIST_ROWS:HIST_ROWS + tc, :] = u.reshape(nb, tc, ch)
    convn_ref[...] = full_ref[:, tc + lo:tc + HIST_ROWS, :]

    acc = jnp.zeros((nb, tc, ch), F32)
    for w in range(CONV_WIDTH):
        acc = acc + full_ref[:, lo + w:lo + w + tc, :] * wdw_ref[w:w + 1, :]
    cf = acc.reshape(nb * tc, ch) + bdw_ref[...]
    mu = jnp.mean(cf, axis=-1, keepdims=True)
    cc = cf - mu
    var = jnp.mean(cc * cc, axis=-1, keepdims=True)
    cn = cc * lax.rsqrt(var + LN_EPS) * lncg_ref[...] + lncb_ref[...]
    act = cn * _sigmoid(cn)
    y_b = _dot_bf16(act, wpw2_ref[...]) + bpw2_ref[...]
    y_a = jnp.dot(yg_ref[...], wouta_ref[...], preferred_element_type=F32)
    pg = pg_ref[...]
    merged = _sigmoid(pg[:, :ch]) * y_a + _sigmoid(pg[:, ch:]) * y_b
    h = x_ref[...] + _dot_bf16(merged, wo_ref[...])
    h_ref[...] = h
    hn = _rms_norm(h, n2g_ref[...])
    hn_ref[...] = hn

    logits = _dot_hi(hn, wr_ref[...]) + br_ref[...]
    lane = lax.broadcasted_iota(jnp.int32, logits.shape, 1)
    vals, idxs = [], []
    l = logits
    for _ in range(TOP_K):
        m = jnp.max(l, axis=-1, keepdims=True)
        idx = jnp.min(jnp.where(l == m, lane, LANES), axis=-1, keepdims=True)
        vals.append(m)
        idxs.append(idx)
        l = jnp.where(lane == idx, -jnp.inf, l)
    exps = [jnp.exp(vk - vals[0]) for vk in vals]
    denom = exps[0]
    for e in exps[1:]:
        denom = denom + e
    te = jnp.zeros(logits.shape, jnp.int32)
    tg = jnp.zeros(logits.shape, F32)
    for kx in range(TOP_K):
        te = jnp.where(lane == kx, idxs[kx], te)
        tg = jnp.where(lane == kx, exps[kx] / denom, tg)
    tope_ref[...] = te
    gate_ref[...] = tg

    rows = logits.shape[0]
    sel = jnp.zeros(logits.shape, F32)
    for kx in range(TOP_K):
        sel = sel + (lane == idxs[kx]).astype(F32)
    earlier = (lax.broadcasted_iota(jnp.int32, (rows, rows), 0)
               > lax.broadcasted_iota(jnp.int32, (rows, rows), 1)).astype(BF16)
    pos = jnp.dot(earlier, sel.astype(BF16), preferred_element_type=F32) + base_ref[...]
    rk = jnp.zeros(logits.shape, jnp.int32)
    for kx in range(TOP_K):
        r_k = jnp.sum(jnp.where(lane == idxs[kx], pos, 0.0), axis=-1, keepdims=True)
        rk = jnp.where(lane == kx, r_k.astype(jnp.int32), rk)
    rank_ref[...] = rk
    total = base_ref[...] + jnp.sum(sel, axis=0, keepdims=True)
    base_ref[...] = total
    cnt_ref[...] = total


def _mix(x2d, p_glu, p_gate, yg, conv0, cnt0, prm, batch, seq, nb, tc):
    t, d = x2d.shape
    ch = conv0.shape[2]
    hist = CONV_WIDTH - 1
    nc = seq // tc
    rows = nb * tc
    row = lambda a: a.reshape(1, -1)
    const = lambda shape: pl.BlockSpec(shape, lambda b, c: (0,) * len(shape))
    tok = lambda width: pl.BlockSpec((rows, width), lambda b, c: (b * nc + c, 0))
    wr = jnp.zeros((d, LANES), F32).at[:, :N_EXPERTS].set(prm["w_router"])
    br = jnp.full((1, LANES), -1e30, F32).at[0, :N_EXPERTS].set(prm["b_router"])
    return pl.pallas_call(
        functools.partial(_mix_kernel, nb=nb, tc=tc),
        out_shape=(jax.ShapeDtypeStruct((t, d), F32),
                   jax.ShapeDtypeStruct((t, d), F32),
                   jax.ShapeDtypeStruct((t, LANES), jnp.int32),
                   jax.ShapeDtypeStruct((t, LANES), F32),
                   jax.ShapeDtypeStruct((t, LANES), jnp.int32),
                   jax.ShapeDtypeStruct((1, LANES), F32),
                   jax.ShapeDtypeStruct((batch, hist, ch), F32)),
        grid=(batch // nb, nc),
        in_specs=[tok(d), tok(2 * ch), tok(2 * d), tok(d),
                  pl.BlockSpec((nb, hist, ch), lambda b, c: (b, 0, 0)),
                  const((1, 2 * ch)), const((CONV_WIDTH, ch)), const((1, ch)), const((1, ch)),
                  const((1, ch)), const((ch, d)), const((1, d)), const((d, d)), const((d, d)),
                  const((1, d)), const((d, LANES)), const((1, LANES)), const((1, LANES))],
        out_specs=(tok(d), tok(d), tok(LANES), tok(LANES), tok(LANES), const((1, LANES)),
                   pl.BlockSpec((nb, hist, ch), lambda b, c: (b, 0, 0))),
        scratch_shapes=[pltpu.VMEM((nb, HIST_ROWS + tc, ch), F32), pltpu.VMEM((1, LANES), F32)],
        compiler_params=pltpu.CompilerParams(dimension_semantics=("arbitrary", "arbitrary"),
                                             vmem_limit_bytes=VMEM_LIMIT),
        name="mix",
    )(x2d, p_glu, p_gate, yg, conv0,
      row(prm["b_glu"]), prm["w_dw"].reshape(CONV_WIDTH, ch), row(prm["b_dw"]),
      row(prm["ln_conv_g"]), row(prm["ln_conv_b"]), prm["w_pw2"].astype(BF16), row(prm["b_pw2"]),
      prm["w_out_a"].astype(BF16), prm["w_o"].astype(BF16), row(prm["norm2_g"]), wr, br, cnt0)


def _expert_kernel(blk_e_ref, nused_ref, tok_cur_ref, tok_nxt_ref, hn_hbm, wup_ref, bup_ref,
                   wdn_ref, bdn_ref, out_ref, xbuf0, xbuf1, wup_bf, wdn_bf, sem, *, n_blocks):
    i = pl.program_id(0)
    n_used = nused_ref[0]
    rows = xbuf0.shape[0]
    d_ff = wdn_ref.shape[1]

    def start_gather(tok_ref, buf, s):
        for r in range(rows):
            pltpu.make_async_copy(hn_hbm.at[pl.ds(tok_ref[0, 0, r], 1), :],
                                  buf.at[pl.ds(r, 1), :], sem.at[s]).start()

    def wait_gather(buf, s):
        pltpu.make_async_copy(hn_hbm.at[pl.ds(0, rows), :], buf, sem.at[s]).wait()

    @pl.when(i == 0)
    def _():
        start_gather(tok_cur_ref, xbuf0, 0)

    @pl.when(i >= n_used)
    def _():
        out_ref[...] = jnp.zeros(out_ref.shape, F32)

    def step(cur, s_cur, nxt, s_nxt):
        @pl.when(i <= n_used)
        def _():
            wait_gather(cur, s_cur)

        @pl.when(i < n_used)
        def _():
            prev_e = blk_e_ref[jnp.maximum(i - 1, 0)]
            @pl.when(jnp.logical_or(i == 0, blk_e_ref[i] != prev_e))
            def _():
                wup_bf[...] = wup_ref[0].astype(BF16)
                wdn_bf[...] = wdn_ref[0].astype(BF16)

            start_gather(tok_nxt_ref, nxt, s_nxt)
            hb = jnp.dot(cur[...].astype(BF16), wup_bf[...], preferred_element_type=F32) + bup_ref[0]
            glu = jnp.minimum(hb[:, :d_ff], SWIGLU_LIMIT)
            lin = jnp.clip(hb[:, d_ff:], -SWIGLU_LIMIT, SWIGLU_LIMIT)
            act = glu * _sigmoid(SWIGLU_ALPHA * glu) * (lin + 1.0)
            out_ref[...] = (jnp.dot(act.astype(BF16), wdn_bf[...], preferred_element_type=F32)
                            + bdn_ref[0])

            @pl.when(i == n_blocks - 1)
            def _():
                wait_gather(nxt, s_nxt)

    @pl.when(i % 2 == 0)
    def _():
        step(xbuf0, 0, xbuf1, 1)

    @pl.when(i % 2 == 1)
    def _():
        step(xbuf1, 1, xbuf0, 0)


def _experts(hn, slot_tok, blk_e, n_used, w_up, b_up, w_down, b_down):
    t, d = hn.shape
    n_blocks = blk_e.shape[0]
    rows = EXPERT_BLOCK
    n_exp, _, ff2 = w_up.shape
    d_ff = w_down.shape[1]
    grid_spec = pltpu.PrefetchScalarGridSpec(
        num_scalar_prefetch=2,
        grid=(n_blocks,),
        in_specs=[pl.BlockSpec((1, 1, rows), lambda i, be, nu: (i, 0, 0), memory_space=pltpu.SMEM),
                  pl.BlockSpec((1, 1, rows), lambda i, be, nu: (jnp.minimum(i + 1, n_blocks - 1), 0, 0),
                               memory_space=pltpu.SMEM),
                  pl.BlockSpec(memory_space=pl.ANY),
                  pl.BlockSpec((1, d, ff2), lambda i, be, nu: (be[i], 0, 0)),
                  pl.BlockSpec((1, 1, ff2), lambda i, be, nu: (be[i], 0, 0)),
                  pl.BlockSpec((1, d_ff, d), lambda i, be, nu: (be[i], 0, 0)),
                  pl.BlockSpec((1, 1, d), lambda i, be, nu: (be[i], 0, 0))],
        out_specs=pl.BlockSpec((rows, d), lambda i, be, nu: (i, 0)),
        scratch_shapes=[pltpu.VMEM((rows, d), F32),
                        pltpu.VMEM((rows, d), F32),
                        pltpu.VMEM((d, ff2), BF16),
                        pltpu.VMEM((d_ff, d), BF16),
                        pltpu.SemaphoreType.DMA((2,))],
    )
    slot_tok3 = slot_tok.reshape(n_blocks, 1, rows)
    return pl.pallas_call(
        functools.partial(_expert_kernel, n_blocks=n_blocks),
        out_shape=jax.ShapeDtypeStruct((n_blocks * rows, d), F32),
        grid_spec=grid_spec,
        compiler_params=pltpu.CompilerParams(dimension_semantics=("arbitrary",),
                                             vmem_limit_bytes=VMEM_LIMIT),
        name="experts",
    )(blk_e, n_used, slot_tok3, slot_tok3, hn,
      w_up, b_up.reshape(n_exp, 1, ff2), w_down, b_down.reshape(n_exp, 1, d))


def _combine_kernel(dest_cur_ref, dest_nxt_ref, gate_ref, h_ref, ys_hbm, g_ref, out_ref, buf, sem,
                    *, n):
    i = pl.program_id(0)
    rows = h_ref.shape[0]
    slot = i % 2

    groups = rows // SUBLANES

    def start_gather(dest_ref, s):
        def body(it, carry):
            for j in range(SUBLANES):
                for kx in range(TOP_K):
                    src_row = dest_ref[0, 0, (it * SUBLANES + j) * TOP_K + kx]
                    pltpu.make_async_copy(
                        ys_hbm.at[src_row >> LOG2_SUBLANES, pl.ds(src_row & (SUBLANES - 1), 1), :],
                        buf.at[(s * TOP_K + kx) * groups + it, pl.ds(j, 1), :], sem.at[s]).start()
            return carry

        lax.fori_loop(0, groups, body, 0)

    def choice_rows(s, kx):
        return buf.at[pl.ds((s * TOP_K + kx) * groups, groups)]

    def wait_gather(s):
        for kx in range(TOP_K):
            pltpu.make_async_copy(ys_hbm.at[pl.ds(0, groups)], choice_rows(s, kx), sem.at[s]).wait()

    @pl.when(i == 0)
    def _():
        start_gather(dest_cur_ref, 0)

    @pl.when(i + 1 < n)
    def _():
        start_gather(dest_nxt_ref, 1 - slot)

    wait_gather(slot)
    gates = gate_ref[...]
    y = None
    for kx in range(TOP_K):
        term = choice_rows(slot, kx)[...].reshape(rows, -1) * gates[:, kx:kx + 1]
        y = term if y is None else y + term
    out_ref[...] = _rms_norm(h_ref[...] + y, g_ref[...])


def _combine(dest, gates, h, ys, norm_g):
    t, d = h.shape
    rows = ROW_BLOCK
    nblk = t // rows
    dest3 = dest.reshape(nblk, 1, rows * TOP_K)
    return pl.pallas_call(
        functools.partial(_combine_kernel, n=nblk),
        out_shape=jax.ShapeDtypeStruct((t, d), F32),
        grid=(nblk,),
        in_specs=[pl.BlockSpec((1, 1, rows * TOP_K), lambda i: (i, 0, 0), memory_space=pltpu.SMEM),
                  pl.BlockSpec((1, 1, rows * TOP_K), lambda i: (jnp.minimum(i + 1, nblk - 1), 0, 0),
                               memory_space=pltpu.SMEM),
                  pl.BlockSpec((rows, LANES), lambda i: (i, 0)),
                  pl.BlockSpec((rows, d), lambda i: (i, 0)),
                  pl.BlockSpec(memory_space=pl.ANY),
                  pl.BlockSpec((1, d), lambda i: (0, 0))],
        out_specs=pl.BlockSpec((rows, d), lambda i: (i, 0)),
        scratch_shapes=[pltpu.VMEM((2 * TOP_K * rows // SUBLANES, SUBLANES, d), F32),
                        pltpu.SemaphoreType.DMA((2,))],
        compiler_params=pltpu.CompilerParams(dimension_semantics=("arbitrary",),
                                             vmem_limit_bytes=VMEM_LIMIT),
        name="combine",
    )(dest3, dest3, gates, h, ys.reshape(-1, SUBLANES, d), norm_g.reshape(1, d))


def _routing_tables(top_e, rank, counts):
    t = top_e.shape[0]
    n_assign = t * TOP_K
    padded = (counts + EXPERT_BLOCK - 1) // EXPERT_BLOCK * EXPERT_BLOCK
    pad_end = jnp.cumsum(padded)
    pad_start = pad_end - padded
    experts = jnp.arange(N_EXPERTS, dtype=jnp.int32)
    start_of = jnp.sum(jnp.where(top_e[..., None] == experts, pad_start, 0), axis=-1)
    dest = (start_of + rank).astype(jnp.int32)
    n_blocks = -(-n_assign // EXPERT_BLOCK) + N_EXPERTS
    flat_tok = jnp.arange(n_assign, dtype=jnp.int32) // TOP_K
    slot_tok = jnp.zeros((n_blocks * EXPERT_BLOCK,), jnp.int32).at[dest.reshape(-1)].set(flat_tok)
    blk_start = jnp.arange(n_blocks, dtype=jnp.int32) * EXPERT_BLOCK
    blk_e = jnp.minimum(jnp.sum((blk_start[:, None] >= pad_end[None, :]).astype(jnp.int32), axis=1),
                        N_EXPERTS - 1)
    n_used = (pad_end[-1:] // EXPERT_BLOCK).astype(jnp.int32)
    return dest, slot_tok, blk_e, n_used


def _mixers(x, wkv0, shift0, conv0, cnt0, prm, w_in_bf16, chunk, nb, tc):
    batch, seq, d = x.shape
    shift_w = shift0.shape[1]
    glu_w = 2 * conv0.shape[2]
    x2d = x.reshape(batch * seq, d)
    p_rwkv, p_glu, p_gate = _in_proj(x2d, prm["norm1_g"], w_in_bf16, shift_w, glu_w)
    yg, wkv_new, shift_new = _rwkv(p_rwkv, shift0, wkv0, prm, batch, seq, chunk)
    h, hn, top_e, gates, rank, cnt, conv_new = _mix(x2d, p_glu, p_gate, yg, conv0, cnt0, prm,
                                                    batch, seq, nb, tc)
    return h, hn, top_e, gates, rank, cnt, wkv_new, shift_new, conv_new


def kernel(x_prompt, x_sample, state_wkv, state_shift, state_conv, norm1_g, w_in, b_glu, mu_shift, w0, w2, a0, a2, g2, k_k, k_a, r_k, lnx_g, lnx_b, w_out_a, w_dw, b_dw, ln_conv_g, ln_conv_b, w_pw2, b_pw2, w_o, norm2_g, w_router, b_router, w_up, b_up, w_down, b_down, norm_f_g):
    prm = dict(norm1_g=norm1_g, b_glu=b_glu, mu_shift=mu_shift, w0=w0, w2=w2, a0=a0, a2=a2,
               g2=g2, k_k=k_k, k_a=k_a, r_k=r_k, lnx_g=lnx_g, lnx_b=lnx_b, w_out_a=w_out_a,
               w_dw=w_dw, b_dw=b_dw, ln_conv_g=ln_conv_g, ln_conv_b=ln_conv_b, w_pw2=w_pw2,
               b_pw2=b_pw2, w_o=w_o, norm2_g=norm2_g, w_router=w_router, b_router=b_router)
    bp, lp, d = x_prompt.shape
    bs, ls, _ = x_sample.shape
    n_heads = state_wkv.shape[1]
    shift_w = state_shift.shape[1]
    ch = state_conv.shape[2]
    w_in_bf16 = w_in.astype(BF16)

    zeros_wkv = jnp.zeros((bp, n_heads, HEAD_DIM, HEAD_DIM), F32)
    zeros_shift = jnp.zeros((bp, shift_w), F32)
    zeros_conv = jnp.zeros((bp, CONV_WIDTH - 1, ch), F32)
    zero_cnt = jnp.zeros((1, LANES), F32)
    hp, hnp, tep, gp, rkp, cnt_p, wkv_p, shift_p, conv_p = _mixers(
        x_prompt, zeros_wkv, zeros_shift, zeros_conv, zero_cnt, prm, w_in_bf16,
        chunk=min(64, lp), nb=1, tc=min(ROW_BLOCK, lp))
    hs, hns, tes, gs, rks, cnt, wkv_s, shift_s, conv_s = _mixers(
        x_sample, state_wkv, state_shift, state_conv, cnt_p, prm, w_in_bf16,
        chunk=ls, nb=ROW_BLOCK // ls, tc=ls)

    hn = jnp.concatenate([hnp, hns], axis=0)
    top_e = jnp.concatenate([tep[:, :TOP_K], tes[:, :TOP_K]], axis=0)
    rank = jnp.concatenate([rkp[:, :TOP_K], rks[:, :TOP_K]], axis=0)
    counts = cnt[0, :N_EXPERTS].astype(jnp.int32)
    dest, slot_tok, blk_e, n_used = _routing_tables(top_e, rank, counts)
    ys = _experts(hn, slot_tok, blk_e, n_used, w_up, b_up, w_down, b_down)
    tp = bp * lp
    y_p = _combine(dest[:tp], gp, hp, ys, norm_f_g)
    y_s = _combine(dest[tp:], gs, hs, ys, norm_f_g)
    return (y_p.reshape(bp, lp, d), y_s.reshape(bs, ls, d),
            wkv_p, shift_p, conv_p, wkv_s, shift_s, conv_s)
```

```python
import functools

import jax
import jax.numpy as jnp
from jax import lax
from jax.experimental import pallas as pl
from jax.experimental.pallas import tpu as pltpu

F32 = jnp.float32
BF16 = jnp.bfloat16
HIGHEST = lax.Precision.HIGHEST

HEAD_DIM = 64
D_DECAY = 64
D_AAA = 64
D_GATE = 128
CONV_WIDTH = 31
N_EXPERTS = 32
TOP_K = 4
SWIGLU_LIMIT = 7.0
SWIGLU_ALPHA = 1.702
RMS_EPS = 1e-6
LN_EPS = 1e-5
GN_EPS = 64e-5

LANES = 128
SUBLANES = 8
LOG2_SUBLANES = 3
HIST_ROWS = 32
ROW_BLOCK = 256
EXPERT_BLOCK = 256
GROUP_HEADS = 4
GROUP_W = GROUP_HEADS * HEAD_DIM
LOG2_HEAD_DIM = HEAD_DIM.bit_length() - 1
PROMPT_SEQS_PER_STEP = 2
SAMPLE_SEQS_PER_STEP = 4
CONV_ROW_TILE = 64
CONV_LANE_TILE = 256
CONV_WINDOW_ROWS = 128
VMEM_LIMIT = 56 * 1024 * 1024


def _dot_hi(a, b):
    return jnp.dot(a, b, precision=HIGHEST, preferred_element_type=F32)


def _dot_bf16(a, b):
    return jnp.dot(a.astype(BF16), b, preferred_element_type=F32)


def _sigmoid(x):
    return 1.0 / (1.0 + jnp.exp(-x))


def _rms_norm(x, g):
    return x * lax.rsqrt(jnp.mean(x * x, axis=-1, keepdims=True) + RMS_EPS) * g


def _inproj_kernel(x_ref, g_ref, w_ref, o_rwkv, o_glu, o_gate, *, shift_w, glu_w):
    xb = _rms_norm(x_ref[...], g_ref[...]).astype(BF16)
    o_rwkv[...] = jnp.dot(xb, w_ref[:, :shift_w], preferred_element_type=F32)
    o_glu[...] = jnp.dot(xb, w_ref[:, shift_w:shift_w + glu_w], preferred_element_type=F32)
    o_gate[...] = jnp.dot(xb, w_ref[:, shift_w + glu_w:], preferred_element_type=F32)


def _in_proj(x2d, norm_g, w_in_bf16, shift_w, glu_w):
    t, d = x2d.shape
    in_w = w_in_bf16.shape[1]
    gate_w = in_w - shift_w - glu_w
    tm = ROW_BLOCK
    return pl.pallas_call(
        functools.partial(_inproj_kernel, shift_w=shift_w, glu_w=glu_w),
        out_shape=(jax.ShapeDtypeStruct((t, shift_w), F32),
                   jax.ShapeDtypeStruct((t, glu_w), F32),
                   jax.ShapeDtypeStruct((t, gate_w), F32)),
        grid=(t // tm,),
        in_specs=[pl.BlockSpec((tm, d), lambda i: (i, 0)),
                  pl.BlockSpec((1, d), lambda i: (0, 0)),
                  pl.BlockSpec((d, in_w), lambda i: (0, 0), pipeline_mode=pl.Buffered(1))],
        out_specs=(pl.BlockSpec((tm, shift_w), lambda i: (i, 0)),
                   pl.BlockSpec((tm, glu_w), lambda i: (i, 0)),
                   pl.BlockSpec((tm, gate_w), lambda i: (i, 0))),
        compiler_params=pltpu.CompilerParams(dimension_semantics=("arbitrary",),
                                             vmem_limit_bytes=VMEM_LIMIT),
        name="in_proj",
    )(x2d, norm_g.reshape(1, d), w_in_bf16)


NN = (((1,), (0,)), ((), ()))
NT = (((1,), (1,)), ((), ()))
TN = (((0,), (0,)), ((), ()))


def _split2(x):
    hi = x.astype(BF16).astype(F32)
    return hi, x - hi


def _parts(x):
    hi, lo = _split2(x)
    return hi.astype(BF16), lo.astype(BF16)


def _mm(a, b_parts, dims=NN):
    m = a.shape[0]
    a_hi, a_lo = _split2(a)
    a_st = jnp.concatenate([a_hi, a_lo], axis=0).astype(BF16)
    b_hi, b_lo = b_parts
    r = lax.dot_general(a_st, b_hi, dims, preferred_element_type=F32)
    return r[:m] + r[m:] + lax.dot_general(a_st[:m], b_lo, dims, preferred_element_type=F32)


def _block_diag_parts(x, mask):
    hi, lo = _split2(x)
    tile = lambda q: jnp.where(mask, jnp.concatenate([q] * GROUP_HEADS, axis=0), 0.0).astype(BF16)
    return tile(hi), tile(lo)


def _rwkv_kernel(p_ref, shift0_ref, wkv0_ref, mu_ref, w0_ref, w2_ref, a0_ref, a2_ref, g2_ref,
                 kk_ref, ka_ref, rk_ref, lng_ref, lnb_ref,
                 yg_ref, wkv_ref, shift_ref,
                 pbuf, state, *, chunk, n_heads, n_chunks, nb):
    c = pl.program_id(1)
    rw = n_heads * HEAD_DIM
    n_groups = rw // GROUP_W
    cw = GROUP_HEADS * chunk
    log2c = chunk.bit_length() - 1
    n_doublings = log2c - 1

    rows = nb * chunk
    seqs = range(nb)

    @pl.when(c == 0)
    def _():
        pbuf[:, 7:8, :] = shift0_ref[...]
        state[...] = jnp.zeros(state.shape, F32)
        for bi in seqs:
            for hd in range(n_heads):
                g, j = divmod(hd, GROUP_HEADS)
                blk = slice(j * HEAD_DIM, (j + 1) * HEAD_DIM)
                state[bi * n_groups + g, blk, blk] = wkv0_ref[bi, hd]

    p3 = p_ref[...]
    pbuf[:, 8:8 + chunk, :] = p3
    prev = pbuf[:, 7:7 + chunk, :].reshape(rows, -1)
    last = p3[:, chunk - 1:chunk, :]
    pbuf[:, 7:8, :] = last
    shift_ref[...] = last
    p = p3.reshape(rows, -1)

    h = p + (prev - p) * mu_ref[...]
    r = h[:, 0:rw]
    k = h[:, rw:2 * rw]
    v = h[:, 2 * rw:3 * rw]
    o = 3 * rw
    wd = h[:, o:o + D_DECAY]
    ad = h[:, o + D_DECAY:o + D_DECAY + D_AAA]
    gd = h[:, o + D_DECAY + D_AAA:o + D_DECAY + D_AAA + D_GATE]

    def iota(shape, dim):
        return lax.broadcasted_iota(jnp.int32, shape, dim)

    tr, tc_ = iota((rows, rows), 0), iota((rows, rows), 1)
    tri = jnp.logical_and(tr >= tc_, (tr >> log2c) == (tc_ >> log2c)).astype(BF16)
    t4 = iota((chunk, cw), 0)
    s4 = jnp.bitwise_and(iota((chunk, cw), 1), chunk - 1)
    strict4 = t4 > s4
    incl4 = t4 >= s4
    eye4 = (t4 == s4).astype(F32)
    mask_v = (iota((cw, GROUP_W), 0) >> log2c) == (iota((cw, GROUP_W), 1) >> LOG2_HEAD_DIM)
    mask_p = (iota((cw, cw), 0) >> log2c) == (iota((cw, cw), 1) >> log2c)
    mask_s = ((iota((GROUP_W, GROUP_W), 0) >> LOG2_HEAD_DIM)
              == (iota((GROUP_W, GROUP_W), 1) >> LOG2_HEAD_DIM))
    ones_bd = mask_s.astype(BF16)

    def seg_sum(x):
        xs = jnp.concatenate([x[:, g * GROUP_W:(g + 1) * GROUP_W] for g in range(n_groups)], axis=0)
        hi, rest = _split2(xs)
        mid, lo = _split2(rest)
        st = jnp.concatenate([hi, mid, lo], axis=0).astype(BF16)
        rr = jnp.dot(st, ones_bd, preferred_element_type=F32)
        m = xs.shape[0]
        s = rr[:m] + rr[m:2 * m] + rr[2 * m:]
        return jnp.concatenate([s[g * rows:(g + 1) * rows] for g in range(n_groups)], axis=1)

    z = -(w0_ref[...] + _mm(jnp.tanh(wd), _parts(w2_ref[...])))
    softplus = jnp.maximum(z, 0.0) + jnp.log(1.0 + jnp.exp(-jnp.abs(z)))
    logw = -jnp.exp(-softplus - 0.5)
    l_hi, l_rest = _split2(logw)
    l_mid, l_lo = _split2(l_rest)
    cs = (jnp.dot(tri, l_hi.astype(BF16), preferred_element_type=F32)
          + jnp.dot(tri, l_mid.astype(BF16), preferred_element_type=F32)
          + jnp.dot(tri, l_lo.astype(BF16), preferred_element_type=F32))
    w_incl = jnp.exp(cs)
    w_excl = jnp.exp(cs - logw)
    w_inv = jnp.exp(-cs)
    w_last = [w_incl[(bi + 1) * chunk - 1:(bi + 1) * chunk, :] for bi in seqs]
    w_last_rows = jnp.concatenate([jnp.broadcast_to(w, (chunk, rw)) for w in w_last], axis=0)

    a = _sigmoid(a0_ref[...] + _mm(ad, _parts(a2_ref[...])))
    g = _mm(_sigmoid(gd), _parts(g2_ref[...]))
    kk = k * kk_ref[...]
    kk = kk * lax.rsqrt(jnp.maximum(seg_sum(kk * kk), 1e-24))
    k2 = k * (1.0 + (a - 1.0) * ka_ref[...])
    at_f = kk * w_excl
    rt_f = r * w_incl
    kh_f = k2 * w_inv
    bh_f = kk * a * w_inv
    khw_f = kh_f * w_last_rows
    bhw_f = bh_f * w_last_rows

    groups = range(nb * n_groups)
    rsl = [slice((gi // n_groups) * chunk, (gi // n_groups + 1) * chunk) for gi in groups]
    gsl = [slice((gi % n_groups) * GROUP_W, (gi % n_groups + 1) * GROUP_W) for gi in groups]
    v_g = [v[rsl[gi], gsl[gi]] for gi in groups]
    ar = [jnp.concatenate([at_f[rsl[gi], gsl[gi]], rt_f[rsl[gi], gsl[gi]]], axis=0)
          for gi in groups]
    gk = [_mm(ar[gi], _block_diag_parts(kh_f[rsl[gi], gsl[gi]], mask_v), NT) for gi in groups]
    gb = [_mm(ar[gi], _block_diag_parts(bh_f[rsl[gi], gsl[gi]], mask_v), NT) for gi in groups]
    s_old = [state[gi] for gi in groups]
    x0 = [_mm(ar[gi], _parts(s_old[gi]), NT) for gi in groups]
    a_ak = [jnp.where(strict4, gk[gi][:chunk], 0.0) for gi in groups]
    a_rk = [jnp.where(incl4, gk[gi][chunk:], 0.0) for gi in groups]
    a_ab = [jnp.where(strict4, gb[gi][:chunk], 0.0) for gi in groups]
    a_rb = [jnp.where(incl4, gb[gi][chunk:], 0.0) for gi in groups]
    av = [_mm(jnp.concatenate([a_ak[gi], a_rk[gi]], axis=0), _block_diag_parts(v_g[gi], mask_v))
          for gi in groups]
    rhs = [x0[gi][:chunk] + av[gi][:chunk] for gi in groups]
    nmat = [-a_ab[gi] for gi in groups]
    tinv = [eye4 + nmat[gi] for gi in groups]
    pw = [_mm(nmat[gi], _block_diag_parts(nmat[gi], mask_p)) for gi in groups]
    for i in range(n_doublings):
        pbd = [_block_diag_parts(pw[gi], mask_p) for gi in groups]
        if i == n_doublings - 1:
            tinv = [tinv[gi] + _mm(tinv[gi], pbd[gi]) for gi in groups]
        else:
            both = [_mm(jnp.concatenate([pw[gi], tinv[gi]], axis=0), pbd[gi]) for gi in groups]
            pw = [both[gi][:chunk] for gi in groups]
            tinv = [tinv[gi] + both[gi][chunk:] for gi in groups]
    u = [_mm(tinv[gi], _block_diag_parts(rhs[gi], mask_v)) for gi in groups]
    ys = [x0[gi][chunk:] + av[gi][chunk:] - _mm(a_rb[gi], _block_diag_parts(u[gi], mask_v))
          for gi in groups]
    for gi in groups:
        rs, gs = rsl[gi], gsl[gi]
        l_hi, l_lo = _split2(jnp.concatenate([v_g[gi], -u[gi]], axis=0))
        r_hi, r_lo = _split2(jnp.concatenate([khw_f[rs, gs], bhw_f[rs, gs]], axis=0))
        lhs = jnp.concatenate([l_hi, l_lo, l_hi], axis=0).astype(BF16)
        rhs3 = jnp.concatenate([r_hi, r_hi, r_lo], axis=0).astype(BF16)
        delta = lax.dot_general(lhs, rhs3, TN, preferred_element_type=F32)
        state[gi] = s_old[gi] * w_last[gi // n_groups][:, gs] + jnp.where(mask_s, delta, 0.0)

    y = jnp.concatenate([jnp.concatenate(ys[bi * n_groups:(bi + 1) * n_groups], axis=1)
                         for bi in seqs], axis=0)
    inv_n = 1.0 / HEAD_DIM
    yc = y - seg_sum(y) * inv_n
    var = seg_sum(yc * yc) * inv_n
    yn = yc * lax.rsqrt(var + GN_EPS) * lng_ref[...] + lnb_ref[...]
    bonus = seg_sum(r * k2 * rk_ref[...]) * v
    yg_ref[...] = ((yn + bonus) * g).astype(BF16).reshape(nb, chunk, rw)

    @pl.when(c == n_chunks - 1)
    def _():
        for bi in seqs:
            for hd in range(n_heads):
                gq, j = divmod(hd, GROUP_HEADS)
                blk = slice(j * HEAD_DIM, (j + 1) * HEAD_DIM)
                wkv_ref[bi, hd] = state[bi * n_groups + gq, blk, blk]


def _rwkv(p_rwkv, shift0, wkv0, prm, batch, seq, chunk, nb):
    t, shift_w = p_rwkv.shape
    n_heads = wkv0.shape[1]
    rw = n_heads * HEAD_DIM
    assert chunk >= 8 and chunk & (chunk - 1) == 0 and seq % chunk == 0 and rw % GROUP_W == 0
    assert batch % nb == 0
    nc = seq // chunk
    row = lambda a: a.reshape(1, -1)
    const = lambda shape: pl.BlockSpec(shape, lambda b, c: (0,) * len(shape))
    yg, wkv, shift = pl.pallas_call(
        functools.partial(_rwkv_kernel, chunk=chunk, n_heads=n_heads, n_chunks=nc, nb=nb),
        out_shape=(jax.ShapeDtypeStruct((batch, seq, rw), BF16),
                   jax.ShapeDtypeStruct(wkv0.shape, F32),
                   jax.ShapeDtypeStruct((batch, 1, shift_w), F32)),
        grid=(batch // nb, nc),
        in_specs=[pl.BlockSpec((nb, chunk, shift_w), lambda b, c: (b, c, 0)),
                  pl.BlockSpec((nb, 1, shift_w), lambda b, c: (b, 0, 0)),
                  pl.BlockSpec((nb, n_heads, HEAD_DIM, HEAD_DIM), lambda b, c: (b, 0, 0, 0)),
                  const((1, shift_w)), const((1, rw)), const((D_DECAY, rw)), const((1, rw)),
                  const((D_AAA, rw)), const((D_GATE, rw)), const((1, rw)), const((1, rw)),
                  const((1, rw)), const((1, rw)), const((1, rw))],
        out_specs=(pl.BlockSpec((nb, chunk, rw), lambda b, c: (b, c, 0)),
                   pl.BlockSpec((nb, n_heads, HEAD_DIM, HEAD_DIM), lambda b, c: (b, 0, 0, 0)),
                   pl.BlockSpec((nb, 1, shift_w), lambda b, c: (b, 0, 0))),
        scratch_shapes=[pltpu.VMEM((nb, 8 + chunk, shift_w), F32),
                        pltpu.VMEM((nb * rw // GROUP_W, GROUP_W, GROUP_W), F32)],
        compiler_params=pltpu.CompilerParams(dimension_semantics=("arbitrary", "arbitrary"),
                                             vmem_limit_bytes=VMEM_LIMIT),
        name="rwkv",
    )(p_rwkv.reshape(batch, seq, shift_w), shift0.reshape(batch, 1, shift_w), wkv0,
      row(prm["mu_shift"]), row(prm["w0"]), prm["w2"], row(prm["a0"]), prm["a2"], prm["g2"],
      row(prm["k_k"]), row(prm["k_a"]), row(prm["r_k"]), row(prm["lnx_g"]), row(prm["lnx_b"]))
    return yg.reshape(t, rw), wkv, shift.reshape(batch, shift_w)


def _mix_kernel(x_ref, pglu_ref, pg_ref, yg_ref, conv0_ref, bglu_ref, wdw_ref, bdw_ref,
                lncg_ref, lncb_ref, wpw2_ref, bpw2_ref, wouta_ref, wo_ref, n2g_ref, wr_ref, br_ref,
                cnt0_ref,
                h_ref, hn_ref, tope_ref, gate_ref, rank_ref, cnt_ref, convn_ref,
                full_ref, base_ref, cf_ref, win_ref, *, nb, tc):
    c = pl.program_id(1)
    ch = x_ref.shape[1]
    hist = CONV_WIDTH - 1
    lo = HIST_ROWS - hist

    @pl.when(jnp.logical_and(pl.program_id(0) == 0, c == 0))
    def _():
        base_ref[...] = cnt0_ref[...]

    @pl.when(c == 0)
    def _():
        full_ref[:, lo:HIST_ROWS, :] = conv0_ref[...]

    @pl.when(c > 0)
    def _():
        full_ref[:, 0:HIST_ROWS, :] = full_ref[:, tc:tc + HIST_ROWS, :]

    z = pglu_ref[...] + bglu_ref[...]
    u = z[:, :ch] * _sigmoid(z[:, ch:])
    full_ref[:, HIST_ROWS:HIST_ROWS + tc, :] = u.reshape(nb, tc, ch)
    convn_ref[...] = full_ref[:, tc + lo:tc + HIST_ROWS, :]

    b_tile, win_rows, _ = win_ref.shape
    r_tile = win_rows - (HIST_ROWS - SUBLANES)
    for b0 in range(0, nb, b_tile):
        for r0 in range(0, tc, r_tile):
            for l0 in range(0, ch, CONV_LANE_TILE):
                ls = slice(l0, l0 + CONV_LANE_TILE)
                acc = jnp.zeros((b_tile, r_tile, CONV_LANE_TILE), F32)
                for s in range(SUBLANES):
                    qs = [q for q in range(HIST_ROWS // SUBLANES + 1)
                          if lo <= SUBLANES * q + s <= lo + CONV_WIDTH - 1]
                    start = r0 + SUBLANES * qs[0] + s
                    n_win = r_tile + SUBLANES * (qs[-1] - qs[0])
                    win_ref[:, 0:n_win, :] = full_ref[b0:b0 + b_tile, start:start + n_win, ls]
                    for q in qs:
                        off = SUBLANES * (q - qs[0])
                        w = SUBLANES * q + s - lo
                        acc = acc + win_ref[:, off:off + r_tile, :] * wdw_ref[w:w + 1, ls]
                cf_ref[b0:b0 + b_tile, r0:r0 + r_tile, ls] = acc + bdw_ref[:, ls]
    cf = cf_ref[...].reshape(nb * tc, ch)
    mu = jnp.mean(cf, axis=-1, keepdims=True)
    cc = cf - mu
    var = jnp.mean(cc * cc, axis=-1, keepdims=True)
    cn = cc * lax.rsqrt(var + LN_EPS) * lncg_ref[...] + lncb_ref[...]
    act = cn * _sigmoid(cn)
    y_b = _dot_bf16(act, wpw2_ref[...]) + bpw2_ref[...]
    y_a = jnp.dot(yg_ref[...], wouta_ref[...], preferred_element_type=F32)
    pg = pg_ref[...]
    merged = _sigmoid(pg[:, :ch]) * y_a + _sigmoid(pg[:, ch:]) * y_b
    h = x_ref[...] + _dot_bf16(merged, wo_ref[...])
    h_ref[...] = h
    hn = _rms_norm(h, n2g_ref[...])
    hn_ref[...] = hn

    logits = _dot_hi(hn, wr_ref[...]) + br_ref[...]
    lane = lax.broadcasted_iota(jnp.int32, logits.shape, 1)
    vals, idxs = [], []
    l = logits
    for _ in range(TOP_K):
        m = jnp.max(l, axis=-1, keepdims=True)
        idx = jnp.min(jnp.where(l == m, lane, LANES), axis=-1, keepdims=True)
        vals.append(m)
        idxs.append(idx)
        l = jnp.where(lane == idx, -jnp.inf, l)
    exps = [jnp.exp(vk - vals[0]) for vk in vals]
    denom = exps[0]
    for e in exps[1:]:
        denom = denom + e
    te = jnp.zeros(logits.shape, jnp.int32)
    tg = jnp.zeros(logits.shape, F32)
    for kx in range(TOP_K):
        te = jnp.where(lane == kx, idxs[kx], te)
        tg = jnp.where(lane == kx, exps[kx] / denom, tg)
    tope_ref[...] = te
    gate_ref[...] = tg

    rows = logits.shape[0]
    sel = jnp.zeros(logits.shape, F32)
    for kx in range(TOP_K):
        sel = sel + (lane == idxs[kx]).astype(F32)
    earlier = (lax.broadcasted_iota(jnp.int32, (rows, rows), 0)
               > lax.broadcasted_iota(jnp.int32, (rows, rows), 1)).astype(BF16)
    pos = jnp.dot(earlier, sel.astype(BF16), preferred_element_type=F32) + base_ref[...]
    rk = jnp.zeros(logits.shape, jnp.int32)
    for kx in range(TOP_K):
        r_k = jnp.sum(jnp.where(lane == idxs[kx], pos, 0.0), axis=-1, keepdims=True)
        rk = jnp.where(lane == kx, r_k.astype(jnp.int32), rk)
    rank_ref[...] = rk
    total = base_ref[...] + jnp.sum(sel, axis=0, keepdims=True)
    base_ref[...] = total
    cnt_ref[...] = total


def _mix(x2d, p_glu, p_gate, yg, conv0, cnt0, prm, batch, seq, nb, tc):
    t, d = x2d.shape
    ch = conv0.shape[2]
    hist = CONV_WIDTH - 1
    nc = seq // tc
    rows = nb * tc
    r_tile = min(tc, CONV_ROW_TILE)
    win_rows = r_tile + HIST_ROWS - SUBLANES
    b_tile = min(nb, max(1, CONV_WINDOW_ROWS // win_rows))
    assert tc % r_tile == 0 and nb % b_tile == 0 and ch % CONV_LANE_TILE == 0
    row = lambda a: a.reshape(1, -1)
    const = lambda shape: pl.BlockSpec(shape, lambda b, c: (0,) * len(shape))
    tok = lambda width: pl.BlockSpec((rows, width), lambda b, c: (b * nc + c, 0))
    wr = jnp.zeros((d, LANES), F32).at[:, :N_EXPERTS].set(prm["w_router"])
    br = jnp.full((1, LANES), -1e30, F32).at[0, :N_EXPERTS].set(prm["b_router"])
    return pl.pallas_call(
        functools.partial(_mix_kernel, nb=nb, tc=tc),
        out_shape=(jax.ShapeDtypeStruct((t, d), F32),
                   jax.ShapeDtypeStruct((t, d), F32),
                   jax.ShapeDtypeStruct((t, LANES), jnp.int32),
                   jax.ShapeDtypeStruct((t, LANES), F32),
                   jax.ShapeDtypeStruct((t, LANES), jnp.int32),
                   jax.ShapeDtypeStruct((1, LANES), F32),
                   jax.ShapeDtypeStruct((batch, hist, ch), F32)),
        grid=(batch // nb, nc),
        in_specs=[tok(d), tok(2 * ch), tok(2 * d), tok(d),
                  pl.BlockSpec((nb, hist, ch), lambda b, c: (b, 0, 0)),
                  const((1, 2 * ch)), const((CONV_WIDTH, ch)), const((1, ch)), const((1, ch)),
                  const((1, ch)), const((ch, d)), const((1, d)), const((d, d)), const((d, d)),
                  const((1, d)), const((d, LANES)), const((1, LANES)), const((1, LANES))],
        out_specs=(tok(d), tok(d), tok(LANES), tok(LANES), tok(LANES), const((1, LANES)),
                   pl.BlockSpec((nb, hist, ch), lambda b, c: (b, 0, 0))),
        scratch_shapes=[pltpu.VMEM((nb, HIST_ROWS + tc, ch), F32), pltpu.VMEM((1, LANES), F32),
                        pltpu.VMEM((nb, tc, ch), F32),
                        pltpu.VMEM((b_tile, win_rows, CONV_LANE_TILE), F32)],
        compiler_params=pltpu.CompilerParams(dimension_semantics=("arbitrary", "arbitrary"),
                                             vmem_limit_bytes=VMEM_LIMIT),
        name="mix",
    )(x2d, p_glu, p_gate, yg, conv0,
      row(prm["b_glu"]), prm["w_dw"].reshape(CONV_WIDTH, ch), row(prm["b_dw"]),
      row(prm["ln_conv_g"]), row(prm["ln_conv_b"]), prm["w_pw2"].astype(BF16), row(prm["b_pw2"]),
      prm["w_out_a"].astype(BF16), prm["w_o"].astype(BF16), row(prm["norm2_g"]), wr, br, cnt0)


def _expert_kernel(blk_e_ref, nused_ref, tok_cur_ref, tok_nxt_ref, hn_hbm, wup_ref, bup_ref,
                   wdn_ref, bdn_ref, out_ref, xbuf0, xbuf1, wup_bf, wdn_bf, sem, *, n_blocks):
    i = pl.program_id(0)
    n_used = nused_ref[0]
    rows = xbuf0.shape[0]
    d_ff = wdn_ref.shape[1]
    assert wup_ref.shape[1] == SUBLANES * LANES

    def start_gather(tok_ref, buf, s):
        for r in range(rows):
            pltpu.make_async_copy(hn_hbm.at[tok_ref[0, 0, r]], buf.at[r], sem.at[s]).start()

    def wait_gather(buf, s):
        pltpu.make_async_copy(hn_hbm.at[pl.ds(0, rows)], buf, sem.at[s]).wait()

    def token_rows(buf):
        return jnp.concatenate([buf[:, j, :] for j in range(SUBLANES)], axis=1)

    @pl.when(i == 0)
    def _():
        start_gather(tok_cur_ref, xbuf0, 0)

    @pl.when(i >= n_used)
    def _():
        out_ref[...] = jnp.zeros(out_ref.shape, F32)

    def step(cur, s_cur, nxt, s_nxt):
        @pl.when(i <= n_used)
        def _():
            wait_gather(cur, s_cur)

        @pl.when(i < n_used)
        def _():
            prev_e = blk_e_ref[jnp.maximum(i - 1, 0)]
            @pl.when(jnp.logical_or(i == 0, blk_e_ref[i] != prev_e))
            def _():
                wup_bf[...] = wup_ref[0].astype(BF16)
                wdn_bf[...] = wdn_ref[0].astype(BF16)

            start_gather(tok_nxt_ref, nxt, s_nxt)
            hb = (jnp.dot(token_rows(cur).astype(BF16), wup_bf[...], preferred_element_type=F32)
                  + bup_ref[0])
            glu = jnp.minimum(hb[:, :d_ff], SWIGLU_LIMIT)
            lin = jnp.clip(hb[:, d_ff:], -SWIGLU_LIMIT, SWIGLU_LIMIT)
            act = glu * _sigmoid(SWIGLU_ALPHA * glu) * (lin + 1.0)
            out_ref[...] = (jnp.dot(act.astype(BF16), wdn_bf[...], preferred_element_type=F32)
                            + bdn_ref[0])

            @pl.when(i == n_blocks - 1)
            def _():
                wait_gather(nxt, s_nxt)

    @pl.when(i % 2 == 0)
    def _():
        step(xbuf0, 0, xbuf1, 1)

    @pl.when(i % 2 == 1)
    def _():
        step(xbuf1, 1, xbuf0, 0)


def _experts(hn, slot_tok, blk_e, n_used, w_up, b_up, w_down, b_down):
    t, d = hn.shape
    n_blocks = blk_e.shape[0]
    rows = EXPERT_BLOCK
    n_exp, _, ff2 = w_up.shape
    d_ff = w_down.shape[1]
    grid_spec = pltpu.PrefetchScalarGridSpec(
        num_scalar_prefetch=2,
        grid=(n_blocks,),
        in_specs=[pl.BlockSpec((1, 1, rows), lambda i, be, nu: (i, 0, 0), memory_space=pltpu.SMEM),
                  pl.BlockSpec((1, 1, rows), lambda i, be, nu: (jnp.minimum(i + 1, n_blocks - 1), 0, 0),
                               memory_space=pltpu.SMEM),
                  pl.BlockSpec(memory_space=pl.ANY),
                  pl.BlockSpec((1, d, ff2), lambda i, be, nu: (be[i], 0, 0)),
                  pl.BlockSpec((1, 1, ff2), lambda i, be, nu: (be[i], 0, 0)),
                  pl.BlockSpec((1, d_ff, d), lambda i, be, nu: (be[i], 0, 0)),
                  pl.BlockSpec((1, 1, d), lambda i, be, nu: (be[i], 0, 0))],
        out_specs=pl.BlockSpec((rows, d), lambda i, be, nu: (i, 0)),
        scratch_shapes=[pltpu.VMEM((rows, SUBLANES, LANES), F32),
                        pltpu.VMEM((rows, SUBLANES, LANES), F32),
                        pltpu.VMEM((d, ff2), BF16),
                        pltpu.VMEM((d_ff, d), BF16),
                        pltpu.SemaphoreType.DMA((2,))],
    )
    slot_tok3 = slot_tok.reshape(n_blocks, 1, rows)
    return pl.pallas_call(
        functools.partial(_expert_kernel, n_blocks=n_blocks),
        out_shape=jax.ShapeDtypeStruct((n_blocks * rows, d), F32),
        grid_spec=grid_spec,
        compiler_params=pltpu.CompilerParams(dimension_semantics=("arbitrary",),
                                             vmem_limit_bytes=VMEM_LIMIT),
        name="experts",
    )(blk_e, n_used, slot_tok3, slot_tok3, hn.reshape(t, SUBLANES, LANES),
      w_up, b_up.reshape(n_exp, 1, ff2), w_down, b_down.reshape(n_exp, 1, d))


def _combine_kernel(dest_cur_ref, dest_nxt_ref, gate_ref, h_ref, ys_hbm, g_ref, out_ref, buf, sem,
                    *, n):
    i = pl.program_id(0)
    rows = h_ref.shape[0]
    slot = i % 2

    groups = rows // SUBLANES

    def start_gather(dest_ref, s):
        def body(it, carry):
            for j in range(SUBLANES):
                for kx in range(TOP_K):
                    src_row = dest_ref[0, 0, (it * SUBLANES + j) * TOP_K + kx]
                    pltpu.make_async_copy(
                        ys_hbm.at[src_row >> LOG2_SUBLANES, pl.ds(src_row & (SUBLANES - 1), 1), :],
                        buf.at[(s * TOP_K + kx) * groups + it, pl.ds(j, 1), :], sem.at[s]).start()
            return carry

        lax.fori_loop(0, groups, body, 0)

    def choice_rows(s, kx):
        return buf.at[pl.ds((s * TOP_K + kx) * groups, groups)]

    def wait_gather(s):
        for kx in range(TOP_K):
            pltpu.make_async_copy(ys_hbm.at[pl.ds(0, groups)], choice_rows(s, kx), sem.at[s]).wait()

    @pl.when(i == 0)
    def _():
        start_gather(dest_cur_ref, 0)

    @pl.when(i + 1 < n)
    def _():
        start_gather(dest_nxt_ref, 1 - slot)

    wait_gather(slot)
    gates = gate_ref[...]
    y = None
    for kx in range(TOP_K):
        term = choice_rows(slot, kx)[...].reshape(rows, -1) * gates[:, kx:kx + 1]
        y = term if y is None else y + term
    out_ref[...] = _rms_norm(h_ref[...] + y, g_ref[...])


def _combine(dest, gates, h, ys, norm_g):
    t, d = h.shape
    rows = ROW_BLOCK
    nblk = t // rows
    dest3 = dest.reshape(nblk, 1, rows * TOP_K)
    return pl.pallas_call(
        functools.partial(_combine_kernel, n=nblk),
        out_shape=jax.ShapeDtypeStruct((t, d), F32),
        grid=(nblk,),
        in_specs=[pl.BlockSpec((1, 1, rows * TOP_K), lambda i: (i, 0, 0), memory_space=pltpu.SMEM),
                  pl.BlockSpec((1, 1, rows * TOP_K), lambda i: (jnp.minimum(i + 1, nblk - 1), 0, 0),
                               memory_space=pltpu.SMEM),
                  pl.BlockSpec((rows, LANES), lambda i: (i, 0)),
                  pl.BlockSpec((rows, d), lambda i: (i, 0)),
                  pl.BlockSpec(memory_space=pl.ANY),
                  pl.BlockSpec((1, d), lambda i: (0, 0))],
        out_specs=pl.BlockSpec((rows, d), lambda i: (i, 0)),
        scratch_shapes=[pltpu.VMEM((2 * TOP_K * rows // SUBLANES, SUBLANES, d), F32),
                        pltpu.SemaphoreType.DMA((2,))],
        compiler_params=pltpu.CompilerParams(dimension_semantics=("arbitrary",),
                                             vmem_limit_bytes=VMEM_LIMIT),
        name="combine",
    )(dest3, dest3, gates, h, ys.reshape(-1, SUBLANES, d), norm_g.reshape(1, d))


def _routing_tables(top_e, rank, counts):
    t = top_e.shape[0]
    n_assign = t * TOP_K
    padded = (counts + EXPERT_BLOCK - 1) // EXPERT_BLOCK * EXPERT_BLOCK
    pad_end = jnp.cumsum(padded)
    pad_start = pad_end - padded
    experts = jnp.arange(N_EXPERTS, dtype=jnp.int32)
    start_of = jnp.sum(jnp.where(top_e[..., None] == experts, pad_start, 0), axis=-1)
    dest = (start_of + rank).astype(jnp.int32)
    n_blocks = -(-n_assign // EXPERT_BLOCK) + N_EXPERTS
    flat_tok = jnp.arange(n_assign, dtype=jnp.int32) // TOP_K
    slot_tok = jnp.zeros((n_blocks * EXPERT_BLOCK,), jnp.int32).at[dest.reshape(-1)].set(flat_tok)
    blk_start = jnp.arange(n_blocks, dtype=jnp.int32) * EXPERT_BLOCK
    blk_e = jnp.minimum(jnp.sum((blk_start[:, None] >= pad_end[None, :]).astype(jnp.int32), axis=1),
                        N_EXPERTS - 1)
    n_used = (pad_end[-1:] // EXPERT_BLOCK).astype(jnp.int32)
    return dest, slot_tok, blk_e, n_used


def _mixers(x, wkv0, shift0, conv0, cnt0, prm, w_in_bf16, chunk, seqs_per_step, nb, tc):
    batch, seq, d = x.shape
    shift_w = shift0.shape[1]
    glu_w = 2 * conv0.shape[2]
    x2d = x.reshape(batch * seq, d)
    p_rwkv, p_glu, p_gate = _in_proj(x2d, prm["norm1_g"], w_in_bf16, shift_w, glu_w)
    yg, wkv_new, shift_new = _rwkv(p_rwkv, shift0, wkv0, prm, batch, seq, chunk, seqs_per_step)
    h, hn, top_e, gates, rank, cnt, conv_new = _mix(x2d, p_glu, p_gate, yg, conv0, cnt0, prm,
                                                    batch, seq, nb, tc)
    return h, hn, top_e, gates, rank, cnt, wkv_new, shift_new, conv_new


def kernel(x_prompt, x_sample, state_wkv, state_shift, state_conv, norm1_g, w_in, b_glu, mu_shift, w0, w2, a0, a2, g2, k_k, k_a, r_k, lnx_g, lnx_b, w_out_a, w_dw, b_dw, ln_conv_g, ln_conv_b, w_pw2, b_pw2, w_o, norm2_g, w_router, b_router, w_up, b_up, w_down, b_down, norm_f_g):
    prm = dict(norm1_g=norm1_g, b_glu=b_glu, mu_shift=mu_shift, w0=w0, w2=w2, a0=a0, a2=a2,
               g2=g2, k_k=k_k, k_a=k_a, r_k=r_k, lnx_g=lnx_g, lnx_b=lnx_b, w_out_a=w_out_a,
               w_dw=w_dw, b_dw=b_dw, ln_conv_g=ln_conv_g, ln_conv_b=ln_conv_b, w_pw2=w_pw2,
               b_pw2=b_pw2, w_o=w_o, norm2_g=norm2_g, w_router=w_router, b_router=b_router)
    bp, lp, d = x_prompt.shape
    bs, ls, _ = x_sample.shape
    n_heads = state_wkv.shape[1]
    shift_w = state_shift.shape[1]
    ch = state_conv.shape[2]
    w_in_bf16 = w_in.astype(BF16)

    zeros_wkv = jnp.zeros((bp, n_heads, HEAD_DIM, HEAD_DIM), F32)
    zeros_shift = jnp.zeros((bp, shift_w), F32)
    zeros_conv = jnp.zeros((bp, CONV_WIDTH - 1, ch), F32)
    zero_cnt = jnp.zeros((1, LANES), F32)
    hp, hnp, tep, gp, rkp, cnt_p, wkv_p, shift_p, conv_p = _mixers(
        x_prompt, zeros_wkv, zeros_shift, zeros_conv, zero_cnt, prm, w_in_bf16,
        chunk=min(64, lp), seqs_per_step=PROMPT_SEQS_PER_STEP, nb=1, tc=min(ROW_BLOCK, lp))
    hs, hns, tes, gs, rks, cnt, wkv_s, shift_s, conv_s = _mixers(
        x_sample, state_wkv, state_shift, state_conv, cnt_p, prm, w_in_bf16,
        chunk=ls, seqs_per_step=SAMPLE_SEQS_PER_STEP, nb=ROW_BLOCK // ls, tc=ls)

    hn = jnp.concatenate([hnp, hns], axis=0)
    top_e = jnp.concatenate([tep[:, :TOP_K], tes[:, :TOP_K]], axis=0)
    rank = jnp.concatenate([rkp[:, :TOP_K], rks[:, :TOP_K]], axis=0)
    counts = cnt[0, :N_EXPERTS].astype(jnp.int32)
    dest, slot_tok, blk_e, n_used = _routing_tables(top_e, rank, counts)
    ys = _experts(hn, slot_tok, blk_e, n_used, w_up, b_up, w_down, b_down)
    tp = bp * lp
    y_p = _combine(dest[:tp], gp, hp, ys, norm_f_g)
    y_s = _combine(dest[tp:], gs, hs, ys, norm_f_g)
    return (y_p.reshape(bp, lp, d), y_s.reshape(bs, ls, d),
            wkv_p, shift_p, conv_p, wkv_s, shift_s, conv_s)
```

```python
import functools

import jax
import jax.numpy as jnp
from jax import lax
from jax.experimental import pallas as pl
from jax.experimental.pallas import tpu as pltpu

F32 = jnp.float32
BF16 = jnp.bfloat16
HIGHEST = lax.Precision.HIGHEST

HEAD_DIM = 64
D_DECAY = 64
D_AAA = 64
D_GATE = 128
CONV_WIDTH = 31
N_EXPERTS = 32
TOP_K = 4
SWIGLU_LIMIT = 7.0
SWIGLU_ALPHA = 1.702
RMS_EPS = 1e-6
LN_EPS = 1e-5
GN_EPS = 64e-5

LANES = 128
SUBLANES = 8
LOG2_SUBLANES = 3
HIST_ROWS = 32
ROW_BLOCK = 256
EXPERT_BLOCK = 256
GROUP_HEADS = 4
GROUP_W = GROUP_HEADS * HEAD_DIM
LOG2_HEAD_DIM = HEAD_DIM.bit_length() - 1
PROMPT_SEQS_PER_STEP = 4
SAMPLE_SEQS_PER_STEP = 4
WEIGHT_PARTS = 4
CONV_ROW_TILE = 64
CONV_LANE_TILE = 256
CONV_WINDOW_ROWS = 128
VMEM_LIMIT = 56 * 1024 * 1024


def _dot_hi(a, b):
    return jnp.dot(a, b, precision=HIGHEST, preferred_element_type=F32)


def _dot_bf16(a, b):
    return jnp.dot(a.astype(BF16), b, preferred_element_type=F32)


def _sigmoid(x):
    return 1.0 / (1.0 + jnp.exp(-x))


def _rms_norm(x, g):
    return x * lax.rsqrt(jnp.mean(x * x, axis=-1, keepdims=True) + RMS_EPS) * g


def _inproj_kernel(x_ref, g_ref, w_ref, o_rwkv, o_glu, o_gate, *, shift_w, glu_w):
    xb = _rms_norm(x_ref[...], g_ref[...]).astype(BF16)
    o_rwkv[...] = jnp.dot(xb, w_ref[:, :shift_w], preferred_element_type=F32)
    o_glu[...] = jnp.dot(xb, w_ref[:, shift_w:shift_w + glu_w], preferred_element_type=F32)
    o_gate[...] = jnp.dot(xb, w_ref[:, shift_w + glu_w:], preferred_element_type=F32)


def _in_proj(x2d, norm_g, w_in_bf16, shift_w, glu_w):
    t, d = x2d.shape
    in_w = w_in_bf16.shape[1]
    gate_w = in_w - shift_w - glu_w
    tm = ROW_BLOCK
    return pl.pallas_call(
        functools.partial(_inproj_kernel, shift_w=shift_w, glu_w=glu_w),
        out_shape=(jax.ShapeDtypeStruct((t, shift_w), F32),
                   jax.ShapeDtypeStruct((t, glu_w), F32),
                   jax.ShapeDtypeStruct((t, gate_w), F32)),
        grid=(t // tm,),
        in_specs=[pl.BlockSpec((tm, d), lambda i: (i, 0)),
                  pl.BlockSpec((1, d), lambda i: (0, 0)),
                  pl.BlockSpec((d, in_w), lambda i: (0, 0), pipeline_mode=pl.Buffered(1))],
        out_specs=(pl.BlockSpec((tm, shift_w), lambda i: (i, 0)),
                   pl.BlockSpec((tm, glu_w), lambda i: (i, 0)),
                   pl.BlockSpec((tm, gate_w), lambda i: (i, 0))),
        compiler_params=pltpu.CompilerParams(dimension_semantics=("arbitrary",),
                                             vmem_limit_bytes=VMEM_LIMIT),
        name="in_proj",
    )(x2d, norm_g.reshape(1, d), w_in_bf16)


NN = (((1,), (0,)), ((), ()))
NT = (((1,), (1,)), ((), ()))
TN = (((0,), (0,)), ((), ()))


def _split2(x):
    hi = x.astype(BF16).astype(F32)
    return hi, x - hi


def _parts(x):
    hi, lo = _split2(x)
    return hi.astype(BF16), lo.astype(BF16)


def _mm(a, b_parts, dims=NN):
    m = a.shape[0]
    a_hi, a_lo = _split2(a)
    a_st = jnp.concatenate([a_hi, a_lo], axis=0).astype(BF16)
    b_hi, b_lo = b_parts
    r = lax.dot_general(a_st, b_hi, dims, preferred_element_type=F32)
    return r[:m] + r[m:] + lax.dot_general(a_st[:m], b_lo, dims, preferred_element_type=F32)


def _block_diag_parts(x, mask):
    hi, lo = _split2(x)
    tile = lambda q: jnp.where(mask, jnp.concatenate([q] * GROUP_HEADS, axis=0), 0.0).astype(BF16)
    return tile(hi), tile(lo)


def _rwkv_kernel(p_ref, shift0_ref, wkv0_ref, mu_ref, w0_ref, w2_ref, a0_ref, a2_ref, g2_ref,
                 kk_ref, ka_ref, rk_ref, lng_ref, lnb_ref,
                 yg_ref, wkv_ref, shift_ref,
                 pbuf, state, *, chunk, n_heads, n_chunks, nb):
    c = pl.program_id(1)
    rw = n_heads * HEAD_DIM
    n_groups = rw // GROUP_W
    cw = GROUP_HEADS * chunk
    log2c = chunk.bit_length() - 1
    n_doublings = log2c - 1

    rows = nb * chunk
    seqs = range(nb)

    @pl.when(c == 0)
    def _():
        pbuf[:, 7:8, :] = shift0_ref[...]
        state[...] = jnp.zeros(state.shape, F32)
        for bi in seqs:
            for hd in range(n_heads):
                g, j = divmod(hd, GROUP_HEADS)
                blk = slice(j * HEAD_DIM, (j + 1) * HEAD_DIM)
                state[bi * n_groups + g, blk, blk] = wkv0_ref[bi, hd]

    p3 = p_ref[...]
    pbuf[:, 8:8 + chunk, :] = p3
    prev = pbuf[:, 7:7 + chunk, :].reshape(rows, -1)
    last = p3[:, chunk - 1:chunk, :]
    pbuf[:, 7:8, :] = last
    shift_ref[...] = last
    p = p3.reshape(rows, -1)

    h = p + (prev - p) * mu_ref[...]
    r = h[:, 0:rw]
    k = h[:, rw:2 * rw]
    v = h[:, 2 * rw:3 * rw]
    o = 3 * rw
    wd = h[:, o:o + D_DECAY]
    ad = h[:, o + D_DECAY:o + D_DECAY + D_AAA]
    gd = h[:, o + D_DECAY + D_AAA:o + D_DECAY + D_AAA + D_GATE]

    def iota(shape, dim):
        return lax.broadcasted_iota(jnp.int32, shape, dim)

    tr, tc_ = iota((rows, rows), 0), iota((rows, rows), 1)
    tri = jnp.logical_and(tr >= tc_, (tr >> log2c) == (tc_ >> log2c)).astype(BF16)
    t4 = iota((chunk, cw), 0)
    s4 = jnp.bitwise_and(iota((chunk, cw), 1), chunk - 1)
    strict4 = t4 > s4
    incl4 = t4 >= s4
    eye4 = (t4 == s4).astype(F32)
    mask_v = (iota((cw, GROUP_W), 0) >> log2c) == (iota((cw, GROUP_W), 1) >> LOG2_HEAD_DIM)
    mask_p = (iota((cw, cw), 0) >> log2c) == (iota((cw, cw), 1) >> log2c)
    mask_s = ((iota((GROUP_W, GROUP_W), 0) >> LOG2_HEAD_DIM)
              == (iota((GROUP_W, GROUP_W), 1) >> LOG2_HEAD_DIM))
    ones_bd = mask_s.astype(BF16)

    def seg_sum(x):
        xs = jnp.concatenate([x[:, g * GROUP_W:(g + 1) * GROUP_W] for g in range(n_groups)], axis=0)
        hi, lo = _split2(xs)
        st = jnp.concatenate([hi, lo], axis=0).astype(BF16)
        rr = jnp.dot(st, ones_bd, preferred_element_type=F32)
        m = xs.shape[0]
        s = rr[:m] + rr[m:]
        return jnp.concatenate([s[g * rows:(g + 1) * rows] for g in range(n_groups)], axis=1)

    z = -(w0_ref[...] + _mm(jnp.tanh(wd), _parts(w2_ref[...])))
    softplus = jnp.maximum(z, 0.0) + jnp.log(1.0 + jnp.exp(-jnp.abs(z)))
    logw = -jnp.exp(-softplus - 0.5)
    l_hi, l_rest = _split2(logw)
    l_mid, l_lo = _split2(l_rest)
    cs = (jnp.dot(tri, l_hi.astype(BF16), preferred_element_type=F32)
          + jnp.dot(tri, l_mid.astype(BF16), preferred_element_type=F32)
          + jnp.dot(tri, l_lo.astype(BF16), preferred_element_type=F32))
    w_incl = jnp.exp(cs)
    w_excl = jnp.exp(cs - logw)
    w_inv = jnp.exp(-cs)
    w_last = [w_incl[(bi + 1) * chunk - 1:(bi + 1) * chunk, :] for bi in seqs]
    w_last_rows = jnp.concatenate([jnp.broadcast_to(w, (chunk, rw)) for w in w_last], axis=0)

    a = _sigmoid(a0_ref[...] + _mm(ad, _parts(a2_ref[...])))
    g = _mm(_sigmoid(gd), _parts(g2_ref[...]))
    kk = k * kk_ref[...]
    kk = kk * lax.rsqrt(jnp.maximum(seg_sum(kk * kk), 1e-24))
    k2 = k * (1.0 + (a - 1.0) * ka_ref[...])
    at_f = kk * w_excl
    rt_f = r * w_incl
    kh_f = k2 * w_inv
    bh_f = kk * a * w_inv
    khw_f = kh_f * w_last_rows
    bhw_f = bh_f * w_last_rows

    groups = range(nb * n_groups)
    rsl = [slice((gi // n_groups) * chunk, (gi // n_groups + 1) * chunk) for gi in groups]
    gsl = [slice((gi % n_groups) * GROUP_W, (gi % n_groups + 1) * GROUP_W) for gi in groups]
    v_g = [v[rsl[gi], gsl[gi]] for gi in groups]
    ar = [jnp.concatenate([at_f[rsl[gi], gsl[gi]], rt_f[rsl[gi], gsl[gi]]], axis=0)
          for gi in groups]
    gk = [_mm(ar[gi], _block_diag_parts(kh_f[rsl[gi], gsl[gi]], mask_v), NT) for gi in groups]
    gb = [_mm(ar[gi], _block_diag_parts(bh_f[rsl[gi], gsl[gi]], mask_v), NT) for gi in groups]
    s_old = [state[gi] for gi in groups]
    x0 = [_mm(ar[gi], _parts(s_old[gi]), NT) for gi in groups]
    a_ak = [jnp.where(strict4, gk[gi][:chunk], 0.0) for gi in groups]
    a_rk = [jnp.where(incl4, gk[gi][chunk:], 0.0) for gi in groups]
    a_ab = [jnp.where(strict4, gb[gi][:chunk], 0.0) for gi in groups]
    a_rb = [jnp.where(incl4, gb[gi][chunk:], 0.0) for gi in groups]
    av = [_mm(jnp.concatenate([a_ak[gi], a_rk[gi]], axis=0), _block_diag_parts(v_g[gi], mask_v))
          for gi in groups]
    rhs = [x0[gi][:chunk] + av[gi][:chunk] for gi in groups]
    nmat = [-a_ab[gi] for gi in groups]
    tinv = [eye4 + nmat[gi] for gi in groups]
    pw = [_mm(nmat[gi], _block_diag_parts(nmat[gi], mask_p)) for gi in groups]
    for i in range(n_doublings):
        pbd = [_block_diag_parts(pw[gi], mask_p) for gi in groups]
        if i == n_doublings - 1:
            tinv = [tinv[gi] + _mm(tinv[gi], pbd[gi]) for gi in groups]
        else:
            both = [_mm(jnp.concatenate([pw[gi], tinv[gi]], axis=0), pbd[gi]) for gi in groups]
            pw = [both[gi][:chunk] for gi in groups]
            tinv = [tinv[gi] + both[gi][chunk:] for gi in groups]
    u = [_mm(tinv[gi], _block_diag_parts(rhs[gi], mask_v)) for gi in groups]
    ys = [x0[gi][chunk:] + av[gi][chunk:] - _mm(a_rb[gi], _block_diag_parts(u[gi], mask_v))
          for gi in groups]
    for gi in groups:
        rs, gs = rsl[gi], gsl[gi]
        l_hi, l_lo = _split2(jnp.concatenate([v_g[gi], -u[gi]], axis=0))
        r_hi, r_lo = _split2(jnp.concatenate([khw_f[rs, gs], bhw_f[rs, gs]], axis=0))
        lhs = jnp.concatenate([l_hi, l_lo, l_hi], axis=0).astype(BF16)
        rhs3 = jnp.concatenate([r_hi, r_hi, r_lo], axis=0).astype(BF16)
        delta = lax.dot_general(lhs, rhs3, TN, preferred_element_type=F32)
        state[gi] = s_old[gi] * w_last[gi // n_groups][:, gs] + jnp.where(mask_s, delta, 0.0)

    y = jnp.concatenate([jnp.concatenate(ys[bi * n_groups:(bi + 1) * n_groups], axis=1)
                         for bi in seqs], axis=0)
    inv_n = 1.0 / HEAD_DIM
    yc = y - seg_sum(y) * inv_n
    var = seg_sum(yc * yc) * inv_n
    yn = yc * lax.rsqrt(var + GN_EPS) * lng_ref[...] + lnb_ref[...]
    bonus = seg_sum(r * k2 * rk_ref[...]) * v
    yg_ref[...] = ((yn + bonus) * g).astype(BF16).reshape(nb, chunk, rw)

    @pl.when(c == n_chunks - 1)
    def _():
        for bi in seqs:
            for hd in range(n_heads):
                gq, j = divmod(hd, GROUP_HEADS)
                blk = slice(j * HEAD_DIM, (j + 1) * HEAD_DIM)
                wkv_ref[bi, hd] = state[bi * n_groups + gq, blk, blk]


def _rwkv(p_rwkv, shift0, wkv0, prm, batch, seq, chunk, nb):
    t, shift_w = p_rwkv.shape
    n_heads = wkv0.shape[1]
    rw = n_heads * HEAD_DIM
    assert chunk >= 8 and chunk & (chunk - 1) == 0 and seq % chunk == 0 and rw % GROUP_W == 0
    assert batch % nb == 0
    nc = seq // chunk
    row = lambda a: a.reshape(1, -1)
    const = lambda shape: pl.BlockSpec(shape, lambda b, c: (0,) * len(shape))
    yg, wkv, shift = pl.pallas_call(
        functools.partial(_rwkv_kernel, chunk=chunk, n_heads=n_heads, n_chunks=nc, nb=nb),
        out_shape=(jax.ShapeDtypeStruct((batch, seq, rw), BF16),
                   jax.ShapeDtypeStruct(wkv0.shape, F32),
                   jax.ShapeDtypeStruct((batch, 1, shift_w), F32)),
        grid=(batch // nb, nc),
        in_specs=[pl.BlockSpec((nb, chunk, shift_w), lambda b, c: (b, c, 0)),
                  pl.BlockSpec((nb, 1, shift_w), lambda b, c: (b, 0, 0)),
                  pl.BlockSpec((nb, n_heads, HEAD_DIM, HEAD_DIM), lambda b, c: (b, 0, 0, 0)),
                  const((1, shift_w)), const((1, rw)), const((D_DECAY, rw)), const((1, rw)),
                  const((D_AAA, rw)), const((D_GATE, rw)), const((1, rw)), const((1, rw)),
                  const((1, rw)), const((1, rw)), const((1, rw))],
        out_specs=(pl.BlockSpec((nb, chunk, rw), lambda b, c: (b, c, 0)),
                   pl.BlockSpec((nb, n_heads, HEAD_DIM, HEAD_DIM), lambda b, c: (b, 0, 0, 0)),
                   pl.BlockSpec((nb, 1, shift_w), lambda b, c: (b, 0, 0))),
        scratch_shapes=[pltpu.VMEM((nb, 8 + chunk, shift_w), F32),
                        pltpu.VMEM((nb * rw // GROUP_W, GROUP_W, GROUP_W), F32)],
        compiler_params=pltpu.CompilerParams(dimension_semantics=("arbitrary", "arbitrary"),
                                             vmem_limit_bytes=VMEM_LIMIT),
        name="rwkv",
    )(p_rwkv.reshape(batch, seq, shift_w), shift0.reshape(batch, 1, shift_w), wkv0,
      row(prm["mu_shift"]), row(prm["w0"]), prm["w2"], row(prm["a0"]), prm["a2"], prm["g2"],
      row(prm["k_k"]), row(prm["k_a"]), row(prm["r_k"]), row(prm["lnx_g"]), row(prm["lnx_b"]))
    return yg.reshape(t, rw), wkv, shift.reshape(batch, shift_w)


def _mix_kernel(x_ref, pglu_ref, pg_ref, yg_ref, conv0_ref, bglu_ref, wdw_ref, bdw_ref,
                lncg_ref, lncb_ref, wpw2_ref, bpw2_ref, wouta_ref, wo_ref, n2g_ref, wr_ref, br_ref,
                cnt0_ref,
                h_ref, hn_ref, tope_ref, gate_ref, rank_ref, cnt_ref, convn_ref,
                full_ref, base_ref, cf_ref, win_ref, *, nb, tc):
    c = pl.program_id(1)
    ch = x_ref.shape[1]
    hist = CONV_WIDTH - 1
    lo = HIST_ROWS - hist

    @pl.when(jnp.logical_and(pl.program_id(0) == 0, c == 0))
    def _():
        base_ref[...] = cnt0_ref[...]

    @pl.when(c == 0)
    def _():
        full_ref[:, lo:HIST_ROWS, :] = conv0_ref[...]

    @pl.when(c > 0)
    def _():
        full_ref[:, 0:HIST_ROWS, :] = full_ref[:, tc:tc + HIST_ROWS, :]

    z = pglu_ref[...] + bglu_ref[...]
    u = z[:, :ch] * _sigmoid(z[:, ch:])
    full_ref[:, HIST_ROWS:HIST_ROWS + tc, :] = u.reshape(nb, tc, ch)
    convn_ref[...] = full_ref[:, tc + lo:tc + HIST_ROWS, :]

    b_tile, win_rows, _ = win_ref.shape
    r_tile = win_rows - (HIST_ROWS - SUBLANES)
    for b0 in range(0, nb, b_tile):
        for r0 in range(0, tc, r_tile):
            for l0 in range(0, ch, CONV_LANE_TILE):
                ls = slice(l0, l0 + CONV_LANE_TILE)
                acc = jnp.zeros((b_tile, r_tile, CONV_LANE_TILE), F32)
                for s in range(SUBLANES):
                    qs = [q for q in range(HIST_ROWS // SUBLANES + 1)
                          if lo <= SUBLANES * q + s <= lo + CONV_WIDTH - 1]
                    start = r0 + SUBLANES * qs[0] + s
                    n_win = r_tile + SUBLANES * (qs[-1] - qs[0])
                    win_ref[:, 0:n_win, :] = full_ref[b0:b0 + b_tile, start:start + n_win, ls]
                    for q in qs:
                        off = SUBLANES * (q - qs[0])
                        w = SUBLANES * q + s - lo
                        acc = acc + win_ref[:, off:off + r_tile, :] * wdw_ref[w:w + 1, ls]
                cf_ref[b0:b0 + b_tile, r0:r0 + r_tile, ls] = acc + bdw_ref[:, ls]
    cf = cf_ref[...].reshape(nb * tc, ch)
    mu = jnp.mean(cf, axis=-1, keepdims=True)
    cc = cf - mu
    var = jnp.mean(cc * cc, axis=-1, keepdims=True)
    cn = cc * lax.rsqrt(var + LN_EPS) * lncg_ref[...] + lncb_ref[...]
    act = cn * _sigmoid(cn)
    y_b = _dot_bf16(act, wpw2_ref[...]) + bpw2_ref[...]
    y_a = jnp.dot(yg_ref[...], wouta_ref[...], preferred_element_type=F32)
    pg = pg_ref[...]
    merged = _sigmoid(pg[:, :ch]) * y_a + _sigmoid(pg[:, ch:]) * y_b
    h = x_ref[...] + _dot_bf16(merged, wo_ref[...])
    h_ref[...] = h
    hn = _rms_norm(h, n2g_ref[...])
    hn_ref[...] = hn

    logits = _dot_hi(hn, wr_ref[...]) + br_ref[...]
    lane = lax.broadcasted_iota(jnp.int32, logits.shape, 1)
    vals, idxs = [], []
    l = logits
    for _ in range(TOP_K):
        m = jnp.max(l, axis=-1, keepdims=True)
        idx = jnp.min(jnp.where(l == m, lane, LANES), axis=-1, keepdims=True)
        vals.append(m)
        idxs.append(idx)
        l = jnp.where(lane == idx, -jnp.inf, l)
    exps = [jnp.exp(vk - vals[0]) for vk in vals]
    denom = exps[0]
    for e in exps[1:]:
        denom = denom + e
    te = jnp.zeros(logits.shape, jnp.int32)
    tg = jnp.zeros(logits.shape, F32)
    for kx in range(TOP_K):
        te = jnp.where(lane == kx, idxs[kx], te)
        tg = jnp.where(lane == kx, exps[kx] / denom, tg)
    tope_ref[...] = te
    gate_ref[...] = tg

    rows = logits.shape[0]
    sel = jnp.zeros(logits.shape, F32)
    for kx in range(TOP_K):
        sel = sel + (lane == idxs[kx]).astype(F32)
    earlier = (lax.broadcasted_iota(jnp.int32, (rows, rows), 0)
               > lax.broadcasted_iota(jnp.int32, (rows, rows), 1)).astype(BF16)
    pos = jnp.dot(earlier, sel.astype(BF16), preferred_element_type=F32) + base_ref[...]
    rk = jnp.zeros(logits.shape, jnp.int32)
    for kx in range(TOP_K):
        r_k = jnp.sum(jnp.where(lane == idxs[kx], pos, 0.0), axis=-1, keepdims=True)
        rk = jnp.where(lane == kx, r_k.astype(jnp.int32), rk)
    rank_ref[...] = rk
    total = base_ref[...] + jnp.sum(sel, axis=0, keepdims=True)
    base_ref[...] = total
    cnt_ref[...] = total


def _mix(x2d, p_glu, p_gate, yg, conv0, cnt0, prm, batch, seq, nb, tc):
    t, d = x2d.shape
    ch = conv0.shape[2]
    hist = CONV_WIDTH - 1
    nc = seq // tc
    rows = nb * tc
    r_tile = min(tc, CONV_ROW_TILE)
    win_rows = r_tile + HIST_ROWS - SUBLANES
    b_tile = min(nb, max(1, CONV_WINDOW_ROWS // win_rows))
    assert tc % r_tile == 0 and nb % b_tile == 0 and ch % CONV_LANE_TILE == 0
    row = lambda a: a.reshape(1, -1)
    const = lambda shape: pl.BlockSpec(shape, lambda b, c: (0,) * len(shape))
    tok = lambda width: pl.BlockSpec((rows, width), lambda b, c: (b * nc + c, 0))
    wr = jnp.zeros((d, LANES), F32).at[:, :N_EXPERTS].set(prm["w_router"])
    br = jnp.full((1, LANES), -1e30, F32).at[0, :N_EXPERTS].set(prm["b_router"])
    return pl.pallas_call(
        functools.partial(_mix_kernel, nb=nb, tc=tc),
        out_shape=(jax.ShapeDtypeStruct((t, d), F32),
                   jax.ShapeDtypeStruct((t, d), F32),
                   jax.ShapeDtypeStruct((t, LANES), jnp.int32),
                   jax.ShapeDtypeStruct((t, LANES), F32),
                   jax.ShapeDtypeStruct((t, LANES), jnp.int32),
                   jax.ShapeDtypeStruct((1, LANES), F32),
                   jax.ShapeDtypeStruct((batch, hist, ch), F32)),
        grid=(batch // nb, nc),
        in_specs=[tok(d), tok(2 * ch), tok(2 * d), tok(d),
                  pl.BlockSpec((nb, hist, ch), lambda b, c: (b, 0, 0)),
                  const((1, 2 * ch)), const((CONV_WIDTH, ch)), const((1, ch)), const((1, ch)),
                  const((1, ch)), const((ch, d)), const((1, d)), const((d, d)), const((d, d)),
                  const((1, d)), const((d, LANES)), const((1, LANES)), const((1, LANES))],
        out_specs=(tok(d), tok(d), tok(LANES), tok(LANES), tok(LANES), const((1, LANES)),
                   pl.BlockSpec((nb, hist, ch), lambda b, c: (b, 0, 0))),
        scratch_shapes=[pltpu.VMEM((nb, HIST_ROWS + tc, ch), F32), pltpu.VMEM((1, LANES), F32),
                        pltpu.VMEM((nb, tc, ch), F32),
                        pltpu.VMEM((b_tile, win_rows, CONV_LANE_TILE), F32)],
        compiler_params=pltpu.CompilerParams(dimension_semantics=("arbitrary", "arbitrary"),
                                             vmem_limit_bytes=VMEM_LIMIT),
        name="mix",
    )(x2d, p_glu, p_gate, yg, conv0,
      row(prm["b_glu"]), prm["w_dw"].reshape(CONV_WIDTH, ch), row(prm["b_dw"]),
      row(prm["ln_conv_g"]), row(prm["ln_conv_b"]), prm["w_pw2"].astype(BF16), row(prm["b_pw2"]),
      prm["w_out_a"].astype(BF16), prm["w_o"].astype(BF16), row(prm["norm2_g"]), wr, br, cnt0)


def _expert_kernel(blk_e_ref, nused_ref, tok_cur_ref, tok_nxt_ref, hn_hbm, *refs, n_blocks):
    wup_parts = refs[:WEIGHT_PARTS]
    bup_ref = refs[WEIGHT_PARTS]
    wdn_parts = refs[WEIGHT_PARTS + 1:2 * WEIGHT_PARTS + 1]
    bdn_ref = refs[2 * WEIGHT_PARTS + 1]
    out_ref, xbuf0, xbuf1, wup_bf, wdn_bf, sem = refs[2 * WEIGHT_PARTS + 2:]
    i = pl.program_id(0)
    n_used = nused_ref[0]
    rows = xbuf0.shape[0]
    d_ff = wdn_bf.shape[0]
    assert wup_bf.shape[0] == SUBLANES * LANES

    def start_gather(tok_ref, buf, s):
        for r in range(rows):
            pltpu.make_async_copy(hn_hbm.at[tok_ref[0, 0, r]], buf.at[r], sem.at[s]).start()

    def wait_gather(buf, s):
        pltpu.make_async_copy(hn_hbm.at[pl.ds(0, rows)], buf, sem.at[s]).wait()

    def token_rows(buf):
        return jnp.concatenate([buf[:, j, :] for j in range(SUBLANES)], axis=1)

    @pl.when(i == 0)
    def _():
        start_gather(tok_cur_ref, xbuf0, 0)

    @pl.when(i >= n_used)
    def _():
        out_ref[...] = jnp.zeros(out_ref.shape, F32)

    def step(cur, s_cur, nxt, s_nxt):
        @pl.when(i <= n_used)
        def _():
            wait_gather(cur, s_cur)

        @pl.when(i < n_used)
        def _():
            prev_e = blk_e_ref[jnp.maximum(i - 1, 0)]
            @pl.when(jnp.logical_or(i == 0, blk_e_ref[i] != prev_e))
            def _():
                for parts, dst in ((wup_parts, wup_bf), (wdn_parts, wdn_bf)):
                    width = dst.shape[1] // WEIGHT_PARTS
                    for j, part in enumerate(parts):
                        dst[:, j * width:(j + 1) * width] = part[0].astype(BF16)

            start_gather(tok_nxt_ref, nxt, s_nxt)
            hb = (jnp.dot(token_rows(cur).astype(BF16), wup_bf[...], preferred_element_type=F32)
                  + bup_ref[0])
            glu = jnp.minimum(hb[:, :d_ff], SWIGLU_LIMIT)
            lin = jnp.clip(hb[:, d_ff:], -SWIGLU_LIMIT, SWIGLU_LIMIT)
            act = glu * _sigmoid(SWIGLU_ALPHA * glu) * (lin + 1.0)
            out_ref[...] = (jnp.dot(act.astype(BF16), wdn_bf[...], preferred_element_type=F32)
                            + bdn_ref[0])

            @pl.when(i == n_blocks - 1)
            def _():
                wait_gather(nxt, s_nxt)

    @pl.when(i % 2 == 0)
    def _():
        step(xbuf0, 0, xbuf1, 1)

    @pl.when(i % 2 == 1)
    def _():
        step(xbuf1, 1, xbuf0, 0)


def _experts(hn, slot_tok, blk_e, n_used, w_up, b_up, w_down, b_down):
    t, d = hn.shape
    n_blocks = blk_e.shape[0]
    rows = EXPERT_BLOCK
    n_exp, _, ff2 = w_up.shape
    d_ff = w_down.shape[1]
    grid_spec = pltpu.PrefetchScalarGridSpec(
        num_scalar_prefetch=2,
        grid=(n_blocks,),
        in_specs=[pl.BlockSpec((1, 1, rows), lambda i, be, nu: (i, 0, 0), memory_space=pltpu.SMEM),
                  pl.BlockSpec((1, 1, rows), lambda i, be, nu: (jnp.minimum(i + 1, n_blocks - 1), 0, 0),
                               memory_space=pltpu.SMEM),
                  pl.BlockSpec(memory_space=pl.ANY),
                  *[pl.BlockSpec((1, d, ff2 // WEIGHT_PARTS), lambda i, be, nu, j=j: (be[i], 0, j))
                    for j in range(WEIGHT_PARTS)],
                  pl.BlockSpec((1, 1, ff2), lambda i, be, nu: (be[i], 0, 0)),
                  *[pl.BlockSpec((1, d_ff, d // WEIGHT_PARTS), lambda i, be, nu, j=j: (be[i], 0, j))
                    for j in range(WEIGHT_PARTS)],
                  pl.BlockSpec((1, 1, d), lambda i, be, nu: (be[i], 0, 0))],
        out_specs=pl.BlockSpec((rows, d), lambda i, be, nu: (i, 0)),
        scratch_shapes=[pltpu.VMEM((rows, SUBLANES, LANES), F32),
                        pltpu.VMEM((rows, SUBLANES, LANES), F32),
                        pltpu.VMEM((d, ff2), BF16),
                        pltpu.VMEM((d_ff, d), BF16),
                        pltpu.SemaphoreType.DMA((2,))],
    )
    slot_tok3 = slot_tok.reshape(n_blocks, 1, rows)
    return pl.pallas_call(
        functools.partial(_expert_kernel, n_blocks=n_blocks),
        out_shape=jax.ShapeDtypeStruct((n_blocks * rows, d), F32),
        grid_spec=grid_spec,
        compiler_params=pltpu.CompilerParams(dimension_semantics=("arbitrary",),
                                             vmem_limit_bytes=VMEM_LIMIT),
        name="experts",
    )(blk_e, n_used, slot_tok3, slot_tok3, hn.reshape(t, SUBLANES, LANES),
      *[w_up] * WEIGHT_PARTS, b_up.reshape(n_exp, 1, ff2),
      *[w_down] * WEIGHT_PARTS, b_down.reshape(n_exp, 1, d))


def _combine_kernel(dest_cur_ref, dest_nxt_ref, gate_ref, h_ref, ys_hbm, g_ref, out_ref, buf, sem,
                    *, n):
    i = pl.program_id(0)
    rows = h_ref.shape[0]
    slot = i % 2

    groups = rows // SUBLANES

    def start_gather(dest_ref, s):
        def body(it, carry):
            for j in range(SUBLANES):
                for kx in range(TOP_K):
                    src_row = dest_ref[0, 0, (it * SUBLANES + j) * TOP_K + kx]
                    pltpu.make_async_copy(
                        ys_hbm.at[src_row >> LOG2_SUBLANES, pl.ds(src_row & (SUBLANES - 1), 1), :],
                        buf.at[(s * TOP_K + kx) * groups + it, pl.ds(j, 1), :], sem.at[s]).start()
            return carry

        lax.fori_loop(0, groups, body, 0)

    def choice_rows(s, kx):
        return buf.at[pl.ds((s * TOP_K + kx) * groups, groups)]

    def wait_gather(s):
        for kx in range(TOP_K):
            pltpu.make_async_copy(ys_hbm.at[pl.ds(0, groups)], choice_rows(s, kx), sem.at[s]).wait()

    @pl.when(i == 0)
    def _():
        start_gather(dest_cur_ref, 0)

    @pl.when(i + 1 < n)
    def _():
        start_gather(dest_nxt_ref, 1 - slot)

    wait_gather(slot)
    gates = gate_ref[...]
    y = None
    for kx in range(TOP_K):
        term = choice_rows(slot, kx)[...].reshape(rows, -1) * gates[:, kx:kx + 1]
        y = term if y is None else y + term
    out_ref[...] = _rms_norm(h_ref[...] + y, g_ref[...])


def _combine(dest, gates, h, ys, norm_g):
    t, d = h.shape
    rows = ROW_BLOCK
    nblk = t // rows
    dest3 = dest.reshape(nblk, 1, rows * TOP_K)
    return pl.pallas_call(
        functools.partial(_combine_kernel, n=nblk),
        out_shape=jax.ShapeDtypeStruct((t, d), F32),
        grid=(nblk,),
        in_specs=[pl.BlockSpec((1, 1, rows * TOP_K), lambda i: (i, 0, 0), memory_space=pltpu.SMEM),
                  pl.BlockSpec((1, 1, rows * TOP_K), lambda i: (jnp.minimum(i + 1, nblk - 1), 0, 0),
                               memory_space=pltpu.SMEM),
                  pl.BlockSpec((rows, LANES), lambda i: (i, 0)),
                  pl.BlockSpec((rows, d), lambda i: (i, 0)),
                  pl.BlockSpec(memory_space=pl.ANY),
                  pl.BlockSpec((1, d), lambda i: (0, 0))],
        out_specs=pl.BlockSpec((rows, d), lambda i: (i, 0)),
        scratch_shapes=[pltpu.VMEM((2 * TOP_K * rows // SUBLANES, SUBLANES, d), F32),
                        pltpu.SemaphoreType.DMA((2,))],
        compiler_params=pltpu.CompilerParams(dimension_semantics=("arbitrary",),
                                             vmem_limit_bytes=VMEM_LIMIT),
        name="combine",
    )(dest3, dest3, gates, h, ys.reshape(-1, SUBLANES, d), norm_g.reshape(1, d))


def _routing_tables(top_e, rank, counts):
    t = top_e.shape[0]
    n_assign = t * TOP_K
    padded = (counts + EXPERT_BLOCK - 1) // EXPERT_BLOCK * EXPERT_BLOCK
    pad_end = jnp.cumsum(padded)
    pad_start = pad_end - padded
    experts = jnp.arange(N_EXPERTS, dtype=jnp.int32)
    start_of = jnp.sum(jnp.where(top_e[..., None] == experts, pad_start, 0), axis=-1)
    dest = (start_of + rank).astype(jnp.int32)
    n_blocks = -(-n_assign // EXPERT_BLOCK) + N_EXPERTS
    flat_tok = jnp.arange(n_assign, dtype=jnp.int32) // TOP_K
    slot_tok = jnp.zeros((n_blocks * EXPERT_BLOCK,), jnp.int32).at[dest.reshape(-1)].set(flat_tok)
    blk_start = jnp.arange(n_blocks, dtype=jnp.int32) * EXPERT_BLOCK
    blk_e = jnp.minimum(jnp.sum((blk_start[:, None] >= pad_end[None, :]).astype(jnp.int32), axis=1),
                        N_EXPERTS - 1)
    n_used = (pad_end[-1:] // EXPERT_BLOCK).astype(jnp.int32)
    return dest, slot_tok, blk_e, n_used


def _mixers(x, wkv0, shift0, conv0, cnt0, prm, w_in_bf16, chunk, seqs_per_step, nb, tc):
    batch, seq, d = x.shape
    shift_w = shift0.shape[1]
    glu_w = 2 * conv0.shape[2]
    x2d = x.reshape(batch * seq, d)
    p_rwkv, p_glu, p_gate = _in_proj(x2d, prm["norm1_g"], w_in_bf16, shift_w, glu_w)
    yg, wkv_new, shift_new = _rwkv(p_rwkv, shift0, wkv0, prm, batch, seq, chunk, seqs_per_step)
    h, hn, top_e, gates, rank, cnt, conv_new = _mix(x2d, p_glu, p_gate, yg, conv0, cnt0, prm,
                                                    batch, seq, nb, tc)
    return h, hn, top_e, gates, rank, cnt, wkv_new, shift_new, conv_new


def kernel(x_prompt, x_sample, state_wkv, state_shift, state_conv, norm1_g, w_in, b_glu, mu_shift, w0, w2, a0, a2, g2, k_k, k_a, r_k, lnx_g, lnx_b, w_out_a, w_dw, b_dw, ln_conv_g, ln_conv_b, w_pw2, b_pw2, w_o, norm2_g, w_router, b_router, w_up, b_up, w_down, b_down, norm_f_g):
    prm = dict(norm1_g=norm1_g, b_glu=b_glu, mu_shift=mu_shift, w0=w0, w2=w2, a0=a0, a2=a2,
               g2=g2, k_k=k_k, k_a=k_a, r_k=r_k, lnx_g=lnx_g, lnx_b=lnx_b, w_out_a=w_out_a,
               w_dw=w_dw, b_dw=b_dw, ln_conv_g=ln_conv_g, ln_conv_b=ln_conv_b, w_pw2=w_pw2,
               b_pw2=b_pw2, w_o=w_o, norm2_g=norm2_g, w_router=w_router, b_router=b_router)
    bp, lp, d = x_prompt.shape
    bs, ls, _ = x_sample.shape
    n_heads = state_wkv.shape[1]
    shift_w = state_shift.shape[1]
    ch = state_conv.shape[2]
    w_in_bf16 = w_in.astype(BF16)

    zeros_wkv = jnp.zeros((bp, n_heads, HEAD_DIM, HEAD_DIM), F32)
    zeros_shift = jnp.zeros((bp, shift_w), F32)
    zeros_conv = jnp.zeros((bp, CONV_WIDTH - 1, ch), F32)
    zero_cnt = jnp.zeros((1, LANES), F32)
    hp, hnp, tep, gp, rkp, cnt_p, wkv_p, shift_p, conv_p = _mixers(
        x_prompt, zeros_wkv, zeros_shift, zeros_conv, zero_cnt, prm, w_in_bf16,
        chunk=min(64, lp), seqs_per_step=PROMPT_SEQS_PER_STEP, nb=1, tc=min(ROW_BLOCK, lp))
    hs, hns, tes, gs, rks, cnt, wkv_s, shift_s, conv_s = _mixers(
        x_sample, state_wkv, state_shift, state_conv, cnt_p, prm, w_in_bf16,
        chunk=ls, seqs_per_step=SAMPLE_SEQS_PER_STEP, nb=ROW_BLOCK // ls, tc=ls)

    hn = jnp.concatenate([hnp, hns], axis=0)
    top_e = jnp.concatenate([tep[:, :TOP_K], tes[:, :TOP_K]], axis=0)
    rank = jnp.concatenate([rkp[:, :TOP_K], rks[:, :TOP_K]], axis=0)
    counts = cnt[0, :N_EXPERTS].astype(jnp.int32)
    dest, slot_tok, blk_e, n_used = _routing_tables(top_e, rank, counts)
    ys = _experts(hn, slot_tok, blk_e, n_used, w_up, b_up, w_down, b_down)
    tp = bp * lp
    y_p = _combine(dest[:tp], gp, hp, ys, norm_f_g)
    y_s = _combine(dest[tp:], gs, hs, ys, norm_f_g)
    return (y_p.reshape(bp, lp, d), y_s.reshape(bs, ls, d),
            wkv_p, shift_p, conv_p, wkv_s, shift_s, conv_s)
```

```python
import functools

import jax
import jax.numpy as jnp
from jax import lax
from jax.experimental import pallas as pl
from jax.experimental.pallas import tpu as pltpu

F32 = jnp.float32
BF16 = jnp.bfloat16

HEAD_DIM = 64
D_DECAY = 64
D_AAA = 64
D_GATE = 128
CONV_WIDTH = 31
N_EXPERTS = 32
TOP_K = 4
SWIGLU_LIMIT = 7.0
SWIGLU_ALPHA = 1.702
RMS_EPS = 1e-6
LN_EPS = 1e-5
GN_EPS = 64e-5

LANES = 128
SUBLANES = 8
LOG2_SUBLANES = 3
HIST_ROWS = 32
ROW_BLOCK = 256
EXPERT_BLOCK = 256
GROUP_HEADS = 4
GROUP_W = GROUP_HEADS * HEAD_DIM
LOG2_HEAD_DIM = HEAD_DIM.bit_length() - 1
PROMPT_SEQS_PER_STEP = 4
SAMPLE_SEQS_PER_STEP = 4
CONV_ROW_TILE = 64
CONV_LANE_TILE = 256
CONV_WINDOW_ROWS = 128
VMEM_LIMIT = 56 * 1024 * 1024


def _dot_bf16(a, b):
    return jnp.dot(a.astype(BF16), b, preferred_element_type=F32)


def _sigmoid(x):
    return 1.0 / (1.0 + jnp.exp(-x))


def _rms_norm(x, g):
    return x * lax.rsqrt(jnp.mean(x * x, axis=-1, keepdims=True) + RMS_EPS) * g


def _inproj_kernel(x_ref, g_ref, w_ref, o_rwkv, o_glu, o_gate, *, shift_w, glu_w):
    xb = _rms_norm(x_ref[...], g_ref[...]).astype(BF16)
    o_rwkv[...] = jnp.dot(xb, w_ref[:, :shift_w], preferred_element_type=F32)
    o_glu[...] = jnp.dot(xb, w_ref[:, shift_w:shift_w + glu_w], preferred_element_type=F32)
    o_gate[...] = jnp.dot(xb, w_ref[:, shift_w + glu_w:], preferred_element_type=F32)


def _in_proj(x2d, norm_g, w_in_bf16, shift_w, glu_w):
    t, d = x2d.shape
    in_w = w_in_bf16.shape[1]
    gate_w = in_w - shift_w - glu_w
    tm = ROW_BLOCK
    return pl.pallas_call(
        functools.partial(_inproj_kernel, shift_w=shift_w, glu_w=glu_w),
        out_shape=(jax.ShapeDtypeStruct((t, shift_w), F32),
                   jax.ShapeDtypeStruct((t, glu_w), F32),
                   jax.ShapeDtypeStruct((t, gate_w), F32)),
        grid=(t // tm,),
        in_specs=[pl.BlockSpec((tm, d), lambda i: (i, 0)),
                  pl.BlockSpec((1, d), lambda i: (0, 0)),
                  pl.BlockSpec((d, in_w), lambda i: (0, 0), pipeline_mode=pl.Buffered(1))],
        out_specs=(pl.BlockSpec((tm, shift_w), lambda i: (i, 0)),
                   pl.BlockSpec((tm, glu_w), lambda i: (i, 0)),
                   pl.BlockSpec((tm, gate_w), lambda i: (i, 0))),
        compiler_params=pltpu.CompilerParams(dimension_semantics=("arbitrary",),
                                             vmem_limit_bytes=VMEM_LIMIT),
        name="in_proj",
    )(x2d, norm_g.reshape(1, d), w_in_bf16)


NN = (((1,), (0,)), ((), ()))
NT = (((1,), (1,)), ((), ()))
TN = (((0,), (0,)), ((), ()))


def _split2(x):
    hi = x.astype(BF16).astype(F32)
    return hi, x - hi


def _parts(x):
    hi, lo = _split2(x)
    return hi.astype(BF16), lo.astype(BF16)


def _mm(a, b_parts, dims=NN):
    m = a.shape[0]
    a_hi, a_lo = _split2(a)
    a_st = jnp.concatenate([a_hi, a_lo], axis=0).astype(BF16)
    b_hi, b_lo = b_parts
    r = lax.dot_general(a_st, b_hi, dims, preferred_element_type=F32)
    return r[:m] + r[m:] + lax.dot_general(a_st[:m], b_lo, dims, preferred_element_type=F32)


def _block_diag_parts(x, mask):
    hi, lo = _split2(x)
    tile = lambda q: jnp.where(mask, jnp.concatenate([q] * GROUP_HEADS, axis=0), 0.0).astype(BF16)
    return tile(hi), tile(lo)


def _rwkv_kernel(p_ref, shift0_ref, wkv0_ref, mu_ref, w0_ref, w2_ref, a0_ref, a2_ref, g2_ref,
                 kk_ref, ka_ref, rk_ref, lng_ref, lnb_ref,
                 yg_ref, wkv_ref, shift_ref,
                 pbuf, state, *, chunk, n_heads, n_chunks, nb):
    c = pl.program_id(1)
    rw = n_heads * HEAD_DIM
    n_groups = rw // GROUP_W
    cw = GROUP_HEADS * chunk
    log2c = chunk.bit_length() - 1
    n_doublings = log2c - 1

    rows = nb * chunk
    seqs = range(nb)

    @pl.when(c == 0)
    def _():
        pbuf[:, 7:8, :] = shift0_ref[...]
        state[...] = jnp.zeros(state.shape, F32)
        for bi in seqs:
            for hd in range(n_heads):
                g, j = divmod(hd, GROUP_HEADS)
                blk = slice(j * HEAD_DIM, (j + 1) * HEAD_DIM)
                state[bi * n_groups + g, blk, blk] = wkv0_ref[bi, hd]

    p3 = p_ref[...]
    pbuf[:, 8:8 + chunk, :] = p3
    prev = pbuf[:, 7:7 + chunk, :].reshape(rows, -1)
    last = p3[:, chunk - 1:chunk, :]
    pbuf[:, 7:8, :] = last
    shift_ref[...] = last
    p = p3.reshape(rows, -1)

    h = p + (prev - p) * mu_ref[...]
    r = h[:, 0:rw]
    k = h[:, rw:2 * rw]
    v = h[:, 2 * rw:3 * rw]
    o = 3 * rw
    wd = h[:, o:o + D_DECAY]
    ad = h[:, o + D_DECAY:o + D_DECAY + D_AAA]
    gd = h[:, o + D_DECAY + D_AAA:o + D_DECAY + D_AAA + D_GATE]

    def iota(shape, dim):
        return lax.broadcasted_iota(jnp.int32, shape, dim)

    tr, tc_ = iota((rows, rows), 0), iota((rows, rows), 1)
    tri = jnp.logical_and(tr >= tc_, (tr >> log2c) == (tc_ >> log2c)).astype(BF16)
    t4 = iota((chunk, cw), 0)
    s4 = jnp.bitwise_and(iota((chunk, cw), 1), chunk - 1)
    strict4 = t4 > s4
    incl4 = t4 >= s4
    eye4 = (t4 == s4).astype(F32)
    mask_v = (iota((cw, GROUP_W), 0) >> log2c) == (iota((cw, GROUP_W), 1) >> LOG2_HEAD_DIM)
    mask_p = (iota((cw, cw), 0) >> log2c) == (iota((cw, cw), 1) >> log2c)
    mask_s = ((iota((GROUP_W, GROUP_W), 0) >> LOG2_HEAD_DIM)
              == (iota((GROUP_W, GROUP_W), 1) >> LOG2_HEAD_DIM))
    ones_bd = mask_s.astype(BF16)

    def seg_sum(x):
        xs = jnp.concatenate([x[:, g * GROUP_W:(g + 1) * GROUP_W] for g in range(n_groups)], axis=0)
        hi, lo = _split2(xs)
        st = jnp.concatenate([hi, lo], axis=0).astype(BF16)
        rr = jnp.dot(st, ones_bd, preferred_element_type=F32)
        m = xs.shape[0]
        s = rr[:m] + rr[m:]
        return jnp.concatenate([s[g * rows:(g + 1) * rows] for g in range(n_groups)], axis=1)

    z = -(w0_ref[...] + _mm(jnp.tanh(wd), _parts(w2_ref[...])))
    softplus = jnp.maximum(z, 0.0) + jnp.log(1.0 + jnp.exp(-jnp.abs(z)))
    logw = -jnp.exp(-softplus - 0.5)
    l_hi, l_rest = _split2(logw)
    l_mid, l_lo = _split2(l_rest)
    cs = (jnp.dot(tri, l_hi.astype(BF16), preferred_element_type=F32)
          + jnp.dot(tri, l_mid.astype(BF16), preferred_element_type=F32)
          + jnp.dot(tri, l_lo.astype(BF16), preferred_element_type=F32))
    w_incl = jnp.exp(cs)
    w_excl = jnp.exp(cs - logw)
    w_inv = jnp.exp(-cs)
    w_last = [w_incl[(bi + 1) * chunk - 1:(bi + 1) * chunk, :] for bi in seqs]
    w_last_rows = jnp.concatenate([jnp.broadcast_to(w, (chunk, rw)) for w in w_last], axis=0)

    a = _sigmoid(a0_ref[...] + _mm(ad, _parts(a2_ref[...])))
    g = _mm(_sigmoid(gd), _parts(g2_ref[...]))
    kk = k * kk_ref[...]
    kk = kk * lax.rsqrt(jnp.maximum(seg_sum(kk * kk), 1e-24))
    k2 = k * (1.0 + (a - 1.0) * ka_ref[...])
    at_f = kk * w_excl
    rt_f = r * w_incl
    kh_f = k2 * w_inv
    bh_f = kk * a * w_inv
    khw_f = kh_f * w_last_rows
    bhw_f = bh_f * w_last_rows

    groups = range(nb * n_groups)
    rsl = [slice((gi // n_groups) * chunk, (gi // n_groups + 1) * chunk) for gi in groups]
    gsl = [slice((gi % n_groups) * GROUP_W, (gi % n_groups + 1) * GROUP_W) for gi in groups]
    v_g = [v[rsl[gi], gsl[gi]] for gi in groups]
    ar = [jnp.concatenate([at_f[rsl[gi], gsl[gi]], rt_f[rsl[gi], gsl[gi]]], axis=0)
          for gi in groups]
    gk = [_mm(ar[gi], _block_diag_parts(kh_f[rsl[gi], gsl[gi]], mask_v), NT) for gi in groups]
    gb = [_mm(ar[gi], _block_diag_parts(bh_f[rsl[gi], gsl[gi]], mask_v), NT) for gi in groups]
    s_old = [state[gi] for gi in groups]
    x0 = [_mm(ar[gi], _parts(s_old[gi]), NT) for gi in groups]
    a_ak = [jnp.where(strict4, gk[gi][:chunk], 0.0) for gi in groups]
    a_rk = [jnp.where(incl4, gk[gi][chunk:], 0.0) for gi in groups]
    a_ab = [jnp.where(strict4, gb[gi][:chunk], 0.0) for gi in groups]
    a_rb = [jnp.where(incl4, gb[gi][chunk:], 0.0) for gi in groups]
    av = [_mm(jnp.concatenate([a_ak[gi], a_rk[gi]], axis=0), _block_diag_parts(v_g[gi], mask_v))
          for gi in groups]
    rhs = [x0[gi][:chunk] + av[gi][:chunk] for gi in groups]
    nmat = [-a_ab[gi] for gi in groups]
    tinv = [eye4 + nmat[gi] for gi in groups]
    pw = [_mm(nmat[gi], _block_diag_parts(nmat[gi], mask_p)) for gi in groups]
    for i in range(n_doublings):
        pbd = [_block_diag_parts(pw[gi], mask_p) for gi in groups]
        if i == n_doublings - 1:
            tinv = [tinv[gi] + _mm(tinv[gi], pbd[gi]) for gi in groups]
        else:
            both = [_mm(jnp.concatenate([pw[gi], tinv[gi]], axis=0), pbd[gi]) for gi in groups]
            pw = [both[gi][:chunk] for gi in groups]
            tinv = [tinv[gi] + both[gi][chunk:] for gi in groups]
    u = [_mm(tinv[gi], _block_diag_parts(rhs[gi], mask_v)) for gi in groups]
    ys = [x0[gi][chunk:] + av[gi][chunk:] - _mm(a_rb[gi], _block_diag_parts(u[gi], mask_v))
          for gi in groups]
    for gi in groups:
        rs, gs = rsl[gi], gsl[gi]
        l_hi, l_lo = _split2(jnp.concatenate([v_g[gi], -u[gi]], axis=0))
        r_hi, r_lo = _split2(jnp.concatenate([khw_f[rs, gs], bhw_f[rs, gs]], axis=0))
        lhs = jnp.concatenate([l_hi, l_lo, l_hi], axis=0).astype(BF16)
        rhs3 = jnp.concatenate([r_hi, r_hi, r_lo], axis=0).astype(BF16)
        delta = lax.dot_general(lhs, rhs3, TN, preferred_element_type=F32)
        state[gi] = s_old[gi] * w_last[gi // n_groups][:, gs] + jnp.where(mask_s, delta, 0.0)

    y = jnp.concatenate([jnp.concatenate(ys[bi * n_groups:(bi + 1) * n_groups], axis=1)
                         for bi in seqs], axis=0)
    inv_n = 1.0 / HEAD_DIM
    yc = y - seg_sum(y) * inv_n
    var = seg_sum(yc * yc) * inv_n
    yn = yc * lax.rsqrt(var + GN_EPS) * lng_ref[...] + lnb_ref[...]
    bonus = seg_sum(r * k2 * rk_ref[...]) * v
    yg_ref[...] = ((yn + bonus) * g).astype(BF16).reshape(nb, chunk, rw)

    @pl.when(c == n_chunks - 1)
    def _():
        for bi in seqs:
            for hd in range(n_heads):
                gq, j = divmod(hd, GROUP_HEADS)
                blk = slice(j * HEAD_DIM, (j + 1) * HEAD_DIM)
                wkv_ref[bi, hd] = state[bi * n_groups + gq, blk, blk]


def _rwkv(p_rwkv, shift0, wkv0, prm, batch, seq, chunk, nb):
    t, shift_w = p_rwkv.shape
    n_heads = wkv0.shape[1]
    rw = n_heads * HEAD_DIM
    assert chunk >= 8 and chunk & (chunk - 1) == 0 and seq % chunk == 0 and rw % GROUP_W == 0
    assert batch % nb == 0
    nc = seq // chunk
    row = lambda a: a.reshape(1, -1)
    const = lambda shape: pl.BlockSpec(shape, lambda b, c: (0,) * len(shape))
    yg, wkv, shift = pl.pallas_call(
        functools.partial(_rwkv_kernel, chunk=chunk, n_heads=n_heads, n_chunks=nc, nb=nb),
        out_shape=(jax.ShapeDtypeStruct((batch, seq, rw), BF16),
                   jax.ShapeDtypeStruct(wkv0.shape, F32),
                   jax.ShapeDtypeStruct((batch, 1, shift_w), F32)),
        grid=(batch // nb, nc),
        in_specs=[pl.BlockSpec((nb, chunk, shift_w), lambda b, c: (b, c, 0)),
                  pl.BlockSpec((nb, 1, shift_w), lambda b, c: (b, 0, 0)),
                  pl.BlockSpec((nb, n_heads, HEAD_DIM, HEAD_DIM), lambda b, c: (b, 0, 0, 0)),
                  const((1, shift_w)), const((1, rw)), const((D_DECAY, rw)), const((1, rw)),
                  const((D_AAA, rw)), const((D_GATE, rw)), const((1, rw)), const((1, rw)),
                  const((1, rw)), const((1, rw)), const((1, rw))],
        out_specs=(pl.BlockSpec((nb, chunk, rw), lambda b, c: (b, c, 0)),
                   pl.BlockSpec((nb, n_heads, HEAD_DIM, HEAD_DIM), lambda b, c: (b, 0, 0, 0)),
                   pl.BlockSpec((nb, 1, shift_w), lambda b, c: (b, 0, 0))),
        scratch_shapes=[pltpu.VMEM((nb, 8 + chunk, shift_w), F32),
                        pltpu.VMEM((nb * rw // GROUP_W, GROUP_W, GROUP_W), F32)],
        compiler_params=pltpu.CompilerParams(dimension_semantics=("arbitrary", "arbitrary"),
                                             vmem_limit_bytes=VMEM_LIMIT),
        name="rwkv",
    )(p_rwkv.reshape(batch, seq, shift_w), shift0.reshape(batch, 1, shift_w), wkv0,
      row(prm["mu_shift"]), row(prm["w0"]), prm["w2"], row(prm["a0"]), prm["a2"], prm["g2"],
      row(prm["k_k"]), row(prm["k_a"]), row(prm["r_k"]), row(prm["lnx_g"]), row(prm["lnx_b"]))
    return yg.reshape(t, rw), wkv, shift.reshape(batch, shift_w)


def _mix_kernel(x_ref, pglu_ref, pg_ref, yg_ref, conv0_ref, bglu_ref, wdw_ref, bdw_ref,
                lncg_ref, lncb_ref, wpw2_ref, bpw2_ref, wouta_ref, wo_ref, n2g_ref, wr_ref, br_ref,
                cnt0_ref,
                h_ref, hn_ref, tope_ref, gate_ref, rank_ref, cnt_ref, convn_ref,
                full_ref, base_ref, cf_ref, win_ref, *, nb, tc):
    c = pl.program_id(1)
    ch = x_ref.shape[1]
    hist = CONV_WIDTH - 1
    lo = HIST_ROWS - hist

    @pl.when(jnp.logical_and(pl.program_id(0) == 0, c == 0))
    def _():
        base_ref[...] = cnt0_ref[...]

    @pl.when(c == 0)
    def _():
        full_ref[:, lo:HIST_ROWS, :] = conv0_ref[...]

    @pl.when(c > 0)
    def _():
        full_ref[:, 0:HIST_ROWS, :] = full_ref[:, tc:tc + HIST_ROWS, :]

    z = pglu_ref[...] + bglu_ref[...]
    u = z[:, :ch] * _sigmoid(z[:, ch:])
    full_ref[:, HIST_ROWS:HIST_ROWS + tc, :] = u.reshape(nb, tc, ch)
    convn_ref[...] = full_ref[:, tc + lo:tc + HIST_ROWS, :]

    b_tile, win_rows, _ = win_ref.shape
    r_tile = win_rows - (HIST_ROWS - SUBLANES)
    for b0 in range(0, nb, b_tile):
        for r0 in range(0, tc, r_tile):
            for l0 in range(0, ch, CONV_LANE_TILE):
                ls = slice(l0, l0 + CONV_LANE_TILE)
                acc = jnp.zeros((b_tile, r_tile, CONV_LANE_TILE), F32)
                for s in range(SUBLANES):
                    qs = [q for q in range(HIST_ROWS // SUBLANES + 1)
                          if lo <= SUBLANES * q + s <= lo + CONV_WIDTH - 1]
                    start = r0 + SUBLANES * qs[0] + s
                    n_win = r_tile + SUBLANES * (qs[-1] - qs[0])
                    win_ref[:, 0:n_win, :] = full_ref[b0:b0 + b_tile, start:start + n_win, ls]
                    for q in qs:
                        off = SUBLANES * (q - qs[0])
                        w = SUBLANES * q + s - lo
                        acc = acc + win_ref[:, off:off + r_tile, :] * wdw_ref[w:w + 1, ls]
                cf_ref[b0:b0 + b_tile, r0:r0 + r_tile, ls] = acc + bdw_ref[:, ls]
    cf = cf_ref[...].reshape(nb * tc, ch)
    mu = jnp.mean(cf, axis=-1, keepdims=True)
    cc = cf - mu
    var = jnp.mean(cc * cc, axis=-1, keepdims=True)
    cn = cc * lax.rsqrt(var + LN_EPS) * lncg_ref[...] + lncb_ref[...]
    act = cn * _sigmoid(cn)
    y_b = _dot_bf16(act, wpw2_ref[...]) + bpw2_ref[...]
    y_a = jnp.dot(yg_ref[...], wouta_ref[...], preferred_element_type=F32)
    pg = pg_ref[...]
    merged = _sigmoid(pg[:, :ch]) * y_a + _sigmoid(pg[:, ch:]) * y_b
    h = x_ref[...] + _dot_bf16(merged, wo_ref[...])
    h_ref[...] = h
    hn = _rms_norm(h, n2g_ref[...])
    hn_ref[...] = hn

    logits = _mm(hn, _parts(wr_ref[...])) + br_ref[...]
    lane = lax.broadcasted_iota(jnp.int32, logits.shape, 1)
    vals, idxs = [], []
    l = logits
    for _ in range(TOP_K):
        m = jnp.max(l, axis=-1, keepdims=True)
        idx = jnp.min(jnp.where(l == m, lane, LANES), axis=-1, keepdims=True)
        vals.append(m)
        idxs.append(idx)
        l = jnp.where(lane == idx, -jnp.inf, l)
    exps = [jnp.exp(vk - vals[0]) for vk in vals]
    denom = exps[0]
    for e in exps[1:]:
        denom = denom + e
    te = jnp.zeros(logits.shape, jnp.int32)
    tg = jnp.zeros(logits.shape, F32)
    for kx in range(TOP_K):
        te = jnp.where(lane == kx, idxs[kx], te)
        tg = jnp.where(lane == kx, exps[kx] / denom, tg)
    tope_ref[...] = te
    gate_ref[...] = tg

    rows = logits.shape[0]
    sel = jnp.zeros(logits.shape, F32)
    for kx in range(TOP_K):
        sel = sel + (lane == idxs[kx]).astype(F32)
    earlier = (lax.broadcasted_iota(jnp.int32, (rows, rows), 0)
               > lax.broadcasted_iota(jnp.int32, (rows, rows), 1)).astype(BF16)
    pos = jnp.dot(earlier, sel.astype(BF16), preferred_element_type=F32) + base_ref[...]
    rk = jnp.zeros(logits.shape, jnp.int32)
    for kx in range(TOP_K):
        r_k = jnp.sum(jnp.where(lane == idxs[kx], pos, 0.0), axis=-1, keepdims=True)
        rk = jnp.where(lane == kx, r_k.astype(jnp.int32), rk)
    rank_ref[...] = rk
    total = base_ref[...] + jnp.sum(sel, axis=0, keepdims=True)
    base_ref[...] = total
    cnt_ref[...] = total


def _mix(x2d, p_glu, p_gate, yg, conv0, cnt0, prm, batch, seq, nb, tc):
    t, d = x2d.shape
    ch = conv0.shape[2]
    hist = CONV_WIDTH - 1
    nc = seq // tc
    rows = nb * tc
    r_tile = min(tc, CONV_ROW_TILE)
    win_rows = r_tile + HIST_ROWS - SUBLANES
    b_tile = min(nb, max(1, CONV_WINDOW_ROWS // win_rows))
    assert tc % r_tile == 0 and nb % b_tile == 0 and ch % CONV_LANE_TILE == 0
    row = lambda a: a.reshape(1, -1)
    const = lambda shape: pl.BlockSpec(shape, lambda b, c: (0,) * len(shape))
    tok = lambda width: pl.BlockSpec((rows, width), lambda b, c: (b * nc + c, 0))
    wr = jnp.zeros((d, LANES), F32).at[:, :N_EXPERTS].set(prm["w_router"])
    br = jnp.full((1, LANES), -1e30, F32).at[0, :N_EXPERTS].set(prm["b_router"])
    return pl.pallas_call(
        functools.partial(_mix_kernel, nb=nb, tc=tc),
        out_shape=(jax.ShapeDtypeStruct((t, d), F32),
                   jax.ShapeDtypeStruct((t, d), F32),
                   jax.ShapeDtypeStruct((t, LANES), jnp.int32),
                   jax.ShapeDtypeStruct((t, LANES), F32),
                   jax.ShapeDtypeStruct((t, LANES), jnp.int32),
                   jax.ShapeDtypeStruct((1, LANES), F32),
                   jax.ShapeDtypeStruct((batch, hist, ch), F32)),
        grid=(batch // nb, nc),
        in_specs=[tok(d), tok(2 * ch), tok(2 * d), tok(d),
                  pl.BlockSpec((nb, hist, ch), lambda b, c: (b, 0, 0)),
                  const((1, 2 * ch)), const((CONV_WIDTH, ch)), const((1, ch)), const((1, ch)),
                  const((1, ch)), const((ch, d)), const((1, d)), const((d, d)), const((d, d)),
                  const((1, d)), const((d, LANES)), const((1, LANES)), const((1, LANES))],
        out_specs=(tok(d), tok(d), tok(LANES), tok(LANES), tok(LANES), const((1, LANES)),
                   pl.BlockSpec((nb, hist, ch), lambda b, c: (b, 0, 0))),
        scratch_shapes=[pltpu.VMEM((nb, HIST_ROWS + tc, ch), F32), pltpu.VMEM((1, LANES), F32),
                        pltpu.VMEM((nb, tc, ch), F32),
                        pltpu.VMEM((b_tile, win_rows, CONV_LANE_TILE), F32)],
        compiler_params=pltpu.CompilerParams(dimension_semantics=("arbitrary", "arbitrary"),
                                             vmem_limit_bytes=VMEM_LIMIT),
        name="mix",
    )(x2d, p_glu, p_gate, yg, conv0,
      row(prm["b_glu"]), prm["w_dw"].reshape(CONV_WIDTH, ch), row(prm["b_dw"]),
      row(prm["ln_conv_g"]), row(prm["ln_conv_b"]), prm["w_pw2"].astype(BF16), row(prm["b_pw2"]),
      prm["w_out_a"].astype(BF16), prm["w_o"].astype(BF16), row(prm["norm2_g"]), wr, br, cnt0)


def _expert_kernel(blk_e_ref, nused_ref, run_par_ref, next_e_ref, tok_cur_ref, tok_nxt_ref, hn_hbm,
                   wup_hbm, bup_ref, wdn_hbm, bdn_ref, out_ref,
                   xbuf0, xbuf1, wup_f32, wdn_f32, wup_bf, wdn_bf, sem, wsem, *, n_blocks):
    i = pl.program_id(0)
    n_used = nused_ref[0]
    rows = xbuf0.shape[0]
    d_ff = wdn_bf.shape[0]
    assert wup_bf.shape[0] == SUBLANES * LANES

    def weight_copies(e, par):
        return (pltpu.make_async_copy(wup_hbm.at[e], wup_f32.at[par], wsem.at[0, par]),
                pltpu.make_async_copy(wdn_hbm.at[e], wdn_f32.at[par], wsem.at[1, par]))

    def start_gather(tok_ref, buf, s):
        for r in range(rows):
            pltpu.make_async_copy(hn_hbm.at[tok_ref[0, 0, r]], buf.at[r], sem.at[s]).start()

    def wait_gather(buf, s):
        pltpu.make_async_copy(hn_hbm.at[pl.ds(0, rows)], buf, sem.at[s]).wait()

    def token_rows(buf):
        return jnp.concatenate([buf[:, j, :] for j in range(SUBLANES)], axis=1)

    @pl.when(i == 0)
    def _():
        for cp in weight_copies(blk_e_ref[0], 0):
            cp.start()
        start_gather(tok_cur_ref, xbuf0, 0)

    @pl.when(i >= n_used)
    def _():
        out_ref[...] = jnp.zeros(out_ref.shape, F32)

    def step(cur, s_cur, nxt, s_nxt):
        @pl.when(i <= n_used)
        def _():
            wait_gather(cur, s_cur)

        @pl.when(i < n_used)
        def _():
            e = blk_e_ref[i]
            prev_e = blk_e_ref[jnp.maximum(i - 1, 0)]
            @pl.when(jnp.logical_or(i == 0, e != prev_e))
            def _():
                par = run_par_ref[i]
                for cp in weight_copies(e, par):
                    cp.wait()
                wup_bf[...] = wup_f32[par].astype(BF16)
                wdn_bf[...] = wdn_f32[par].astype(BF16)
                next_e = next_e_ref[i]
                @pl.when(next_e >= 0)
                def _():
                    for cp in weight_copies(next_e, 1 - par):
                        cp.start()

            start_gather(tok_nxt_ref, nxt, s_nxt)
            hb = (jnp.dot(token_rows(cur).astype(BF16), wup_bf[...], preferred_element_type=F32)
                  + bup_ref[0])
            glu = jnp.minimum(hb[:, :d_ff], SWIGLU_LIMIT)
            lin = jnp.clip(hb[:, d_ff:], -SWIGLU_LIMIT, SWIGLU_LIMIT)
            act = glu * _sigmoid(SWIGLU_ALPHA * glu) * (lin + 1.0)
            out_ref[...] = (jnp.dot(act.astype(BF16), wdn_bf[...], preferred_element_type=F32)
                            + bdn_ref[0])

            @pl.when(i == n_blocks - 1)
            def _():
                wait_gather(nxt, s_nxt)

    @pl.when(i % 2 == 0)
    def _():
        step(xbuf0, 0, xbuf1, 1)

    @pl.when(i % 2 == 1)
    def _():
        step(xbuf1, 1, xbuf0, 0)


def _experts(hn, slot_tok, blk_e, n_used, run_par, next_e, w_up, b_up, w_down, b_down):
    t, d = hn.shape
    n_blocks = blk_e.shape[0]
    rows = EXPERT_BLOCK
    n_exp, _, ff2 = w_up.shape
    d_ff = w_down.shape[1]
    grid_spec = pltpu.PrefetchScalarGridSpec(
        num_scalar_prefetch=4,
        grid=(n_blocks,),
        in_specs=[pl.BlockSpec((1, 1, rows), lambda i, *_: (i, 0, 0), memory_space=pltpu.SMEM),
                  pl.BlockSpec((1, 1, rows), lambda i, *_: (jnp.minimum(i + 1, n_blocks - 1), 0, 0),
                               memory_space=pltpu.SMEM),
                  pl.BlockSpec(memory_space=pl.ANY),
                  pl.BlockSpec(memory_space=pl.ANY),
                  pl.BlockSpec((1, 1, ff2), lambda i, be, *_: (be[i], 0, 0)),
                  pl.BlockSpec(memory_space=pl.ANY),
                  pl.BlockSpec((1, 1, d), lambda i, be, *_: (be[i], 0, 0))],
        out_specs=pl.BlockSpec((rows, d), lambda i, *_: (i, 0)),
        scratch_shapes=[pltpu.VMEM((rows, SUBLANES, LANES), F32),
                        pltpu.VMEM((rows, SUBLANES, LANES), F32),
                        pltpu.VMEM((2, d, ff2), F32),
                        pltpu.VMEM((2, d_ff, d), F32),
                        pltpu.VMEM((d, ff2), BF16),
                        pltpu.VMEM((d_ff, d), BF16),
                        pltpu.SemaphoreType.DMA((2,)),
                        pltpu.SemaphoreType.DMA((2, 2))],
    )
    slot_tok3 = slot_tok.reshape(n_blocks, 1, rows)
    return pl.pallas_call(
        functools.partial(_expert_kernel, n_blocks=n_blocks),
        out_shape=jax.ShapeDtypeStruct((n_blocks * rows, d), F32),
        grid_spec=grid_spec,
        compiler_params=pltpu.CompilerParams(dimension_semantics=("arbitrary",),
                                             vmem_limit_bytes=VMEM_LIMIT),
        name="experts",
    )(blk_e, n_used, run_par, next_e, slot_tok3, slot_tok3, hn.reshape(t, SUBLANES, LANES),
      w_up, b_up.reshape(n_exp, 1, ff2), w_down, b_down.reshape(n_exp, 1, d))


def _combine_kernel(dest_cur_ref, dest_nxt_ref, gate_ref, h_ref, ys_hbm, g_ref, out_ref, buf, sem,
                    *, n):
    i = pl.program_id(0)
    rows = h_ref.shape[0]
    slot = i % 2

    groups = rows // SUBLANES

    def start_gather(dest_ref, s):
        def body(it, carry):
            for j in range(SUBLANES):
                for kx in range(TOP_K):
                    src_row = dest_ref[0, 0, (it * SUBLANES + j) * TOP_K + kx]
                    pltpu.make_async_copy(
                        ys_hbm.at[src_row >> LOG2_SUBLANES, pl.ds(src_row & (SUBLANES - 1), 1), :],
                        buf.at[(s * TOP_K + kx) * groups + it, pl.ds(j, 1), :], sem.at[s]).start()
            return carry

        lax.fori_loop(0, groups, body, 0)

    def choice_rows(s, kx):
        return buf.at[pl.ds((s * TOP_K + kx) * groups, groups)]

    def wait_gather(s):
        for kx in range(TOP_K):
            pltpu.make_async_copy(ys_hbm.at[pl.ds(0, groups)], choice_rows(s, kx), sem.at[s]).wait()

    @pl.when(i == 0)
    def _():
        start_gather(dest_cur_ref, 0)

    @pl.when(i + 1 < n)
    def _():
        start_gather(dest_nxt_ref, 1 - slot)

    wait_gather(slot)
    gates = gate_ref[...]
    y = None
    for kx in range(TOP_K):
        term = choice_rows(slot, kx)[...].reshape(rows, -1) * gates[:, kx:kx + 1]
        y = term if y is None else y + term
    out_ref[...] = _rms_norm(h_ref[...] + y, g_ref[...])


def _combine(dest, gates, h, ys, norm_g):
    t, d = h.shape
    rows = ROW_BLOCK
    nblk = t // rows
    dest3 = dest.reshape(nblk, 1, rows * TOP_K)
    return pl.pallas_call(
        functools.partial(_combine_kernel, n=nblk),
        out_shape=jax.ShapeDtypeStruct((t, d), F32),
        grid=(nblk,),
        in_specs=[pl.BlockSpec((1, 1, rows * TOP_K), lambda i: (i, 0, 0), memory_space=pltpu.SMEM),
                  pl.BlockSpec((1, 1, rows * TOP_K), lambda i: (jnp.minimum(i + 1, nblk - 1), 0, 0),
                               memory_space=pltpu.SMEM),
                  pl.BlockSpec((rows, LANES), lambda i: (i, 0)),
                  pl.BlockSpec((rows, d), lambda i: (i, 0)),
                  pl.BlockSpec(memory_space=pl.ANY),
                  pl.BlockSpec((1, d), lambda i: (0, 0))],
        out_specs=pl.BlockSpec((rows, d), lambda i: (i, 0)),
        scratch_shapes=[pltpu.VMEM((2 * TOP_K * rows // SUBLANES, SUBLANES, d), F32),
                        pltpu.SemaphoreType.DMA((2,))],
        compiler_params=pltpu.CompilerParams(dimension_semantics=("arbitrary",),
                                             vmem_limit_bytes=VMEM_LIMIT),
        name="combine",
    )(dest3, dest3, gates, h, ys.reshape(-1, SUBLANES, d), norm_g.reshape(1, d))


def _routing_tables(top_e, rank, counts):
    t = top_e.shape[0]
    n_assign = t * TOP_K
    padded = (counts + EXPERT_BLOCK - 1) // EXPERT_BLOCK * EXPERT_BLOCK
    pad_end = jnp.cumsum(padded)
    pad_start = pad_end - padded
    experts = jnp.arange(N_EXPERTS, dtype=jnp.int32)
    start_of = jnp.sum(jnp.where(top_e[..., None] == experts, pad_start, 0), axis=-1)
    dest = (start_of + rank).astype(jnp.int32)
    n_blocks = -(-n_assign // EXPERT_BLOCK) + N_EXPERTS
    flat_tok = jnp.arange(n_assign, dtype=jnp.int32) // TOP_K
    slot_tok = jnp.zeros((n_blocks * EXPERT_BLOCK,), jnp.int32).at[dest.reshape(-1)].set(flat_tok)
    blk_start = jnp.arange(n_blocks, dtype=jnp.int32) * EXPERT_BLOCK
    blk_e = jnp.minimum(jnp.sum((blk_start[:, None] >= pad_end[None, :]).astype(jnp.int32), axis=1),
                        N_EXPERTS - 1)
    n_used = (pad_end[-1:] // EXPERT_BLOCK).astype(jnp.int32)
    nonempty = counts > 0
    order = jnp.cumsum(nonempty.astype(jnp.int32)) - 1
    later = jnp.where(nonempty, experts, N_EXPERTS)
    next_nonempty = jnp.concatenate(
        [lax.cummin(later[::-1], axis=0)[::-1][1:], jnp.full((1,), N_EXPERTS, jnp.int32)])
    next_nonempty = jnp.where(next_nonempty >= N_EXPERTS, -1, next_nonempty)
    run_par = (order % 2)[blk_e].astype(jnp.int32)
    next_e = next_nonempty[blk_e].astype(jnp.int32)
    return dest, slot_tok, blk_e, n_used, run_par, next_e


def _mixers(x, wkv0, shift0, conv0, cnt0, prm, w_in_bf16, chunk, seqs_per_step, nb, tc):
    batch, seq, d = x.shape
    shift_w = shift0.shape[1]
    glu_w = 2 * conv0.shape[2]
    x2d = x.reshape(batch * seq, d)
    p_rwkv, p_glu, p_gate = _in_proj(x2d, prm["norm1_g"], w_in_bf16, shift_w, glu_w)
    yg, wkv_new, shift_new = _rwkv(p_rwkv, shift0, wkv0, prm, batch, seq, chunk, seqs_per_step)
    h, hn, top_e, gates, rank, cnt, conv_new = _mix(x2d, p_glu, p_gate, yg, conv0, cnt0, prm,
                                                    batch, seq, nb, tc)
    return h, hn, top_e, gates, rank, cnt, wkv_new, shift_new, conv_new


def kernel(x_prompt, x_sample, state_wkv, state_shift, state_conv, norm1_g, w_in, b_glu, mu_shift, w0, w2, a0, a2, g2, k_k, k_a, r_k, lnx_g, lnx_b, w_out_a, w_dw, b_dw, ln_conv_g, ln_conv_b, w_pw2, b_pw2, w_o, norm2_g, w_router, b_router, w_up, b_up, w_down, b_down, norm_f_g):
    prm = dict(norm1_g=norm1_g, b_glu=b_glu, mu_shift=mu_shift, w0=w0, w2=w2, a0=a0, a2=a2,
               g2=g2, k_k=k_k, k_a=k_a, r_k=r_k, lnx_g=lnx_g, lnx_b=lnx_b, w_out_a=w_out_a,
               w_dw=w_dw, b_dw=b_dw, ln_conv_g=ln_conv_g, ln_conv_b=ln_conv_b, w_pw2=w_pw2,
               b_pw2=b_pw2, w_o=w_o, norm2_g=norm2_g, w_router=w_router, b_router=b_router)
    bp, lp, d = x_prompt.shape
    bs, ls, _ = x_sample.shape
    n_heads = state_wkv.shape[1]
    shift_w = state_shift.shape[1]
    ch = state_conv.shape[2]
    w_in_bf16 = w_in.astype(BF16)

    zeros_wkv = jnp.zeros((bp, n_heads, HEAD_DIM, HEAD_DIM), F32)
    zeros_shift = jnp.zeros((bp, shift_w), F32)
    zeros_conv = jnp.zeros((bp, CONV_WIDTH - 1, ch), F32)
    zero_cnt = jnp.zeros((1, LANES), F32)
    hp, hnp, tep, gp, rkp, cnt_p, wkv_p, shift_p, conv_p = _mixers(
        x_prompt, zeros_wkv, zeros_shift, zeros_conv, zero_cnt, prm, w_in_bf16,
        chunk=min(64, lp), seqs_per_step=PROMPT_SEQS_PER_STEP, nb=1, tc=min(ROW_BLOCK, lp))
    hs, hns, tes, gs, rks, cnt, wkv_s, shift_s, conv_s = _mixers(
        x_sample, state_wkv, state_shift, state_conv, cnt_p, prm, w_in_bf16,
        chunk=ls, seqs_per_step=SAMPLE_SEQS_PER_STEP, nb=ROW_BLOCK // ls, tc=ls)

    hn = jnp.concatenate([hnp, hns], axis=0)
    top_e = jnp.concatenate([tep[:, :TOP_K], tes[:, :TOP_K]], axis=0)
    rank = jnp.concatenate([rkp[:, :TOP_K], rks[:, :TOP_K]], axis=0)
    counts = cnt[0, :N_EXPERTS].astype(jnp.int32)
    dest, slot_tok, blk_e, n_used, run_par, next_e = _routing_tables(top_e, rank, counts)
    ys = _experts(hn, slot_tok, blk_e, n_used, run_par, next_e, w_up, b_up, w_down, b_down)
    tp = bp * lp
    y_p = _combine(dest[:tp], gp, hp, ys, norm_f_g)
    y_s = _combine(dest[tp:], gs, hs, ys, norm_f_g)
    return (y_p.reshape(bp, lp, d), y_s.reshape(bs, ls, d),
            wkv_p, shift_p, conv_p, wkv_s, shift_s, conv_s)
```

```python
import functools

import jax
import jax.numpy as jnp
from jax import lax
from jax.experimental import pallas as pl
from jax.experimental.pallas import tpu as pltpu

F32 = jnp.float32
BF16 = jnp.bfloat16

HEAD_DIM = 64
D_DECAY = 64
D_AAA = 64
D_GATE = 128
CONV_WIDTH = 31
N_EXPERTS = 32
TOP_K = 4
SWIGLU_LIMIT = 7.0
SWIGLU_ALPHA = 1.702
RMS_EPS = 1e-6
LN_EPS = 1e-5
GN_EPS = 64e-5

LANES = 128
SUBLANES = 8
LOG2_SUBLANES = 3
HIST_ROWS = 32
ROW_BLOCK = 256
EXPERT_BLOCK = 512
GROUP_HEADS = 4
GROUP_W = GROUP_HEADS * HEAD_DIM
LOG2_HEAD_DIM = HEAD_DIM.bit_length() - 1
PROMPT_SEQS_PER_STEP = 4
SAMPLE_SEQS_PER_STEP = 4
CONV_ROW_TILE = 64
CONV_LANE_TILE = 256
CONV_WINDOW_ROWS = 128
VMEM_LIMIT = 56 * 1024 * 1024


def _dot_bf16(a, b):
    return jnp.dot(a.astype(BF16), b, preferred_element_type=F32)


def _sigmoid(x):
    return 1.0 / (1.0 + jnp.exp(-x))


def _rms_norm(x, g):
    return x * lax.rsqrt(jnp.mean(x * x, axis=-1, keepdims=True) + RMS_EPS) * g


def _inproj_kernel(x_ref, g_ref, w_ref, o_rwkv, o_glu, o_gate, *, shift_w, glu_w):
    xb = _rms_norm(x_ref[...], g_ref[...]).astype(BF16)
    o_rwkv[...] = jnp.dot(xb, w_ref[:, :shift_w], preferred_element_type=F32)
    o_glu[...] = jnp.dot(xb, w_ref[:, shift_w:shift_w + glu_w], preferred_element_type=F32)
    o_gate[...] = jnp.dot(xb, w_ref[:, shift_w + glu_w:], preferred_element_type=F32)


def _in_proj(x2d, norm_g, w_in_bf16, shift_w, glu_w):
    t, d = x2d.shape
    in_w = w_in_bf16.shape[1]
    gate_w = in_w - shift_w - glu_w
    tm = ROW_BLOCK
    return pl.pallas_call(
        functools.partial(_inproj_kernel, shift_w=shift_w, glu_w=glu_w),
        out_shape=(jax.ShapeDtypeStruct((t, shift_w), F32),
                   jax.ShapeDtypeStruct((t, glu_w), F32),
                   jax.ShapeDtypeStruct((t, gate_w), F32)),
        grid=(t // tm,),
        in_specs=[pl.BlockSpec((tm, d), lambda i: (i, 0)),
                  pl.BlockSpec((1, d), lambda i: (0, 0)),
                  pl.BlockSpec((d, in_w), lambda i: (0, 0), pipeline_mode=pl.Buffered(1))],
        out_specs=(pl.BlockSpec((tm, shift_w), lambda i: (i, 0)),
                   pl.BlockSpec((tm, glu_w), lambda i: (i, 0)),
                   pl.BlockSpec((tm, gate_w), lambda i: (i, 0))),
        compiler_params=pltpu.CompilerParams(dimension_semantics=("arbitrary",),
                                             vmem_limit_bytes=VMEM_LIMIT),
        name="in_proj",
    )(x2d, norm_g.reshape(1, d), w_in_bf16)


NN = (((1,), (0,)), ((), ()))
NT = (((1,), (1,)), ((), ()))
TN = (((0,), (0,)), ((), ()))


def _split2(x):
    hi = x.astype(BF16).astype(F32)
    return hi, x - hi


def _parts(x):
    hi, lo = _split2(x)
    return hi.astype(BF16), lo.astype(BF16)


def _mm(a, b_parts, dims=NN):
    m = a.shape[0]
    a_hi, a_lo = _split2(a)
    a_st = jnp.concatenate([a_hi, a_lo], axis=0).astype(BF16)
    b_hi, b_lo = b_parts
    r = lax.dot_general(a_st, b_hi, dims, preferred_element_type=F32)
    return r[:m] + r[m:] + lax.dot_general(a_st[:m], b_lo, dims, preferred_element_type=F32)


def _block_diag_parts(x, mask):
    hi, lo = _split2(x)
    tile = lambda q: jnp.where(mask, jnp.concatenate([q] * GROUP_HEADS, axis=0), 0.0).astype(BF16)
    return tile(hi), tile(lo)


def _rwkv_kernel(p_ref, shift0_ref, wkv0_ref, mu_ref, w0_ref, w2_ref, a0_ref, a2_ref, g2_ref,
                 kk_ref, ka_ref, rk_ref, lng_ref, lnb_ref,
                 yg_ref, wkv_ref, shift_ref,
                 pbuf, state, *, chunk, n_heads, n_chunks, nb):
    c = pl.program_id(1)
    rw = n_heads * HEAD_DIM
    n_groups = rw // GROUP_W
    cw = GROUP_HEADS * chunk
    log2c = chunk.bit_length() - 1
    n_doublings = log2c - 1

    rows = nb * chunk
    seqs = range(nb)

    @pl.when(c == 0)
    def _():
        pbuf[:, 7:8, :] = shift0_ref[...]
        state[...] = jnp.zeros(state.shape, F32)
        for bi in seqs:
            for hd in range(n_heads):
                g, j = divmod(hd, GROUP_HEADS)
                blk = slice(j * HEAD_DIM, (j + 1) * HEAD_DIM)
                state[bi * n_groups + g, blk, blk] = wkv0_ref[bi, hd]

    p3 = p_ref[...]
    pbuf[:, 8:8 + chunk, :] = p3
    prev = pbuf[:, 7:7 + chunk, :].reshape(rows, -1)
    last = p3[:, chunk - 1:chunk, :]
    pbuf[:, 7:8, :] = last
    shift_ref[...] = last
    p = p3.reshape(rows, -1)

    h = p + (prev - p) * mu_ref[...]
    r = h[:, 0:rw]
    k = h[:, rw:2 * rw]
    v = h[:, 2 * rw:3 * rw]
    o = 3 * rw
    wd = h[:, o:o + D_DECAY]
    ad = h[:, o + D_DECAY:o + D_DECAY + D_AAA]
    gd = h[:, o + D_DECAY + D_AAA:o + D_DECAY + D_AAA + D_GATE]

    def iota(shape, dim):
        return lax.broadcasted_iota(jnp.int32, shape, dim)

    tr, tc_ = iota((rows, rows), 0), iota((rows, rows), 1)
    tri = jnp.logical_and(tr >= tc_, (tr >> log2c) == (tc_ >> log2c)).astype(BF16)
    t4 = iota((chunk, cw), 0)
    s4 = jnp.bitwise_and(iota((chunk, cw), 1), chunk - 1)
    strict4 = t4 > s4
    incl4 = t4 >= s4
    eye4 = (t4 == s4).astype(F32)
    mask_v = (iota((cw, GROUP_W), 0) >> log2c) == (iota((cw, GROUP_W), 1) >> LOG2_HEAD_DIM)
    mask_p = (iota((cw, cw), 0) >> log2c) == (iota((cw, cw), 1) >> log2c)
    mask_s = ((iota((GROUP_W, GROUP_W), 0) >> LOG2_HEAD_DIM)
              == (iota((GROUP_W, GROUP_W), 1) >> LOG2_HEAD_DIM))
    ones_bd = mask_s.astype(BF16)

    def seg_sum(x):
        xs = jnp.concatenate([x[:, g * GROUP_W:(g + 1) * GROUP_W] for g in range(n_groups)], axis=0)
        hi, lo = _split2(xs)
        st = jnp.concatenate([hi, lo], axis=0).astype(BF16)
        rr = jnp.dot(st, ones_bd, preferred_element_type=F32)
        m = xs.shape[0]
        s = rr[:m] + rr[m:]
        return jnp.concatenate([s[g * rows:(g + 1) * rows] for g in range(n_groups)], axis=1)

    z = -(w0_ref[...] + _mm(jnp.tanh(wd), _parts(w2_ref[...])))
    softplus = jnp.maximum(z, 0.0) + jnp.log(1.0 + jnp.exp(-jnp.abs(z)))
    logw = -jnp.exp(-softplus - 0.5)
    l_hi, l_rest = _split2(logw)
    l_mid, l_lo = _split2(l_rest)
    cs = (jnp.dot(tri, l_hi.astype(BF16), preferred_element_type=F32)
          + jnp.dot(tri, l_mid.astype(BF16), preferred_element_type=F32)
          + jnp.dot(tri, l_lo.astype(BF16), preferred_element_type=F32))
    w_incl = jnp.exp(cs)
    w_excl = jnp.exp(cs - logw)
    w_inv = jnp.exp(-cs)
    w_last = [w_incl[(bi + 1) * chunk - 1:(bi + 1) * chunk, :] for bi in seqs]
    w_last_rows = jnp.concatenate([jnp.broadcast_to(w, (chunk, rw)) for w in w_last], axis=0)

    a = _sigmoid(a0_ref[...] + _mm(ad, _parts(a2_ref[...])))
    g = _mm(_sigmoid(gd), _parts(g2_ref[...]))
    kk = k * kk_ref[...]
    kk = kk * lax.rsqrt(jnp.maximum(seg_sum(kk * kk), 1e-24))
    k2 = k * (1.0 + (a - 1.0) * ka_ref[...])
    at_f = kk * w_excl
    rt_f = r * w_incl
    kh_f = k2 * w_inv
    bh_f = kk * a * w_inv
    khw_f = kh_f * w_last_rows
    bhw_f = bh_f * w_last_rows

    groups = range(nb * n_groups)
    rsl = [slice((gi // n_groups) * chunk, (gi // n_groups + 1) * chunk) for gi in groups]
    gsl = [slice((gi % n_groups) * GROUP_W, (gi % n_groups + 1) * GROUP_W) for gi in groups]
    v_g = [v[rsl[gi], gsl[gi]] for gi in groups]
    ar = [jnp.concatenate([at_f[rsl[gi], gsl[gi]], rt_f[rsl[gi], gsl[gi]]], axis=0)
          for gi in groups]
    gk = [_mm(ar[gi], _block_diag_parts(kh_f[rsl[gi], gsl[gi]], mask_v), NT) for gi in groups]
    gb = [_mm(ar[gi], _block_diag_parts(bh_f[rsl[gi], gsl[gi]], mask_v), NT) for gi in groups]
    s_old = [state[gi] for gi in groups]
    x0 = [_mm(ar[gi], _parts(s_old[gi]), NT) for gi in groups]
    a_ak = [jnp.where(strict4, gk[gi][:chunk], 0.0) for gi in groups]
    a_rk = [jnp.where(incl4, gk[gi][chunk:], 0.0) for gi in groups]
    a_ab = [jnp.where(strict4, gb[gi][:chunk], 0.0) for gi in groups]
    a_rb = [jnp.where(incl4, gb[gi][chunk:], 0.0) for gi in groups]
    av = [_mm(jnp.concatenate([a_ak[gi], a_rk[gi]], axis=0), _block_diag_parts(v_g[gi], mask_v))
          for gi in groups]
    rhs = [x0[gi][:chunk] + av[gi][:chunk] for gi in groups]
    nmat = [-a_ab[gi] for gi in groups]
    tinv = [eye4 + nmat[gi] for gi in groups]
    pw = [_mm(nmat[gi], _block_diag_parts(nmat[gi], mask_p)) for gi in groups]
    for i in range(n_doublings):
        pbd = [_block_diag_parts(pw[gi], mask_p) for gi in groups]
        if i == n_doublings - 1:
            tinv = [tinv[gi] + _mm(tinv[gi], pbd[gi]) for gi in groups]
        else:
            both = [_mm(jnp.concatenate([pw[gi], tinv[gi]], axis=0), pbd[gi]) for gi in groups]
            pw = [both[gi][:chunk] for gi in groups]
            tinv = [tinv[gi] + both[gi][chunk:] for gi in groups]
    u = [_mm(tinv[gi], _block_diag_parts(rhs[gi], mask_v)) for gi in groups]
    ys = [x0[gi][chunk:] + av[gi][chunk:] - _mm(a_rb[gi], _block_diag_parts(u[gi], mask_v))
          for gi in groups]
    for gi in groups:
        rs, gs = rsl[gi], gsl[gi]
        l_hi, l_lo = _split2(jnp.concatenate([v_g[gi], -u[gi]], axis=0))
        r_hi, r_lo = _split2(jnp.concatenate([khw_f[rs, gs], bhw_f[rs, gs]], axis=0))
        lhs = jnp.concatenate([l_hi, l_lo, l_hi], axis=0).astype(BF16)
        rhs3 = jnp.concatenate([r_hi, r_hi, r_lo], axis=0).astype(BF16)
        delta = lax.dot_general(lhs, rhs3, TN, preferred_element_type=F32)
        state[gi] = s_old[gi] * w_last[gi // n_groups][:, gs] + jnp.where(mask_s, delta, 0.0)

    y = jnp.concatenate([jnp.concatenate(ys[bi * n_groups:(bi + 1) * n_groups], axis=1)
                         for bi in seqs], axis=0)
    inv_n = 1.0 / HEAD_DIM
    yc = y - seg_sum(y) * inv_n
    var = seg_sum(yc * yc) * inv_n
    yn = yc * lax.rsqrt(var + GN_EPS) * lng_ref[...] + lnb_ref[...]
    bonus = seg_sum(r * k2 * rk_ref[...]) * v
    yg_ref[...] = ((yn + bonus) * g).astype(BF16).reshape(nb, chunk, rw)

    @pl.when(c == n_chunks - 1)
    def _():
        for bi in seqs:
            for hd in range(n_heads):
                gq, j = divmod(hd, GROUP_HEADS)
                blk = slice(j * HEAD_DIM, (j + 1) * HEAD_DIM)
                wkv_ref[bi, hd] = state[bi * n_groups + gq, blk, blk]


def _rwkv(p_rwkv, shift0, wkv0, prm, batch, seq, chunk, nb):
    t, shift_w = p_rwkv.shape
    n_heads = wkv0.shape[1]
    rw = n_heads * HEAD_DIM
    assert chunk >= 8 and chunk & (chunk - 1) == 0 and seq % chunk == 0 and rw % GROUP_W == 0
    assert batch % nb == 0
    nc = seq // chunk
    row = lambda a: a.reshape(1, -1)
    const = lambda shape: pl.BlockSpec(shape, lambda b, c: (0,) * len(shape))
    yg, wkv, shift = pl.pallas_call(
        functools.partial(_rwkv_kernel, chunk=chunk, n_heads=n_heads, n_chunks=nc, nb=nb),
        out_shape=(jax.ShapeDtypeStruct((batch, seq, rw), BF16),
                   jax.ShapeDtypeStruct(wkv0.shape, F32),
                   jax.ShapeDtypeStruct((batch, 1, shift_w), F32)),
        grid=(batch // nb, nc),
        in_specs=[pl.BlockSpec((nb, chunk, shift_w), lambda b, c: (b, c, 0)),
                  pl.BlockSpec((nb, 1, shift_w), lambda b, c: (b, 0, 0)),
                  pl.BlockSpec((nb, n_heads, HEAD_DIM, HEAD_DIM), lambda b, c: (b, 0, 0, 0)),
                  const((1, shift_w)), const((1, rw)), const((D_DECAY, rw)), const((1, rw)),
                  const((D_AAA, rw)), const((D_GATE, rw)), const((1, rw)), const((1, rw)),
                  const((1, rw)), const((1, rw)), const((1, rw))],
        out_specs=(pl.BlockSpec((nb, chunk, rw), lambda b, c: (b, c, 0)),
                   pl.BlockSpec((nb, n_heads, HEAD_DIM, HEAD_DIM), lambda b, c: (b, 0, 0, 0)),
                   pl.BlockSpec((nb, 1, shift_w), lambda b, c: (b, 0, 0))),
        scratch_shapes=[pltpu.VMEM((nb, 8 + chunk, shift_w), F32),
                        pltpu.VMEM((nb * rw // GROUP_W, GROUP_W, GROUP_W), F32)],
        compiler_params=pltpu.CompilerParams(dimension_semantics=("arbitrary", "arbitrary"),
                                             vmem_limit_bytes=VMEM_LIMIT),
        name="rwkv",
    )(p_rwkv.reshape(batch, seq, shift_w), shift0.reshape(batch, 1, shift_w), wkv0,
      row(prm["mu_shift"]), row(prm["w0"]), prm["w2"], row(prm["a0"]), prm["a2"], prm["g2"],
      row(prm["k_k"]), row(prm["k_a"]), row(prm["r_k"]), row(prm["lnx_g"]), row(prm["lnx_b"]))
    return yg.reshape(t, rw), wkv, shift.reshape(batch, shift_w)


def _mix_kernel(x_ref, pglu_ref, pg_ref, yg_ref, conv0_ref, bglu_ref, wdw_ref, bdw_ref,
                lncg_ref, lncb_ref, wpw2_ref, bpw2_ref, wouta_ref, wo_ref, n2g_ref, wr_ref, br_ref,
                cnt0_ref,
                h_ref, hn_ref, tope_ref, gate_ref, rank_ref, cnt_ref, convn_ref,
                full_ref, base_ref, cf_ref, win_ref, *, nb, tc):
    c = pl.program_id(1)
    ch = x_ref.shape[1]
    hist = CONV_WIDTH - 1
    lo = HIST_ROWS - hist

    @pl.when(jnp.logical_and(pl.program_id(0) == 0, c == 0))
    def _():
        base_ref[...] = cnt0_ref[...]

    @pl.when(c == 0)
    def _():
        full_ref[:, lo:HIST_ROWS, :] = conv0_ref[...]

    @pl.when(c > 0)
    def _():
        full_ref[:, 0:HIST_ROWS, :] = full_ref[:, tc:tc + HIST_ROWS, :]

    z = pglu_ref[...] + bglu_ref[...]
    u = z[:, :ch] * _sigmoid(z[:, ch:])
    full_ref[:, HIST_ROWS:HIST_ROWS + tc, :] = u.reshape(nb, tc, ch)
    convn_ref[...] = full_ref[:, tc + lo:tc + HIST_ROWS, :]

    b_tile, win_rows, _ = win_ref.shape
    r_tile = win_rows - (HIST_ROWS - SUBLANES)
    for b0 in range(0, nb, b_tile):
        for r0 in range(0, tc, r_tile):
            for l0 in range(0, ch, CONV_LANE_TILE):
                ls = slice(l0, l0 + CONV_LANE_TILE)
                acc = jnp.zeros((b_tile, r_tile, CONV_LANE_TILE), F32)
                for s in range(SUBLANES):
                    qs = [q for q in range(HIST_ROWS // SUBLANES + 1)
                          if lo <= SUBLANES * q + s <= lo + CONV_WIDTH - 1]
                    start = r0 + SUBLANES * qs[0] + s
                    n_win = r_tile + SUBLANES * (qs[-1] - qs[0])
                    win_ref[:, 0:n_win, :] = full_ref[b0:b0 + b_tile, start:start + n_win, ls]
                    for q in qs:
                        off = SUBLANES * (q - qs[0])
                        w = SUBLANES * q + s - lo
                        acc = acc + win_ref[:, off:off + r_tile, :] * wdw_ref[w:w + 1, ls]
                cf_ref[b0:b0 + b_tile, r0:r0 + r_tile, ls] = acc + bdw_ref[:, ls]
    cf = cf_ref[...].reshape(nb * tc, ch)
    mu = jnp.mean(cf, axis=-1, keepdims=True)
    cc = cf - mu
    var = jnp.mean(cc * cc, axis=-1, keepdims=True)
    cn = cc * lax.rsqrt(var + LN_EPS) * lncg_ref[...] + lncb_ref[...]
    act = cn * _sigmoid(cn)
    y_b = _dot_bf16(act, wpw2_ref[...]) + bpw2_ref[...]
    y_a = jnp.dot(yg_ref[...], wouta_ref[...], preferred_element_type=F32)
    pg = pg_ref[...]
    merged = _sigmoid(pg[:, :ch]) * y_a + _sigmoid(pg[:, ch:]) * y_b
    h = x_ref[...] + _dot_bf16(merged, wo_ref[...])
    h_ref[...] = h
    hn = _rms_norm(h, n2g_ref[...])
    hn_ref[...] = hn

    logits = _mm(hn, _parts(wr_ref[...])) + br_ref[...]
    lane = lax.broadcasted_iota(jnp.int32, logits.shape, 1)
    vals, idxs = [], []
    l = logits
    for _ in range(TOP_K):
        m = jnp.max(l, axis=-1, keepdims=True)
        idx = jnp.min(jnp.where(l == m, lane, LANES), axis=-1, keepdims=True)
        vals.append(m)
        idxs.append(idx)
        l = jnp.where(lane == idx, -jnp.inf, l)
    exps = [jnp.exp(vk - vals[0]) for vk in vals]
    denom = exps[0]
    for e in exps[1:]:
        denom = denom + e
    te = jnp.zeros(logits.shape, jnp.int32)
    tg = jnp.zeros(logits.shape, F32)
    for kx in range(TOP_K):
        te = jnp.where(lane == kx, idxs[kx], te)
        tg = jnp.where(lane == kx, exps[kx] / denom, tg)
    tope_ref[...] = te
    gate_ref[...] = tg

    rows = logits.shape[0]
    sel = jnp.zeros(logits.shape, F32)
    for kx in range(TOP_K):
        sel = sel + (lane == idxs[kx]).astype(F32)
    earlier = (lax.broadcasted_iota(jnp.int32, (rows, rows), 0)
               > lax.broadcasted_iota(jnp.int32, (rows, rows), 1)).astype(BF16)
    pos = jnp.dot(earlier, sel.astype(BF16), preferred_element_type=F32) + base_ref[...]
    rk = jnp.zeros(logits.shape, jnp.int32)
    for kx in range(TOP_K):
        r_k = jnp.sum(jnp.where(lane == idxs[kx], pos, 0.0), axis=-1, keepdims=True)
        rk = jnp.where(lane == kx, r_k.astype(jnp.int32), rk)
    rank_ref[...] = rk
    total = base_ref[...] + jnp.sum(sel, axis=0, keepdims=True)
    base_ref[...] = total
    cnt_ref[...] = total


def _mix(x2d, p_glu, p_gate, yg, conv0, cnt0, prm, batch, seq, nb, tc):
    t, d = x2d.shape
    ch = conv0.shape[2]
    hist = CONV_WIDTH - 1
    nc = seq // tc
    rows = nb * tc
    r_tile = min(tc, CONV_ROW_TILE)
    win_rows = r_tile + HIST_ROWS - SUBLANES
    b_tile = min(nb, max(1, CONV_WINDOW_ROWS // win_rows))
    assert tc % r_tile == 0 and nb % b_tile == 0 and ch % CONV_LANE_TILE == 0
    row = lambda a: a.reshape(1, -1)
    const = lambda shape: pl.BlockSpec(shape, lambda b, c: (0,) * len(shape))
    tok = lambda width: pl.BlockSpec((rows, width), lambda b, c: (b * nc + c, 0))
    wr = jnp.zeros((d, LANES), F32).at[:, :N_EXPERTS].set(prm["w_router"])
    br = jnp.full((1, LANES), -1e30, F32).at[0, :N_EXPERTS].set(prm["b_router"])
    return pl.pallas_call(
        functools.partial(_mix_kernel, nb=nb, tc=tc),
        out_shape=(jax.ShapeDtypeStruct((t, d), F32),
                   jax.ShapeDtypeStruct((t, d), F32),
                   jax.ShapeDtypeStruct((t, LANES), jnp.int32),
                   jax.ShapeDtypeStruct((t, LANES), F32),
                   jax.ShapeDtypeStruct((t, LANES), jnp.int32),
                   jax.ShapeDtypeStruct((1, LANES), F32),
                   jax.ShapeDtypeStruct((batch, hist, ch), F32)),
        grid=(batch // nb, nc),
        in_specs=[tok(d), tok(2 * ch), tok(2 * d), tok(d),
                  pl.BlockSpec((nb, hist, ch), lambda b, c: (b, 0, 0)),
                  const((1, 2 * ch)), const((CONV_WIDTH, ch)), const((1, ch)), const((1, ch)),
                  const((1, ch)), const((ch, d)), const((1, d)), const((d, d)), const((d, d)),
                  const((1, d)), const((d, LANES)), const((1, LANES)), const((1, LANES))],
        out_specs=(tok(d), tok(d), tok(LANES), tok(LANES), tok(LANES), const((1, LANES)),
                   pl.BlockSpec((nb, hist, ch), lambda b, c: (b, 0, 0))),
        scratch_shapes=[pltpu.VMEM((nb, HIST_ROWS + tc, ch), F32), pltpu.VMEM((1, LANES), F32),
                        pltpu.VMEM((nb, tc, ch), F32),
                        pltpu.VMEM((b_tile, win_rows, CONV_LANE_TILE), F32)],
        compiler_params=pltpu.CompilerParams(dimension_semantics=("arbitrary", "arbitrary"),
                                             vmem_limit_bytes=VMEM_LIMIT),
        name="mix",
    )(x2d, p_glu, p_gate, yg, conv0,
      row(prm["b_glu"]), prm["w_dw"].reshape(CONV_WIDTH, ch), row(prm["b_dw"]),
      row(prm["ln_conv_g"]), row(prm["ln_conv_b"]), prm["w_pw2"].astype(BF16), row(prm["b_pw2"]),
      prm["w_out_a"].astype(BF16), prm["w_o"].astype(BF16), row(prm["norm2_g"]), wr, br, cnt0)


def _expert_kernel(blk_e_ref, nused_ref, run_par_ref, next_e_ref, tok_cur_ref, tok_nxt_ref, hn_hbm,
                   wup_hbm, bup_ref, wdn_hbm, bdn_ref, out_ref,
                   xbuf0, xbuf1, wup_f32, wdn_f32, wup_bf, wdn_bf, sem, wsem, *, n_blocks):
    i = pl.program_id(0)
    n_used = nused_ref[0]
    rows = xbuf0.shape[0]
    d_ff = wdn_bf.shape[0]
    assert wup_bf.shape[0] == SUBLANES * LANES

    def weight_copies(e, par):
        return (pltpu.make_async_copy(wup_hbm.at[e], wup_f32.at[par], wsem.at[0, par]),
                pltpu.make_async_copy(wdn_hbm.at[e], wdn_f32.at[par], wsem.at[1, par]))

    def start_gather(tok_ref, buf, s):
        for r in range(rows):
            pltpu.make_async_copy(hn_hbm.at[tok_ref[0, 0, r]], buf.at[r], sem.at[s]).start()

    def wait_gather(buf, s):
        pltpu.make_async_copy(hn_hbm.at[pl.ds(0, rows)], buf, sem.at[s]).wait()

    def token_rows(buf):
        return jnp.concatenate([buf[:, j, :] for j in range(SUBLANES)], axis=1)

    @pl.when(i == 0)
    def _():
        for cp in weight_copies(blk_e_ref[0], 0):
            cp.start()
        start_gather(tok_cur_ref, xbuf0, 0)

    @pl.when(i >= n_used)
    def _():
        out_ref[...] = jnp.zeros(out_ref.shape, F32)

    def step(cur, s_cur, nxt, s_nxt):
        @pl.when(i <= n_used)
        def _():
            wait_gather(cur, s_cur)

        @pl.when(i < n_used)
        def _():
            e = blk_e_ref[i]
            prev_e = blk_e_ref[jnp.maximum(i - 1, 0)]
            @pl.when(jnp.logical_or(i == 0, e != prev_e))
            def _():
                par = run_par_ref[i]
                for cp in weight_copies(e, par):
                    cp.wait()
                wup_bf[...] = wup_f32[par].astype(BF16)
                wdn_bf[...] = wdn_f32[par].astype(BF16)
                next_e = next_e_ref[i]
                @pl.when(next_e >= 0)
                def _():
                    for cp in weight_copies(next_e, 1 - par):
                        cp.start()

            start_gather(tok_nxt_ref, nxt, s_nxt)
            hb = (jnp.dot(token_rows(cur).astype(BF16), wup_bf[...], preferred_element_type=F32)
                  + bup_ref[0])
            glu = jnp.minimum(hb[:, :d_ff], SWIGLU_LIMIT)
            lin = jnp.clip(hb[:, d_ff:], -SWIGLU_LIMIT, SWIGLU_LIMIT)
            act = glu * _sigmoid(SWIGLU_ALPHA * glu) * (lin + 1.0)
            out_ref[...] = (jnp.dot(act.astype(BF16), wdn_bf[...], preferred_element_type=F32)
                            + bdn_ref[0])

            @pl.when(i == n_blocks - 1)
            def _():
                wait_gather(nxt, s_nxt)

    @pl.when(i % 2 == 0)
    def _():
        step(xbuf0, 0, xbuf1, 1)

    @pl.when(i % 2 == 1)
    def _():
        step(xbuf1, 1, xbuf0, 0)


def _experts(hn, slot_tok, blk_e, n_used, run_par, next_e, w_up, b_up, w_down, b_down):
    t, d = hn.shape
    n_blocks = blk_e.shape[0]
    rows = EXPERT_BLOCK
    n_exp, _, ff2 = w_up.shape
    d_ff = w_down.shape[1]
    grid_spec = pltpu.PrefetchScalarGridSpec(
        num_scalar_prefetch=4,
        grid=(n_blocks,),
        in_specs=[pl.BlockSpec((1, 1, rows), lambda i, *_: (i, 0, 0), memory_space=pltpu.SMEM),
                  pl.BlockSpec((1, 1, rows), lambda i, *_: (jnp.minimum(i + 1, n_blocks - 1), 0, 0),
                               memory_space=pltpu.SMEM),
                  pl.BlockSpec(memory_space=pl.ANY),
                  pl.BlockSpec(memory_space=pl.ANY),
                  pl.BlockSpec((1, 1, ff2), lambda i, be, *_: (be[i], 0, 0)),
                  pl.BlockSpec(memory_space=pl.ANY),
                  pl.BlockSpec((1, 1, d), lambda i, be, *_: (be[i], 0, 0))],
        out_specs=pl.BlockSpec((rows, d), lambda i, *_: (i, 0)),
        scratch_shapes=[pltpu.VMEM((rows, SUBLANES, LANES), F32),
                        pltpu.VMEM((rows, SUBLANES, LANES), F32),
                        pltpu.VMEM((2, d, ff2), F32),
                        pltpu.VMEM((2, d_ff, d), F32),
                        pltpu.VMEM((d, ff2), BF16),
                        pltpu.VMEM((d_ff, d), BF16),
                        pltpu.SemaphoreType.DMA((2,)),
                        pltpu.SemaphoreType.DMA((2, 2))],
    )
    slot_tok3 = slot_tok.reshape(n_blocks, 1, rows)
    return pl.pallas_call(
        functools.partial(_expert_kernel, n_blocks=n_blocks),
        out_shape=jax.ShapeDtypeStruct((n_blocks * rows, d), F32),
        grid_spec=grid_spec,
        compiler_params=pltpu.CompilerParams(dimension_semantics=("arbitrary",),
                                             vmem_limit_bytes=VMEM_LIMIT),
        name="experts",
    )(blk_e, n_used, run_par, next_e, slot_tok3, slot_tok3, hn.reshape(t, SUBLANES, LANES),
      w_up, b_up.reshape(n_exp, 1, ff2), w_down, b_down.reshape(n_exp, 1, d))


def _combine_kernel(dest_cur_ref, dest_nxt_ref, gate_ref, h_ref, ys_hbm, g_ref, out_ref, buf, sem,
                    *, n):
    i = pl.program_id(0)
    rows = h_ref.shape[0]
    slot = i % 2

    groups = rows // SUBLANES

    def start_gather(dest_ref, s):
        def body(it, carry):
            for j in range(SUBLANES):
                for kx in range(TOP_K):
                    src_row = dest_ref[0, 0, (it * SUBLANES + j) * TOP_K + kx]
                    pltpu.make_async_copy(
                        ys_hbm.at[src_row >> LOG2_SUBLANES, pl.ds(src_row & (SUBLANES - 1), 1), :],
                        buf.at[(s * TOP_K + kx) * groups + it, pl.ds(j, 1), :], sem.at[s]).start()
            return carry

        lax.fori_loop(0, groups, body, 0)

    def choice_rows(s, kx):
        return buf.at[pl.ds((s * TOP_K + kx) * groups, groups)]

    def wait_gather(s):
        for kx in range(TOP_K):
            pltpu.make_async_copy(ys_hbm.at[pl.ds(0, groups)], choice_rows(s, kx), sem.at[s]).wait()

    @pl.when(i == 0)
    def _():
        start_gather(dest_cur_ref, 0)

    @pl.when(i + 1 < n)
    def _():
        start_gather(dest_nxt_ref, 1 - slot)

    wait_gather(slot)
    gates = gate_ref[...]
    y = None
    for kx in range(TOP_K):
        term = choice_rows(slot, kx)[...].reshape(rows, -1) * gates[:, kx:kx + 1]
        y = term if y is None else y + term
    out_ref[...] = _rms_norm(h_ref[...] + y, g_ref[...])


def _combine(dest, gates, h, ys, norm_g):
    t, d = h.shape
    rows = ROW_BLOCK
    nblk = t // rows
    dest3 = dest.reshape(nblk, 1, rows * TOP_K)
    return pl.pallas_call(
        functools.partial(_combine_kernel, n=nblk),
        out_shape=jax.ShapeDtypeStruct((t, d), F32),
        grid=(nblk,),
        in_specs=[pl.BlockSpec((1, 1, rows * TOP_K), lambda i: (i, 0, 0), memory_space=pltpu.SMEM),
                  pl.BlockSpec((1, 1, rows * TOP_K), lambda i: (jnp.minimum(i + 1, nblk - 1), 0, 0),
                               memory_space=pltpu.SMEM),
                  pl.BlockSpec((rows, LANES), lambda i: (i, 0)),
                  pl.BlockSpec((rows, d), lambda i: (i, 0)),
                  pl.BlockSpec(memory_space=pl.ANY),
                  pl.BlockSpec((1, d), lambda i: (0, 0))],
        out_specs=pl.BlockSpec((rows, d), lambda i: (i, 0)),
        scratch_shapes=[pltpu.VMEM((2 * TOP_K * rows // SUBLANES, SUBLANES, d), F32),
                        pltpu.SemaphoreType.DMA((2,))],
        compiler_params=pltpu.CompilerParams(dimension_semantics=("arbitrary",),
                                             vmem_limit_bytes=VMEM_LIMIT),
        name="combine",
    )(dest3, dest3, gates, h, ys.reshape(-1, SUBLANES, d), norm_g.reshape(1, d))


def _routing_tables(top_e, rank, counts):
    t = top_e.shape[0]
    n_assign = t * TOP_K
    padded = (counts + EXPERT_BLOCK - 1) // EXPERT_BLOCK * EXPERT_BLOCK
    pad_end = jnp.cumsum(padded)
    pad_start = pad_end - padded
    experts = jnp.arange(N_EXPERTS, dtype=jnp.int32)
    start_of = jnp.sum(jnp.where(top_e[..., None] == experts, pad_start, 0), axis=-1)
    dest = (start_of + rank).astype(jnp.int32)
    n_blocks = -(-n_assign // EXPERT_BLOCK) + N_EXPERTS
    flat_tok = jnp.arange(n_assign, dtype=jnp.int32) // TOP_K
    slot_tok = jnp.zeros((n_blocks * EXPERT_BLOCK,), jnp.int32).at[dest.reshape(-1)].set(flat_tok)
    blk_start = jnp.arange(n_blocks, dtype=jnp.int32) * EXPERT_BLOCK
    blk_e = jnp.minimum(jnp.sum((blk_start[:, None] >= pad_end[None, :]).astype(jnp.int32), axis=1),
                        N_EXPERTS - 1)
    n_used = (pad_end[-1:] // EXPERT_BLOCK).astype(jnp.int32)
    nonempty = counts > 0
    order = jnp.cumsum(nonempty.astype(jnp.int32)) - 1
    later = jnp.where(nonempty, experts, N_EXPERTS)
    next_nonempty = jnp.concatenate(
        [lax.cummin(later[::-1], axis=0)[::-1][1:], jnp.full((1,), N_EXPERTS, jnp.int32)])
    next_nonempty = jnp.where(next_nonempty >= N_EXPERTS, -1, next_nonempty)
    run_par = (order % 2)[blk_e].astype(jnp.int32)
    next_e = next_nonempty[blk_e].astype(jnp.int32)
    return dest, slot_tok, blk_e, n_used, run_par, next_e


def _mixers(x, wkv0, shift0, conv0, cnt0, prm, w_in_bf16, chunk, seqs_per_step, nb, tc):
    batch, seq, d = x.shape
    shift_w = shift0.shape[1]
    glu_w = 2 * conv0.shape[2]
    x2d = x.reshape(batch * seq, d)
    p_rwkv, p_glu, p_gate = _in_proj(x2d, prm["norm1_g"], w_in_bf16, shift_w, glu_w)
    yg, wkv_new, shift_new = _rwkv(p_rwkv, shift0, wkv0, prm, batch, seq, chunk, seqs_per_step)
    h, hn, top_e, gates, rank, cnt, conv_new = _mix(x2d, p_glu, p_gate, yg, conv0, cnt0, prm,
                                                    batch, seq, nb, tc)
    return h, hn, top_e, gates, rank, cnt, wkv_new, shift_new, conv_new


def kernel(x_prompt, x_sample, state_wkv, state_shift, state_conv, norm1_g, w_in, b_glu, mu_shift, w0, w2, a0, a2, g2, k_k, k_a, r_k, lnx_g, lnx_b, w_out_a, w_dw, b_dw, ln_conv_g, ln_conv_b, w_pw2, b_pw2, w_o, norm2_g, w_router, b_router, w_up, b_up, w_down, b_down, norm_f_g):
    prm = dict(norm1_g=norm1_g, b_glu=b_glu, mu_shift=mu_shift, w0=w0, w2=w2, a0=a0, a2=a2,
               g2=g2, k_k=k_k, k_a=k_a, r_k=r_k, lnx_g=lnx_g, lnx_b=lnx_b, w_out_a=w_out_a,
               w_dw=w_dw, b_dw=b_dw, ln_conv_g=ln_conv_g, ln_conv_b=ln_conv_b, w_pw2=w_pw2,
               b_pw2=b_pw2, w_o=w_o, norm2_g=norm2_g, w_router=w_router, b_router=b_router)
    bp, lp, d = x_prompt.shape
    bs, ls, _ = x_sample.shape
    n_heads = state_wkv.shape[1]
    shift_w = state_shift.shape[1]
    ch = state_conv.shape[2]
    w_in_bf16 = w_in.astype(BF16)

    zeros_wkv = jnp.zeros((bp, n_heads, HEAD_DIM, HEAD_DIM), F32)
    zeros_shift = jnp.zeros((bp, shift_w), F32)
    zeros_conv = jnp.zeros((bp, CONV_WIDTH - 1, ch), F32)
    zero_cnt = jnp.zeros((1, LANES), F32)
    hp, hnp, tep, gp, rkp, cnt_p, wkv_p, shift_p, conv_p = _mixers(
        x_prompt, zeros_wkv, zeros_shift, zeros_conv, zero_cnt, prm, w_in_bf16,
        chunk=min(64, lp), seqs_per_step=PROMPT_SEQS_PER_STEP, nb=1, tc=min(ROW_BLOCK, lp))
    hs, hns, tes, gs, rks, cnt, wkv_s, shift_s, conv_s = _mixers(
        x_sample, state_wkv, state_shift, state_conv, cnt_p, prm, w_in_bf16,
        chunk=ls, seqs_per_step=SAMPLE_SEQS_PER_STEP, nb=ROW_BLOCK // ls, tc=ls)

    hn = jnp.concatenate([hnp, hns], axis=0)
    top_e = jnp.concatenate([tep[:, :TOP_K], tes[:, :TOP_K]], axis=0)
    rank = jnp.concatenate([rkp[:, :TOP_K], rks[:, :TOP_K]], axis=0)
    counts = cnt[0, :N_EXPERTS].astype(jnp.int32)
    dest, slot_tok, blk_e, n_used, run_par, next_e = _routing_tables(top_e, rank, counts)
    ys = _experts(hn, slot_tok, blk_e, n_used, run_par, next_e, w_up, b_up, w_down, b_down)
    tp = bp * lp
    y_p = _combine(dest[:tp], gp, hp, ys, norm_f_g)
    y_s = _combine(dest[tp:], gs, hs, ys, norm_f_g)
    return (y_p.reshape(bp, lp, d), y_s.reshape(bs, ls, d),
            wkv_p, shift_p, conv_p, wkv_s, shift_s, conv_s)
```

```python
import functools

import jax
import jax.numpy as jnp
from jax import lax
from jax.experimental import pallas as pl
from jax.experimental.pallas import tpu as pltpu

F32 = jnp.float32
BF16 = jnp.bfloat16

HEAD_DIM = 64
D_DECAY = 64
D_AAA = 64
D_GATE = 128
CONV_WIDTH = 31
N_EXPERTS = 32
TOP_K = 4
SWIGLU_LIMIT = 7.0
SWIGLU_ALPHA = 1.702
RMS_EPS = 1e-6
LN_EPS = 1e-5
GN_EPS = 64e-5

LANES = 128
SUBLANES = 8
LOG2_SUBLANES = 3
HIST_ROWS = 32
ROW_BLOCK = 256
EXPERT_BLOCK = 256
WEIGHT_DMA_PRIORITY = 1
GROUP_HEADS = 4
GROUP_W = GROUP_HEADS * HEAD_DIM
LOG2_HEAD_DIM = HEAD_DIM.bit_length() - 1
PROMPT_SEQS_PER_STEP = 4
SAMPLE_SEQS_PER_STEP = 4
CONV_ROW_TILE = 64
CONV_LANE_TILE = 256
CONV_WINDOW_ROWS = 128
VMEM_LIMIT = 56 * 1024 * 1024


def _dot_bf16(a, b):
    return jnp.dot(a.astype(BF16), b, preferred_element_type=F32)


def _sigmoid(x):
    return 1.0 / (1.0 + jnp.exp(-x))


def _rms_norm(x, g):
    return x * lax.rsqrt(jnp.mean(x * x, axis=-1, keepdims=True) + RMS_EPS) * g


def _inproj_kernel(x_ref, g_ref, w_ref, o_rwkv, o_glu, o_gate, *, shift_w, glu_w):
    xb = _rms_norm(x_ref[...], g_ref[...]).astype(BF16)
    o_rwkv[...] = jnp.dot(xb, w_ref[:, :shift_w], preferred_element_type=F32)
    o_glu[...] = jnp.dot(xb, w_ref[:, shift_w:shift_w + glu_w], preferred_element_type=F32)
    o_gate[...] = jnp.dot(xb, w_ref[:, shift_w + glu_w:], preferred_element_type=F32)


def _in_proj(x2d, norm_g, w_in_bf16, shift_w, glu_w):
    t, d = x2d.shape
    in_w = w_in_bf16.shape[1]
    gate_w = in_w - shift_w - glu_w
    tm = ROW_BLOCK
    return pl.pallas_call(
        functools.partial(_inproj_kernel, shift_w=shift_w, glu_w=glu_w),
        out_shape=(jax.ShapeDtypeStruct((t, shift_w), F32),
                   jax.ShapeDtypeStruct((t, glu_w), F32),
                   jax.ShapeDtypeStruct((t, gate_w), F32)),
        grid=(t // tm,),
        in_specs=[pl.BlockSpec((tm, d), lambda i: (i, 0)),
                  pl.BlockSpec((1, d), lambda i: (0, 0)),
                  pl.BlockSpec((d, in_w), lambda i: (0, 0), pipeline_mode=pl.Buffered(1))],
        out_specs=(pl.BlockSpec((tm, shift_w), lambda i: (i, 0)),
                   pl.BlockSpec((tm, glu_w), lambda i: (i, 0)),
                   pl.BlockSpec((tm, gate_w), lambda i: (i, 0))),
        compiler_params=pltpu.CompilerParams(dimension_semantics=("arbitrary",),
                                             vmem_limit_bytes=VMEM_LIMIT),
        name="in_proj",
    )(x2d, norm_g.reshape(1, d), w_in_bf16)


NN = (((1,), (0,)), ((), ()))
NT = (((1,), (1,)), ((), ()))
TN = (((0,), (0,)), ((), ()))


def _split2(x):
    hi = x.astype(BF16).astype(F32)
    return hi, x - hi


def _parts(x):
    hi, lo = _split2(x)
    return hi.astype(BF16), lo.astype(BF16)


def _mm(a, b_parts, dims=NN):
    m = a.shape[0]
    a_hi, a_lo = _split2(a)
    a_st = jnp.concatenate([a_hi, a_lo], axis=0).astype(BF16)
    b_hi, b_lo = b_parts
    r = lax.dot_general(a_st, b_hi, dims, preferred_element_type=F32)
    return r[:m] + r[m:] + lax.dot_general(a_st[:m], b_lo, dims, preferred_element_type=F32)


def _block_diag_parts(x, mask):
    hi, lo = _split2(x)
    tile = lambda q: jnp.where(mask, jnp.concatenate([q] * GROUP_HEADS, axis=0), 0.0).astype(BF16)
    return tile(hi), tile(lo)


def _rwkv_kernel(p_ref, shift0_ref, wkv0_ref, mu_ref, w0_ref, w2_ref, a0_ref, a2_ref, g2_ref,
                 kk_ref, ka_ref, rk_ref, lng_ref, lnb_ref,
                 yg_ref, wkv_ref, shift_ref,
                 pbuf, state, *, chunk, n_heads, n_chunks, nb):
    c = pl.program_id(1)
    rw = n_heads * HEAD_DIM
    n_groups = rw // GROUP_W
    cw = GROUP_HEADS * chunk
    log2c = chunk.bit_length() - 1
    n_doublings = log2c - 1

    rows = nb * chunk
    seqs = range(nb)

    @pl.when(c == 0)
    def _():
        pbuf[:, 7:8, :] = shift0_ref[...]
        state[...] = jnp.zeros(state.shape, F32)
        for bi in seqs:
            for hd in range(n_heads):
                g, j = divmod(hd, GROUP_HEADS)
                blk = slice(j * HEAD_DIM, (j + 1) * HEAD_DIM)
                state[bi * n_groups + g, blk, blk] = wkv0_ref[bi, hd]

    p3 = p_ref[...]
    pbuf[:, 8:8 + chunk, :] = p3
    prev = pbuf[:, 7:7 + chunk, :].reshape(rows, -1)
    last = p3[:, chunk - 1:chunk, :]
    pbuf[:, 7:8, :] = last
    shift_ref[...] = last
    p = p3.reshape(rows, -1)

    h = p + (prev - p) * mu_ref[...]
    r = h[:, 0:rw]
    k = h[:, rw:2 * rw]
    v = h[:, 2 * rw:3 * rw]
    o = 3 * rw
    wd = h[:, o:o + D_DECAY]
    ad = h[:, o + D_DECAY:o + D_DECAY + D_AAA]
    gd = h[:, o + D_DECAY + D_AAA:o + D_DECAY + D_AAA + D_GATE]

    def iota(shape, dim):
        return lax.broadcasted_iota(jnp.int32, shape, dim)

    tr, tc_ = iota((rows, rows), 0), iota((rows, rows), 1)
    tri = jnp.logical_and(tr >= tc_, (tr >> log2c) == (tc_ >> log2c)).astype(BF16)
    t4 = iota((chunk, cw), 0)
    s4 = jnp.bitwise_and(iota((chunk, cw), 1), chunk - 1)
    strict4 = t4 > s4
    incl4 = t4 >= s4
    eye4 = (t4 == s4).astype(F32)
    mask_v = (iota((cw, GROUP_W), 0) >> log2c) == (iota((cw, GROUP_W), 1) >> LOG2_HEAD_DIM)
    mask_p = (iota((cw, cw), 0) >> log2c) == (iota((cw, cw), 1) >> log2c)
    mask_s = ((iota((GROUP_W, GROUP_W), 0) >> LOG2_HEAD_DIM)
              == (iota((GROUP_W, GROUP_W), 1) >> LOG2_HEAD_DIM))
    ones_bd = mask_s.astype(BF16)

    def seg_sum(x):
        xs = jnp.concatenate([x[:, g * GROUP_W:(g + 1) * GROUP_W] for g in range(n_groups)], axis=0)
        hi, lo = _split2(xs)
        st = jnp.concatenate([hi, lo], axis=0).astype(BF16)
        rr = jnp.dot(st, ones_bd, preferred_element_type=F32)
        m = xs.shape[0]
        s = rr[:m] + rr[m:]
        return jnp.concatenate([s[g * rows:(g + 1) * rows] for g in range(n_groups)], axis=1)

    z = -(w0_ref[...] + _mm(jnp.tanh(wd), _parts(w2_ref[...])))
    softplus = jnp.maximum(z, 0.0) + jnp.log(1.0 + jnp.exp(-jnp.abs(z)))
    logw = -jnp.exp(-softplus - 0.5)
    l_hi, l_rest = _split2(logw)
    l_mid, l_lo = _split2(l_rest)
    cs = (jnp.dot(tri, l_hi.astype(BF16), preferred_element_type=F32)
          + jnp.dot(tri, l_mid.astype(BF16), preferred_element_type=F32)
          + jnp.dot(tri, l_lo.astype(BF16), preferred_element_type=F32))
    w_incl = jnp.exp(cs)
    w_excl = jnp.exp(cs - logw)
    w_inv = jnp.exp(-cs)
    w_last = [w_incl[(bi + 1) * chunk - 1:(bi + 1) * chunk, :] for bi in seqs]
    w_last_rows = jnp.concatenate([jnp.broadcast_to(w, (chunk, rw)) for w in w_last], axis=0)

    a = _sigmoid(a0_ref[...] + _mm(ad, _parts(a2_ref[...])))
    g = _mm(_sigmoid(gd), _parts(g2_ref[...]))
    kk = k * kk_ref[...]
    kk = kk * lax.rsqrt(jnp.maximum(seg_sum(kk * kk), 1e-24))
    k2 = k * (1.0 + (a - 1.0) * ka_ref[...])
    at_f = kk * w_excl
    rt_f = r * w_incl
    kh_f = k2 * w_inv
    bh_f = kk * a * w_inv
    khw_f = kh_f * w_last_rows
    bhw_f = bh_f * w_last_rows

    groups = range(nb * n_groups)
    rsl = [slice((gi // n_groups) * chunk, (gi // n_groups + 1) * chunk) for gi in groups]
    gsl = [slice((gi % n_groups) * GROUP_W, (gi % n_groups + 1) * GROUP_W) for gi in groups]
    v_g = [v[rsl[gi], gsl[gi]] for gi in groups]
    ar = [jnp.concatenate([at_f[rsl[gi], gsl[gi]], rt_f[rsl[gi], gsl[gi]]], axis=0)
          for gi in groups]
    gk = [_mm(ar[gi], _block_diag_parts(kh_f[rsl[gi], gsl[gi]], mask_v), NT) for gi in groups]
    gb = [_mm(ar[gi], _block_diag_parts(bh_f[rsl[gi], gsl[gi]], mask_v), NT) for gi in groups]
    s_old = [state[gi] for gi in groups]
    x0 = [_mm(ar[gi], _parts(s_old[gi]), NT) for gi in groups]
    a_ak = [jnp.where(strict4, gk[gi][:chunk], 0.0) for gi in groups]
    a_rk = [jnp.where(incl4, gk[gi][chunk:], 0.0) for gi in groups]
    a_ab = [jnp.where(strict4, gb[gi][:chunk], 0.0) for gi in groups]
    a_rb = [jnp.where(incl4, gb[gi][chunk:], 0.0) for gi in groups]
    av = [_mm(jnp.concatenate([a_ak[gi], a_rk[gi]], axis=0), _block_diag_parts(v_g[gi], mask_v))
          for gi in groups]
    rhs = [x0[gi][:chunk] + av[gi][:chunk] for gi in groups]
    nmat = [-a_ab[gi] for gi in groups]
    tinv = [eye4 + nmat[gi] for gi in groups]
    pw = [_mm(nmat[gi], _block_diag_parts(nmat[gi], mask_p)) for gi in groups]
    for i in range(n_doublings):
        pbd = [_block_diag_parts(pw[gi], mask_p) for gi in groups]
        if i == n_doublings - 1:
            tinv = [tinv[gi] + _mm(tinv[gi], pbd[gi]) for gi in groups]
        else:
            both = [_mm(jnp.concatenate([pw[gi], tinv[gi]], axis=0), pbd[gi]) for gi in groups]
            pw = [both[gi][:chunk] for gi in groups]
            tinv = [tinv[gi] + both[gi][chunk:] for gi in groups]
    u = [_mm(tinv[gi], _block_diag_parts(rhs[gi], mask_v)) for gi in groups]
    ys = [x0[gi][chunk:] + av[gi][chunk:] - _mm(a_rb[gi], _block_diag_parts(u[gi], mask_v))
          for gi in groups]
    for gi in groups:
        rs, gs = rsl[gi], gsl[gi]
        l_hi, l_lo = _split2(jnp.concatenate([v_g[gi], -u[gi]], axis=0))
        r_hi, r_lo = _split2(jnp.concatenate([khw_f[rs, gs], bhw_f[rs, gs]], axis=0))
        lhs = jnp.concatenate([l_hi, l_lo, l_hi], axis=0).astype(BF16)
        rhs3 = jnp.concatenate([r_hi, r_hi, r_lo], axis=0).astype(BF16)
        delta = lax.dot_general(lhs, rhs3, TN, preferred_element_type=F32)
        state[gi] = s_old[gi] * w_last[gi // n_groups][:, gs] + jnp.where(mask_s, delta, 0.0)

    y = jnp.concatenate([jnp.concatenate(ys[bi * n_groups:(bi + 1) * n_groups], axis=1)
                         for bi in seqs], axis=0)
    inv_n = 1.0 / HEAD_DIM
    yc = y - seg_sum(y) * inv_n
    var = seg_sum(yc * yc) * inv_n
    yn = yc * lax.rsqrt(var + GN_EPS) * lng_ref[...] + lnb_ref[...]
    bonus = seg_sum(r * k2 * rk_ref[...]) * v
    yg_ref[...] = ((yn + bonus) * g).astype(BF16).reshape(nb, chunk, rw)

    @pl.when(c == n_chunks - 1)
    def _():
        for bi in seqs:
            for hd in range(n_heads):
                gq, j = divmod(hd, GROUP_HEADS)
                blk = slice(j * HEAD_DIM, (j + 1) * HEAD_DIM)
                wkv_ref[bi, hd] = state[bi * n_groups + gq, blk, blk]


def _rwkv(p_rwkv, shift0, wkv0, prm, batch, seq, chunk, nb):
    t, shift_w = p_rwkv.shape
    n_heads = wkv0.shape[1]
    rw = n_heads * HEAD_DIM
    assert chunk >= 8 and chunk & (chunk - 1) == 0 and seq % chunk == 0 and rw % GROUP_W == 0
    assert batch % nb == 0
    nc = seq // chunk
    row = lambda a: a.reshape(1, -1)
    const = lambda shape: pl.BlockSpec(shape, lambda b, c: (0,) * len(shape))
    yg, wkv, shift = pl.pallas_call(
        functools.partial(_rwkv_kernel, chunk=chunk, n_heads=n_heads, n_chunks=nc, nb=nb),
        out_shape=(jax.ShapeDtypeStruct((batch, seq, rw), BF16),
                   jax.ShapeDtypeStruct(wkv0.shape, F32),
                   jax.ShapeDtypeStruct((batch, 1, shift_w), F32)),
        grid=(batch // nb, nc),
        in_specs=[pl.BlockSpec((nb, chunk, shift_w), lambda b, c: (b, c, 0)),
                  pl.BlockSpec((nb, 1, shift_w), lambda b, c: (b, 0, 0)),
                  pl.BlockSpec((nb, n_heads, HEAD_DIM, HEAD_DIM), lambda b, c: (b, 0, 0, 0)),
                  const((1, shift_w)), const((1, rw)), const((D_DECAY, rw)), const((1, rw)),
                  const((D_AAA, rw)), const((D_GATE, rw)), const((1, rw)), const((1, rw)),
                  const((1, rw)), const((1, rw)), const((1, rw))],
        out_specs=(pl.BlockSpec((nb, chunk, rw), lambda b, c: (b, c, 0)),
                   pl.BlockSpec((nb, n_heads, HEAD_DIM, HEAD_DIM), lambda b, c: (b, 0, 0, 0)),
                   pl.BlockSpec((nb, 1, shift_w), lambda b, c: (b, 0, 0))),
        scratch_shapes=[pltpu.VMEM((nb, 8 + chunk, shift_w), F32),
                        pltpu.VMEM((nb * rw // GROUP_W, GROUP_W, GROUP_W), F32)],
        compiler_params=pltpu.CompilerParams(dimension_semantics=("arbitrary", "arbitrary"),
                                             vmem_limit_bytes=VMEM_LIMIT),
        name="rwkv",
    )(p_rwkv.reshape(batch, seq, shift_w), shift0.reshape(batch, 1, shift_w), wkv0,
      row(prm["mu_shift"]), row(prm["w0"]), prm["w2"], row(prm["a0"]), prm["a2"], prm["g2"],
      row(prm["k_k"]), row(prm["k_a"]), row(prm["r_k"]), row(prm["lnx_g"]), row(prm["lnx_b"]))
    return yg.reshape(t, rw), wkv, shift.reshape(batch, shift_w)


def _mix_kernel(x_ref, pglu_ref, pg_ref, yg_ref, conv0_ref, bglu_ref, wdw_ref, bdw_ref,
                lncg_ref, lncb_ref, wpw2_ref, bpw2_ref, wouta_ref, wo_ref, n2g_ref, wr_ref, br_ref,
                cnt0_ref,
                h_ref, hn_ref, tope_ref, gate_ref, rank_ref, cnt_ref, convn_ref,
                full_ref, base_ref, cf_ref, win_ref, *, nb, tc):
    c = pl.program_id(1)
    ch = x_ref.shape[1]
    hist = CONV_WIDTH - 1
    lo = HIST_ROWS - hist

    @pl.when(jnp.logical_and(pl.program_id(0) == 0, c == 0))
    def _():
        base_ref[...] = cnt0_ref[...]

    @pl.when(c == 0)
    def _():
        full_ref[:, lo:HIST_ROWS, :] = conv0_ref[...]

    @pl.when(c > 0)
    def _():
        full_ref[:, 0:HIST_ROWS, :] = full_ref[:, tc:tc + HIST_ROWS, :]

    z = pglu_ref[...] + bglu_ref[...]
    u = z[:, :ch] * _sigmoid(z[:, ch:])
    full_ref[:, HIST_ROWS:HIST_ROWS + tc, :] = u.reshape(nb, tc, ch)
    convn_ref[...] = full_ref[:, tc + lo:tc + HIST_ROWS, :]

    b_tile, win_rows, _ = win_ref.shape
    r_tile = win_rows - (HIST_ROWS - SUBLANES)
    for b0 in range(0, nb, b_tile):
        for r0 in range(0, tc, r_tile):
            for l0 in range(0, ch, CONV_LANE_TILE):
                ls = slice(l0, l0 + CONV_LANE_TILE)
                acc = jnp.zeros((b_tile, r_tile, CONV_LANE_TILE), F32)
                for s in range(SUBLANES):
                    qs = [q for q in range(HIST_ROWS // SUBLANES + 1)
                          if lo <= SUBLANES * q + s <= lo + CONV_WIDTH - 1]
                    start = r0 + SUBLANES * qs[0] + s
                    n_win = r_tile + SUBLANES * (qs[-1] - qs[0])
                    win_ref[:, 0:n_win, :] = full_ref[b0:b0 + b_tile, start:start + n_win, ls]
                    for q in qs:
                        off = SUBLANES * (q - qs[0])
                        w = SUBLANES * q + s - lo
                        acc = acc + win_ref[:, off:off + r_tile, :] * wdw_ref[w:w + 1, ls]
                cf_ref[b0:b0 + b_tile, r0:r0 + r_tile, ls] = acc + bdw_ref[:, ls]
    cf = cf_ref[...].reshape(nb * tc, ch)
    mu = jnp.mean(cf, axis=-1, keepdims=True)
    cc = cf - mu
    var = jnp.mean(cc * cc, axis=-1, keepdims=True)
    cn = cc * lax.rsqrt(var + LN_EPS) * lncg_ref[...] + lncb_ref[...]
    act = cn * _sigmoid(cn)
    y_b = _dot_bf16(act, wpw2_ref[...]) + bpw2_ref[...]
    y_a = jnp.dot(yg_ref[...], wouta_ref[...], preferred_element_type=F32)
    pg = pg_ref[...]
    merged = _sigmoid(pg[:, :ch]) * y_a + _sigmoid(pg[:, ch:]) * y_b
    h = x_ref[...] + _dot_bf16(merged, wo_ref[...])
    h_ref[...] = h
    hn = _rms_norm(h, n2g_ref[...])
    hn_ref[...] = hn

    logits = _mm(hn, _parts(wr_ref[...])) + br_ref[...]
    lane = lax.broadcasted_iota(jnp.int32, logits.shape, 1)
    vals, idxs = [], []
    l = logits
    for _ in range(TOP_K):
        m = jnp.max(l, axis=-1, keepdims=True)
        idx = jnp.min(jnp.where(l == m, lane, LANES), axis=-1, keepdims=True)
        vals.append(m)
        idxs.append(idx)
        l = jnp.where(lane == idx, -jnp.inf, l)
    exps = [jnp.exp(vk - vals[0]) for vk in vals]
    denom = exps[0]
    for e in exps[1:]:
        denom = denom + e
    te = jnp.zeros(logits.shape, jnp.int32)
    tg = jnp.zeros(logits.shape, F32)
    for kx in range(TOP_K):
        te = jnp.where(lane == kx, idxs[kx], te)
        tg = jnp.where(lane == kx, exps[kx] / denom, tg)
    tope_ref[...] = te
    gate_ref[...] = tg

    rows = logits.shape[0]
    sel = jnp.zeros(logits.shape, F32)
    for kx in range(TOP_K):
        sel = sel + (lane == idxs[kx]).astype(F32)
    earlier = (lax.broadcasted_iota(jnp.int32, (rows, rows), 0)
               > lax.broadcasted_iota(jnp.int32, (rows, rows), 1)).astype(BF16)
    pos = jnp.dot(earlier, sel.astype(BF16), preferred_element_type=F32) + base_ref[...]
    rk = jnp.zeros(logits.shape, jnp.int32)
    for kx in range(TOP_K):
        r_k = jnp.sum(jnp.where(lane == idxs[kx], pos, 0.0), axis=-1, keepdims=True)
        rk = jnp.where(lane == kx, r_k.astype(jnp.int32), rk)
    rank_ref[...] = rk
    total = base_ref[...] + jnp.sum(sel, axis=0, keepdims=True)
    base_ref[...] = total
    cnt_ref[...] = total


def _mix(x2d, p_glu, p_gate, yg, conv0, cnt0, prm, batch, seq, nb, tc):
    t, d = x2d.shape
    ch = conv0.shape[2]
    hist = CONV_WIDTH - 1
    nc = seq // tc
    rows = nb * tc
    r_tile = min(tc, CONV_ROW_TILE)
    win_rows = r_tile + HIST_ROWS - SUBLANES
    b_tile = min(nb, max(1, CONV_WINDOW_ROWS // win_rows))
    assert tc % r_tile == 0 and nb % b_tile == 0 and ch % CONV_LANE_TILE == 0
    row = lambda a: a.reshape(1, -1)
    const = lambda shape: pl.BlockSpec(shape, lambda b, c: (0,) * len(shape))
    tok = lambda width: pl.BlockSpec((rows, width), lambda b, c: (b * nc + c, 0))
    wr = jnp.zeros((d, LANES), F32).at[:, :N_EXPERTS].set(prm["w_router"])
    br = jnp.full((1, LANES), -1e30, F32).at[0, :N_EXPERTS].set(prm["b_router"])
    return pl.pallas_call(
        functools.partial(_mix_kernel, nb=nb, tc=tc),
        out_shape=(jax.ShapeDtypeStruct((t, d), F32),
                   jax.ShapeDtypeStruct((t, d), F32),
                   jax.ShapeDtypeStruct((t, LANES), jnp.int32),
                   jax.ShapeDtypeStruct((t, LANES), F32),
                   jax.ShapeDtypeStruct((t, LANES), jnp.int32),
                   jax.ShapeDtypeStruct((1, LANES), F32),
                   jax.ShapeDtypeStruct((batch, hist, ch), F32)),
        grid=(batch // nb, nc),
        in_specs=[tok(d), tok(2 * ch), tok(2 * d), tok(d),
                  pl.BlockSpec((nb, hist, ch), lambda b, c: (b, 0, 0)),
                  const((1, 2 * ch)), const((CONV_WIDTH, ch)), const((1, ch)), const((1, ch)),
                  const((1, ch)), const((ch, d)), const((1, d)), const((d, d)), const((d, d)),
                  const((1, d)), const((d, LANES)), const((1, LANES)), const((1, LANES))],
        out_specs=(tok(d), tok(d), tok(LANES), tok(LANES), tok(LANES), const((1, LANES)),
                   pl.BlockSpec((nb, hist, ch), lambda b, c: (b, 0, 0))),
        scratch_shapes=[pltpu.VMEM((nb, HIST_ROWS + tc, ch), F32), pltpu.VMEM((1, LANES), F32),
                        pltpu.VMEM((nb, tc, ch), F32),
                        pltpu.VMEM((b_tile, win_rows, CONV_LANE_TILE), F32)],
        compiler_params=pltpu.CompilerParams(dimension_semantics=("arbitrary", "arbitrary"),
                                             vmem_limit_bytes=VMEM_LIMIT),
        name="mix",
    )(x2d, p_glu, p_gate, yg, conv0,
      row(prm["b_glu"]), prm["w_dw"].reshape(CONV_WIDTH, ch), row(prm["b_dw"]),
      row(prm["ln_conv_g"]), row(prm["ln_conv_b"]), prm["w_pw2"].astype(BF16), row(prm["b_pw2"]),
      prm["w_out_a"].astype(BF16), prm["w_o"].astype(BF16), row(prm["norm2_g"]), wr, br, cnt0)


def _expert_kernel(blk_e_ref, nused_ref, run_par_ref, next_e_ref, tok_cur_ref, tok_nxt_ref, hn_hbm,
                   wup_hbm, bup_ref, wdn_hbm, bdn_ref, out_ref,
                   xbuf0, xbuf1, wup_f32, wdn_f32, wup_bf, wdn_bf, sem, wsem, *, n_blocks):
    i = pl.program_id(0)
    n_used = nused_ref[0]
    rows = xbuf0.shape[0]
    d_ff = wdn_bf.shape[0]
    assert wup_bf.shape[0] == SUBLANES * LANES

    def weight_copies(e, par):
        return (pltpu.make_async_copy(wup_hbm.at[e], wup_f32.at[par], wsem.at[0, par]),
                pltpu.make_async_copy(wdn_hbm.at[e], wdn_f32.at[par], wsem.at[1, par]))

    def start_gather(tok_ref, buf, s):
        for r in range(rows):
            pltpu.make_async_copy(hn_hbm.at[tok_ref[0, 0, r]], buf.at[r], sem.at[s]).start()

    def wait_gather(buf, s):
        pltpu.make_async_copy(hn_hbm.at[pl.ds(0, rows)], buf, sem.at[s]).wait()

    def token_rows(buf):
        return jnp.concatenate([buf[:, j, :] for j in range(SUBLANES)], axis=1)

    @pl.when(i == 0)
    def _():
        for cp in weight_copies(blk_e_ref[0], 0):
            cp.start(priority=WEIGHT_DMA_PRIORITY)
        start_gather(tok_cur_ref, xbuf0, 0)

    @pl.when(i >= n_used)
    def _():
        out_ref[...] = jnp.zeros(out_ref.shape, F32)

    def step(cur, s_cur, nxt, s_nxt):
        @pl.when(i <= n_used)
        def _():
            wait_gather(cur, s_cur)

        @pl.when(i < n_used)
        def _():
            e = blk_e_ref[i]
            prev_e = blk_e_ref[jnp.maximum(i - 1, 0)]
            @pl.when(jnp.logical_or(i == 0, e != prev_e))
            def _():
                par = run_par_ref[i]
                for cp in weight_copies(e, par):
                    cp.wait()
                wup_bf[...] = wup_f32[par].astype(BF16)
                wdn_bf[...] = wdn_f32[par].astype(BF16)
                next_e = next_e_ref[i]
                @pl.when(next_e >= 0)
                def _():
                    for cp in weight_copies(next_e, 1 - par):
                        cp.start(priority=WEIGHT_DMA_PRIORITY)

            start_gather(tok_nxt_ref, nxt, s_nxt)
            hb = (jnp.dot(token_rows(cur).astype(BF16), wup_bf[...], preferred_element_type=F32)
                  + bup_ref[0])
            glu = jnp.minimum(hb[:, :d_ff], SWIGLU_LIMIT)
            lin = jnp.clip(hb[:, d_ff:], -SWIGLU_LIMIT, SWIGLU_LIMIT)
            act = glu * _sigmoid(SWIGLU_ALPHA * glu) * (lin + 1.0)
            out_ref[...] = (jnp.dot(act.astype(BF16), wdn_bf[...], preferred_element_type=F32)
                            + bdn_ref[0])

            @pl.when(i == n_blocks - 1)
            def _():
                wait_gather(nxt, s_nxt)

    @pl.when(i % 2 == 0)
    def _():
        step(xbuf0, 0, xbuf1, 1)

    @pl.when(i % 2 == 1)
    def _():
        step(xbuf1, 1, xbuf0, 0)


def _experts(hn, slot_tok, blk_e, n_used, run_par, next_e, w_up, b_up, w_down, b_down):
    t, d = hn.shape
    n_blocks = blk_e.shape[0]
    rows = EXPERT_BLOCK
    n_exp, _, ff2 = w_up.shape
    d_ff = w_down.shape[1]
    grid_spec = pltpu.PrefetchScalarGridSpec(
        num_scalar_prefetch=4,
        grid=(n_blocks,),
        in_specs=[pl.BlockSpec((1, 1, rows), lambda i, *_: (i, 0, 0), memory_space=pltpu.SMEM),
                  pl.BlockSpec((1, 1, rows), lambda i, *_: (jnp.minimum(i + 1, n_blocks - 1), 0, 0),
                               memory_space=pltpu.SMEM),
                  pl.BlockSpec(memory_space=pl.ANY),
                  pl.BlockSpec(memory_space=pl.ANY),
                  pl.BlockSpec((1, 1, ff2), lambda i, be, *_: (be[i], 0, 0)),
                  pl.BlockSpec(memory_space=pl.ANY),
                  pl.BlockSpec((1, 1, d), lambda i, be, *_: (be[i], 0, 0))],
        out_specs=pl.BlockSpec((rows, d), lambda i, *_: (i, 0)),
        scratch_shapes=[pltpu.VMEM((rows, SUBLANES, LANES), F32),
                        pltpu.VMEM((rows, SUBLANES, LANES), F32),
                        pltpu.VMEM((2, d, ff2), F32),
                        pltpu.VMEM((2, d_ff, d), F32),
                        pltpu.VMEM((d, ff2), BF16),
                        pltpu.VMEM((d_ff, d), BF16),
                        pltpu.SemaphoreType.DMA((2,)),
                        pltpu.SemaphoreType.DMA((2, 2))],
    )
    slot_tok3 = slot_tok.reshape(n_blocks, 1, rows)
    return pl.pallas_call(
        functools.partial(_expert_kernel, n_blocks=n_blocks),
        out_shape=jax.ShapeDtypeStruct((n_blocks * rows, d), F32),
        grid_spec=grid_spec,
        compiler_params=pltpu.CompilerParams(dimension_semantics=("arbitrary",),
                                             vmem_limit_bytes=VMEM_LIMIT),
        name="experts",
    )(blk_e, n_used, run_par, next_e, slot_tok3, slot_tok3, hn.reshape(t, SUBLANES, LANES),
      w_up, b_up.reshape(n_exp, 1, ff2), w_down, b_down.reshape(n_exp, 1, d))


def _combine_kernel(dest_cur_ref, dest_nxt_ref, gate_ref, h_ref, ys_hbm, g_ref, out_ref, buf, sem,
                    *, n):
    i = pl.program_id(0)
    rows = h_ref.shape[0]
    slot = i % 2

    groups = rows // SUBLANES

    def start_gather(dest_ref, s):
        def body(it, carry):
            for j in range(SUBLANES):
                for kx in range(TOP_K):
                    src_row = dest_ref[0, 0, (it * SUBLANES + j) * TOP_K + kx]
                    pltpu.make_async_copy(
                        ys_hbm.at[src_row >> LOG2_SUBLANES, pl.ds(src_row & (SUBLANES - 1), 1), :],
                        buf.at[(s * TOP_K + kx) * groups + it, pl.ds(j, 1), :], sem.at[s]).start()
            return carry

        lax.fori_loop(0, groups, body, 0)

    def choice_rows(s, kx):
        return buf.at[pl.ds((s * TOP_K + kx) * groups, groups)]

    def wait_gather(s):
        for kx in range(TOP_K):
            pltpu.make_async_copy(ys_hbm.at[pl.ds(0, groups)], choice_rows(s, kx), sem.at[s]).wait()

    @pl.when(i == 0)
    def _():
        start_gather(dest_cur_ref, 0)

    @pl.when(i + 1 < n)
    def _():
        start_gather(dest_nxt_ref, 1 - slot)

    wait_gather(slot)
    gates = gate_ref[...]
    y = None
    for kx in range(TOP_K):
        term = choice_rows(slot, kx)[...].reshape(rows, -1) * gates[:, kx:kx + 1]
        y = term if y is None else y + term
    out_ref[...] = _rms_norm(h_ref[...] + y, g_ref[...])


def _combine(dest, gates, h, ys, norm_g):
    t, d = h.shape
    rows = ROW_BLOCK
    nblk = t // rows
    dest3 = dest.reshape(nblk, 1, rows * TOP_K)
    return pl.pallas_call(
        functools.partial(_combine_kernel, n=nblk),
        out_shape=jax.ShapeDtypeStruct((t, d), F32),
        grid=(nblk,),
        in_specs=[pl.BlockSpec((1, 1, rows * TOP_K), lambda i: (i, 0, 0), memory_space=pltpu.SMEM),
                  pl.BlockSpec((1, 1, rows * TOP_K), lambda i: (jnp.minimum(i + 1, nblk - 1), 0, 0),
                               memory_space=pltpu.SMEM),
                  pl.BlockSpec((rows, LANES), lambda i: (i, 0)),
                  pl.BlockSpec((rows, d), lambda i: (i, 0)),
                  pl.BlockSpec(memory_space=pl.ANY),
                  pl.BlockSpec((1, d), lambda i: (0, 0))],
        out_specs=pl.BlockSpec((rows, d), lambda i: (i, 0)),
        scratch_shapes=[pltpu.VMEM((2 * TOP_K * rows // SUBLANES, SUBLANES, d), F32),
                        pltpu.SemaphoreType.DMA((2,))],
        compiler_params=pltpu.CompilerParams(dimension_semantics=("arbitrary",),
                                             vmem_limit_bytes=VMEM_LIMIT),
        name="combine",
    )(dest3, dest3, gates, h, ys.reshape(-1, SUBLANES, d), norm_g.reshape(1, d))


def _routing_tables(top_e, rank, counts):
    t = top_e.shape[0]
    n_assign = t * TOP_K
    padded = (counts + EXPERT_BLOCK - 1) // EXPERT_BLOCK * EXPERT_BLOCK
    pad_end = jnp.cumsum(padded)
    pad_start = pad_end - padded
    experts = jnp.arange(N_EXPERTS, dtype=jnp.int32)
    start_of = jnp.sum(jnp.where(top_e[..., None] == experts, pad_start, 0), axis=-1)
    dest = (start_of + rank).astype(jnp.int32)
    n_blocks = -(-n_assign // EXPERT_BLOCK) + N_EXPERTS
    flat_tok = jnp.arange(n_assign, dtype=jnp.int32) // TOP_K
    slot_tok = jnp.zeros((n_blocks * EXPERT_BLOCK,), jnp.int32).at[dest.reshape(-1)].set(flat_tok)
    blk_start = jnp.arange(n_blocks, dtype=jnp.int32) * EXPERT_BLOCK
    blk_e = jnp.minimum(jnp.sum((blk_start[:, None] >= pad_end[None, :]).astype(jnp.int32), axis=1),
                        N_EXPERTS - 1)
    n_used = (pad_end[-1:] // EXPERT_BLOCK).astype(jnp.int32)
    nonempty = counts > 0
    order = jnp.cumsum(nonempty.astype(jnp.int32)) - 1
    later = jnp.where(nonempty, experts, N_EXPERTS)
    next_nonempty = jnp.concatenate(
        [lax.cummin(later[::-1], axis=0)[::-1][1:], jnp.full((1,), N_EXPERTS, jnp.int32)])
    next_nonempty = jnp.where(next_nonempty >= N_EXPERTS, -1, next_nonempty)
    run_par = (order % 2)[blk_e].astype(jnp.int32)
    next_e = next_nonempty[blk_e].astype(jnp.int32)
    return dest, slot_tok, blk_e, n_used, run_par, next_e


def _mixers(x, wkv0, shift0, conv0, cnt0, prm, w_in_bf16, chunk, seqs_per_step, nb, tc):
    batch, seq, d = x.shape
    shift_w = shift0.shape[1]
    glu_w = 2 * conv0.shape[2]
    x2d = x.reshape(batch * seq, d)
    p_rwkv, p_glu, p_gate = _in_proj(x2d, prm["norm1_g"], w_in_bf16, shift_w, glu_w)
    yg, wkv_new, shift_new = _rwkv(p_rwkv, shift0, wkv0, prm, batch, seq, chunk, seqs_per_step)
    h, hn, top_e, gates, rank, cnt, conv_new = _mix(x2d, p_glu, p_gate, yg, conv0, cnt0, prm,
                                                    batch, seq, nb, tc)
    return h, hn, top_e, gates, rank, cnt, wkv_new, shift_new, conv_new


def kernel(x_prompt, x_sample, state_wkv, state_shift, state_conv, norm1_g, w_in, b_glu, mu_shift, w0, w2, a0, a2, g2, k_k, k_a, r_k, lnx_g, lnx_b, w_out_a, w_dw, b_dw, ln_conv_g, ln_conv_b, w_pw2, b_pw2, w_o, norm2_g, w_router, b_router, w_up, b_up, w_down, b_down, norm_f_g):
    prm = dict(norm1_g=norm1_g, b_glu=b_glu, mu_shift=mu_shift, w0=w0, w2=w2, a0=a0, a2=a2,
               g2=g2, k_k=k_k, k_a=k_a, r_k=r_k, lnx_g=lnx_g, lnx_b=lnx_b, w_out_a=w_out_a,
               w_dw=w_dw, b_dw=b_dw, ln_conv_g=ln_conv_g, ln_conv_b=ln_conv_b, w_pw2=w_pw2,
               b_pw2=b_pw2, w_o=w_o, norm2_g=norm2_g, w_router=w_router, b_router=b_router)
    bp, lp, d = x_prompt.shape
    bs, ls, _ = x_sample.shape
    n_heads = state_wkv.shape[1]
    shift_w = state_shift.shape[1]
    ch = state_conv.shape[2]
    w_in_bf16 = w_in.astype(BF16)

    zeros_wkv = jnp.zeros((bp, n_heads, HEAD_DIM, HEAD_DIM), F32)
    zeros_shift = jnp.zeros((bp, shift_w), F32)
    zeros_conv = jnp.zeros((bp, CONV_WIDTH - 1, ch), F32)
    zero_cnt = jnp.zeros((1, LANES), F32)
    hp, hnp, tep, gp, rkp, cnt_p, wkv_p, shift_p, conv_p = _mixers(
        x_prompt, zeros_wkv, zeros_shift, zeros_conv, zero_cnt, prm, w_in_bf16,
        chunk=min(64, lp), seqs_per_step=PROMPT_SEQS_PER_STEP, nb=1, tc=min(ROW_BLOCK, lp))
    hs, hns, tes, gs, rks, cnt, wkv_s, shift_s, conv_s = _mixers(
        x_sample, state_wkv, state_shift, state_conv, cnt_p, prm, w_in_bf16,
        chunk=ls, seqs_per_step=SAMPLE_SEQS_PER_STEP, nb=ROW_BLOCK // ls, tc=ls)

    hn = jnp.concatenate([hnp, hns], axis=0)
    top_e = jnp.concatenate([tep[:, :TOP_K], tes[:, :TOP_K]], axis=0)
    rank = jnp.concatenate([rkp[:, :TOP_K], rks[:, :TOP_K]], axis=0)
    counts = cnt[0, :N_EXPERTS].astype(jnp.int32)
    dest, slot_tok, blk_e, n_used, run_par, next_e = _routing_tables(top_e, rank, counts)
    ys = _experts(hn, slot_tok, blk_e, n_used, run_par, next_e, w_up, b_up, w_down, b_down)
    tp = bp * lp
    y_p = _combine(dest[:tp], gp, hp, ys, norm_f_g)
    y_s = _combine(dest[tp:], gs, hs, ys, norm_f_g)
    return (y_p.reshape(bp, lp, d), y_s.reshape(bs, ls, d),
            wkv_p, shift_p, conv_p, wkv_s, shift_s, conv_s)
```

```python
import functools

import jax
import jax.numpy as jnp
from jax import lax
from jax.experimental import pallas as pl
from jax.experimental.pallas import tpu as pltpu

F32 = jnp.float32
BF16 = jnp.bfloat16

HEAD_DIM = 64
D_DECAY = 64
D_AAA = 64
D_GATE = 128
CONV_WIDTH = 31
N_EXPERTS = 32
TOP_K = 4
SWIGLU_LIMIT = 7.0
SWIGLU_ALPHA = 1.702
RMS_EPS = 1e-6
LN_EPS = 1e-5
GN_EPS = 64e-5

LANES = 128
SUBLANES = 8
LOG2_SUBLANES = 3
HIST_ROWS = 32
ROW_BLOCK = 256
EXPERT_BLOCK = 256
WEIGHT_DMA_PRIORITY = 1
GROUP_HEADS = 4
GROUP_W = GROUP_HEADS * HEAD_DIM
LOG2_HEAD_DIM = HEAD_DIM.bit_length() - 1
PROMPT_SEQS_PER_STEP = 4
SAMPLE_SEQS_PER_STEP = 8
CONV_ROW_TILE = 64
CONV_LANE_TILE = 256
CONV_WINDOW_ROWS = 128
VMEM_LIMIT = 56 * 1024 * 1024


def _dot_bf16(a, b):
    return jnp.dot(a.astype(BF16), b, preferred_element_type=F32)


def _sigmoid(x):
    return 1.0 / (1.0 + jnp.exp(-x))


def _rms_norm(x, g):
    return x * lax.rsqrt(jnp.mean(x * x, axis=-1, keepdims=True) + RMS_EPS) * g


def _inproj_kernel(x_ref, g_ref, w_ref, o_rwkv, o_glu, o_gate, *, shift_w, glu_w):
    xb = _rms_norm(x_ref[...], g_ref[...]).astype(BF16)
    o_rwkv[...] = jnp.dot(xb, w_ref[:, :shift_w], preferred_element_type=F32)
    o_glu[...] = jnp.dot(xb, w_ref[:, shift_w:shift_w + glu_w], preferred_element_type=F32)
    o_gate[...] = jnp.dot(xb, w_ref[:, shift_w + glu_w:], preferred_element_type=F32)


def _in_proj(x2d, norm_g, w_in_bf16, shift_w, glu_w):
    t, d = x2d.shape
    in_w = w_in_bf16.shape[1]
    gate_w = in_w - shift_w - glu_w
    tm = ROW_BLOCK
    return pl.pallas_call(
        functools.partial(_inproj_kernel, shift_w=shift_w, glu_w=glu_w),
        out_shape=(jax.ShapeDtypeStruct((t, shift_w), F32),
                   jax.ShapeDtypeStruct((t, glu_w), F32),
                   jax.ShapeDtypeStruct((t, gate_w), F32)),
        grid=(t // tm,),
        in_specs=[pl.BlockSpec((tm, d), lambda i: (i, 0)),
                  pl.BlockSpec((1, d), lambda i: (0, 0)),
                  pl.BlockSpec((d, in_w), lambda i: (0, 0), pipeline_mode=pl.Buffered(1))],
        out_specs=(pl.BlockSpec((tm, shift_w), lambda i: (i, 0)),
                   pl.BlockSpec((tm, glu_w), lambda i: (i, 0)),
                   pl.BlockSpec((tm, gate_w), lambda i: (i, 0))),
        compiler_params=pltpu.CompilerParams(dimension_semantics=("arbitrary",),
                                             vmem_limit_bytes=VMEM_LIMIT),
        name="in_proj",
    )(x2d, norm_g.reshape(1, d), w_in_bf16)


NN = (((1,), (0,)), ((), ()))
NT = (((1,), (1,)), ((), ()))
TN = (((0,), (0,)), ((), ()))


def _split2(x):
    hi = x.astype(BF16).astype(F32)
    return hi, x - hi


def _parts(x):
    hi, lo = _split2(x)
    return hi.astype(BF16), lo.astype(BF16)


def _mm(a, b_parts, dims=NN):
    m = a.shape[0]
    a_hi, a_lo = _split2(a)
    a_st = jnp.concatenate([a_hi, a_lo], axis=0).astype(BF16)
    b_hi, b_lo = b_parts
    r = lax.dot_general(a_st, b_hi, dims, preferred_element_type=F32)
    return r[:m] + r[m:] + lax.dot_general(a_st[:m], b_lo, dims, preferred_element_type=F32)


def _block_diag_parts(x, mask):
    hi, lo = _split2(x)
    tile = lambda q: jnp.where(mask, jnp.concatenate([q] * GROUP_HEADS, axis=0), 0.0).astype(BF16)
    return tile(hi), tile(lo)


def _rwkv_kernel(p_ref, shift0_ref, wkv0_ref, mu_ref, w0_ref, w2_ref, a0_ref, a2_ref, g2_ref,
                 kk_ref, ka_ref, rk_ref, lng_ref, lnb_ref,
                 yg_ref, wkv_ref, shift_ref,
                 pbuf, state, *, chunk, n_heads, n_chunks, nb):
    c = pl.program_id(1)
    rw = n_heads * HEAD_DIM
    n_groups = rw // GROUP_W
    cw = GROUP_HEADS * chunk
    log2c = chunk.bit_length() - 1
    n_doublings = log2c - 1

    rows = nb * chunk
    seqs = range(nb)

    @pl.when(c == 0)
    def _():
        pbuf[:, 7:8, :] = shift0_ref[...]
        state[...] = jnp.zeros(state.shape, F32)
        for bi in seqs:
            for hd in range(n_heads):
                g, j = divmod(hd, GROUP_HEADS)
                blk = slice(j * HEAD_DIM, (j + 1) * HEAD_DIM)
                state[bi * n_groups + g, blk, blk] = wkv0_ref[bi, hd]

    p3 = p_ref[...]
    pbuf[:, 8:8 + chunk, :] = p3
    prev = pbuf[:, 7:7 + chunk, :].reshape(rows, -1)
    last = p3[:, chunk - 1:chunk, :]
    pbuf[:, 7:8, :] = last
    shift_ref[...] = last
    p = p3.reshape(rows, -1)

    h = p + (prev - p) * mu_ref[...]
    r = h[:, 0:rw]
    k = h[:, rw:2 * rw]
    v = h[:, 2 * rw:3 * rw]
    o = 3 * rw
    wd = h[:, o:o + D_DECAY]
    ad = h[:, o + D_DECAY:o + D_DECAY + D_AAA]
    gd = h[:, o + D_DECAY + D_AAA:o + D_DECAY + D_AAA + D_GATE]

    def iota(shape, dim):
        return lax.broadcasted_iota(jnp.int32, shape, dim)

    tr, tc_ = iota((rows, rows), 0), iota((rows, rows), 1)
    tri = jnp.logical_and(tr >= tc_, (tr >> log2c) == (tc_ >> log2c)).astype(BF16)
    t4 = iota((chunk, cw), 0)
    s4 = jnp.bitwise_and(iota((chunk, cw), 1), chunk - 1)
    strict4 = t4 > s4
    incl4 = t4 >= s4
    eye4 = (t4 == s4).astype(F32)
    mask_v = (iota((cw, GROUP_W), 0) >> log2c) == (iota((cw, GROUP_W), 1) >> LOG2_HEAD_DIM)
    mask_p = (iota((cw, cw), 0) >> log2c) == (iota((cw, cw), 1) >> log2c)
    mask_s = ((iota((GROUP_W, GROUP_W), 0) >> LOG2_HEAD_DIM)
              == (iota((GROUP_W, GROUP_W), 1) >> LOG2_HEAD_DIM))
    ones_bd = mask_s.astype(BF16)

    def seg_sum(x):
        xs = jnp.concatenate([x[:, g * GROUP_W:(g + 1) * GROUP_W] for g in range(n_groups)], axis=0)
        hi, lo = _split2(xs)
        st = jnp.concatenate([hi, lo], axis=0).astype(BF16)
        rr = jnp.dot(st, ones_bd, preferred_element_type=F32)
        m = xs.shape[0]
        s = rr[:m] + rr[m:]
        return jnp.concatenate([s[g * rows:(g + 1) * rows] for g in range(n_groups)], axis=1)

    z = -(w0_ref[...] + _mm(jnp.tanh(wd), _parts(w2_ref[...])))
    softplus = jnp.maximum(z, 0.0) + jnp.log(1.0 + jnp.exp(-jnp.abs(z)))
    logw = -jnp.exp(-softplus - 0.5)
    l_hi, l_rest = _split2(logw)
    l_mid, l_lo = _split2(l_rest)
    cs = (jnp.dot(tri, l_hi.astype(BF16), preferred_element_type=F32)
          + jnp.dot(tri, l_mid.astype(BF16), preferred_element_type=F32)
          + jnp.dot(tri, l_lo.astype(BF16), preferred_element_type=F32))
    w_incl = jnp.exp(cs)
    w_excl = jnp.exp(cs - logw)
    w_inv = jnp.exp(-cs)
    w_last = [w_incl[(bi + 1) * chunk - 1:(bi + 1) * chunk, :] for bi in seqs]
    w_last_rows = jnp.concatenate([jnp.broadcast_to(w, (chunk, rw)) for w in w_last], axis=0)

    a = _sigmoid(a0_ref[...] + _mm(ad, _parts(a2_ref[...])))
    g = _mm(_sigmoid(gd), _parts(g2_ref[...]))
    kk = k * kk_ref[...]
    kk = kk * lax.rsqrt(jnp.maximum(seg_sum(kk * kk), 1e-24))
    k2 = k * (1.0 + (a - 1.0) * ka_ref[...])
    at_f = kk * w_excl
    rt_f = r * w_incl
    kh_f = k2 * w_inv
    bh_f = kk * a * w_inv
    khw_f = kh_f * w_last_rows
    bhw_f = bh_f * w_last_rows

    groups = range(nb * n_groups)
    rsl = [slice((gi // n_groups) * chunk, (gi // n_groups + 1) * chunk) for gi in groups]
    gsl = [slice((gi % n_groups) * GROUP_W, (gi % n_groups + 1) * GROUP_W) for gi in groups]
    v_g = [v[rsl[gi], gsl[gi]] for gi in groups]
    ar = [jnp.concatenate([at_f[rsl[gi], gsl[gi]], rt_f[rsl[gi], gsl[gi]]], axis=0)
          for gi in groups]
    gk = [_mm(ar[gi], _block_diag_parts(kh_f[rsl[gi], gsl[gi]], mask_v), NT) for gi in groups]
    gb = [_mm(ar[gi], _block_diag_parts(bh_f[rsl[gi], gsl[gi]], mask_v), NT) for gi in groups]
    s_old = [state[gi] for gi in groups]
    x0 = [_mm(ar[gi], _parts(s_old[gi]), NT) for gi in groups]
    a_ak = [jnp.where(strict4, gk[gi][:chunk], 0.0) for gi in groups]
    a_rk = [jnp.where(incl4, gk[gi][chunk:], 0.0) for gi in groups]
    a_ab = [jnp.where(strict4, gb[gi][:chunk], 0.0) for gi in groups]
    a_rb = [jnp.where(incl4, gb[gi][chunk:], 0.0) for gi in groups]
    av = [_mm(jnp.concatenate([a_ak[gi], a_rk[gi]], axis=0), _block_diag_parts(v_g[gi], mask_v))
          for gi in groups]
    rhs = [x0[gi][:chunk] + av[gi][:chunk] for gi in groups]
    nmat = [-a_ab[gi] for gi in groups]
    tinv = [eye4 + nmat[gi] for gi in groups]
    pw = [_mm(nmat[gi], _block_diag_parts(nmat[gi], mask_p)) for gi in groups]
    for i in range(n_doublings):
        pbd = [_block_diag_parts(pw[gi], mask_p) for gi in groups]
        if i == n_doublings - 1:
            tinv = [tinv[gi] + _mm(tinv[gi], pbd[gi]) for gi in groups]
        else:
            both = [_mm(jnp.concatenate([pw[gi], tinv[gi]], axis=0), pbd[gi]) for gi in groups]
            pw = [both[gi][:chunk] for gi in groups]
            tinv = [tinv[gi] + both[gi][chunk:] for gi in groups]
    u = [_mm(tinv[gi], _block_diag_parts(rhs[gi], mask_v)) for gi in groups]
    ys = [x0[gi][chunk:] + av[gi][chunk:] - _mm(a_rb[gi], _block_diag_parts(u[gi], mask_v))
          for gi in groups]
    for gi in groups:
        rs, gs = rsl[gi], gsl[gi]
        l_hi, l_lo = _split2(jnp.concatenate([v_g[gi], -u[gi]], axis=0))
        r_hi, r_lo = _split2(jnp.concatenate([khw_f[rs, gs], bhw_f[rs, gs]], axis=0))
        lhs = jnp.concatenate([l_hi, l_lo, l_hi], axis=0).astype(BF16)
        rhs3 = jnp.concatenate([r_hi, r_hi, r_lo], axis=0).astype(BF16)
        delta = lax.dot_general(lhs, rhs3, TN, preferred_element_type=F32)
        state[gi] = s_old[gi] * w_last[gi // n_groups][:, gs] + jnp.where(mask_s, delta, 0.0)

    y = jnp.concatenate([jnp.concatenate(ys[bi * n_groups:(bi + 1) * n_groups], axis=1)
                         for bi in seqs], axis=0)
    inv_n = 1.0 / HEAD_DIM
    yc = y - seg_sum(y) * inv_n
    var = seg_sum(yc * yc) * inv_n
    yn = yc * lax.rsqrt(var + GN_EPS) * lng_ref[...] + lnb_ref[...]
    bonus = seg_sum(r * k2 * rk_ref[...]) * v
    yg_ref[...] = ((yn + bonus) * g).astype(BF16).reshape(nb, chunk, rw)

    @pl.when(c == n_chunks - 1)
    def _():
        for bi in seqs:
            for hd in range(n_heads):
                gq, j = divmod(hd, GROUP_HEADS)
                blk = slice(j * HEAD_DIM, (j + 1) * HEAD_DIM)
                wkv_ref[bi, hd] = state[bi * n_groups + gq, blk, blk]


def _rwkv(p_rwkv, shift0, wkv0, prm, batch, seq, chunk, nb):
    t, shift_w = p_rwkv.shape
    n_heads = wkv0.shape[1]
    rw = n_heads * HEAD_DIM
    assert chunk >= 8 and chunk & (chunk - 1) == 0 and seq % chunk == 0 and rw % GROUP_W == 0
    assert batch % nb == 0
    nc = seq // chunk
    row = lambda a: a.reshape(1, -1)
    const = lambda shape: pl.BlockSpec(shape, lambda b, c: (0,) * len(shape))
    yg, wkv, shift = pl.pallas_call(
        functools.partial(_rwkv_kernel, chunk=chunk, n_heads=n_heads, n_chunks=nc, nb=nb),
        out_shape=(jax.ShapeDtypeStruct((batch, seq, rw), BF16),
                   jax.ShapeDtypeStruct(wkv0.shape, F32),
                   jax.ShapeDtypeStruct((batch, 1, shift_w), F32)),
        grid=(batch // nb, nc),
        in_specs=[pl.BlockSpec((nb, chunk, shift_w), lambda b, c: (b, c, 0)),
                  pl.BlockSpec((nb, 1, shift_w), lambda b, c: (b, 0, 0)),
                  pl.BlockSpec((nb, n_heads, HEAD_DIM, HEAD_DIM), lambda b, c: (b, 0, 0, 0)),
                  const((1, shift_w)), const((1, rw)), const((D_DECAY, rw)), const((1, rw)),
                  const((D_AAA, rw)), const((D_GATE, rw)), const((1, rw)), const((1, rw)),
                  const((1, rw)), const((1, rw)), const((1, rw))],
        out_specs=(pl.BlockSpec((nb, chunk, rw), lambda b, c: (b, c, 0)),
                   pl.BlockSpec((nb, n_heads, HEAD_DIM, HEAD_DIM), lambda b, c: (b, 0, 0, 0)),
                   pl.BlockSpec((nb, 1, shift_w), lambda b, c: (b, 0, 0))),
        scratch_shapes=[pltpu.VMEM((nb, 8 + chunk, shift_w), F32),
                        pltpu.VMEM((nb * rw // GROUP_W, GROUP_W, GROUP_W), F32)],
        compiler_params=pltpu.CompilerParams(dimension_semantics=("arbitrary", "arbitrary"),
                                             vmem_limit_bytes=VMEM_LIMIT),
        name="rwkv",
    )(p_rwkv.reshape(batch, seq, shift_w), shift0.reshape(batch, 1, shift_w), wkv0,
      row(prm["mu_shift"]), row(prm["w0"]), prm["w2"], row(prm["a0"]), prm["a2"], prm["g2"],
      row(prm["k_k"]), row(prm["k_a"]), row(prm["r_k"]), row(prm["lnx_g"]), row(prm["lnx_b"]))
    return yg.reshape(t, rw), wkv, shift.reshape(batch, shift_w)


def _mix_kernel(x_ref, pglu_ref, pg_ref, yg_ref, conv0_ref, bglu_ref, wdw_ref, bdw_ref,
                lncg_ref, lncb_ref, wpw2_ref, bpw2_ref, wouta_ref, wo_ref, n2g_ref, wr_ref, br_ref,
                cnt0_ref,
                h_ref, hn_ref, tope_ref, gate_ref, rank_ref, cnt_ref, convn_ref,
                full_ref, base_ref, cf_ref, win_ref, *, nb, tc):
    c = pl.program_id(1)
    ch = x_ref.shape[1]
    hist = CONV_WIDTH - 1
    lo = HIST_ROWS - hist

    @pl.when(jnp.logical_and(pl.program_id(0) == 0, c == 0))
    def _():
        base_ref[...] = cnt0_ref[...]

    @pl.when(c == 0)
    def _():
        full_ref[:, lo:HIST_ROWS, :] = conv0_ref[...]

    @pl.when(c > 0)
    def _():
        full_ref[:, 0:HIST_ROWS, :] = full_ref[:, tc:tc + HIST_ROWS, :]

    z = pglu_ref[...] + bglu_ref[...]
    u = z[:, :ch] * _sigmoid(z[:, ch:])
    full_ref[:, HIST_ROWS:HIST_ROWS + tc, :] = u.reshape(nb, tc, ch)
    convn_ref[...] = full_ref[:, tc + lo:tc + HIST_ROWS, :]

    b_tile, win_rows, _ = win_ref.shape
    r_tile = win_rows - (HIST_ROWS - SUBLANES)
    for b0 in range(0, nb, b_tile):
        for r0 in range(0, tc, r_tile):
            for l0 in range(0, ch, CONV_LANE_TILE):
                ls = slice(l0, l0 + CONV_LANE_TILE)
                acc = jnp.zeros((b_tile, r_tile, CONV_LANE_TILE), F32)
                for s in range(SUBLANES):
                    qs = [q for q in range(HIST_ROWS // SUBLANES + 1)
                          if lo <= SUBLANES * q + s <= lo + CONV_WIDTH - 1]
                    start = r0 + SUBLANES * qs[0] + s
                    n_win = r_tile + SUBLANES * (qs[-1] - qs[0])
                    win_ref[:, 0:n_win, :] = full_ref[b0:b0 + b_tile, start:start + n_win, ls]
                    for q in qs:
                        off = SUBLANES * (q - qs[0])
                        w = SUBLANES * q + s - lo
                        acc = acc + win_ref[:, off:off + r_tile, :] * wdw_ref[w:w + 1, ls]
                cf_ref[b0:b0 + b_tile, r0:r0 + r_tile, ls] = acc + bdw_ref[:, ls]
    cf = cf_ref[...].reshape(nb * tc, ch)
    mu = jnp.mean(cf, axis=-1, keepdims=True)
    cc = cf - mu
    var = jnp.mean(cc * cc, axis=-1, keepdims=True)
    cn = cc * lax.rsqrt(var + LN_EPS) * lncg_ref[...] + lncb_ref[...]
    act = cn * _sigmoid(cn)
    y_b = _dot_bf16(act, wpw2_ref[...]) + bpw2_ref[...]
    y_a = jnp.dot(yg_ref[...], wouta_ref[...], preferred_element_type=F32)
    pg = pg_ref[...]
    merged = _sigmoid(pg[:, :ch]) * y_a + _sigmoid(pg[:, ch:]) * y_b
    h = x_ref[...] + _dot_bf16(merged, wo_ref[...])
    h_ref[...] = h
    hn = _rms_norm(h, n2g_ref[...])
    hn_ref[...] = hn

    logits = _mm(hn, _parts(wr_ref[...])) + br_ref[...]
    lane = lax.broadcasted_iota(jnp.int32, logits.shape, 1)
    vals, idxs = [], []
    l = logits
    for _ in range(TOP_K):
        m = jnp.max(l, axis=-1, keepdims=True)
        idx = jnp.min(jnp.where(l == m, lane, LANES), axis=-1, keepdims=True)
        vals.append(m)
        idxs.append(idx)
        l = jnp.where(lane == idx, -jnp.inf, l)
    exps = [jnp.exp(vk - vals[0]) for vk in vals]
    denom = exps[0]
    for e in exps[1:]:
        denom = denom + e
    te = jnp.zeros(logits.shape, jnp.int32)
    tg = jnp.zeros(logits.shape, F32)
    for kx in range(TOP_K):
        te = jnp.where(lane == kx, idxs[kx], te)
        tg = jnp.where(lane == kx, exps[kx] / denom, tg)
    tope_ref[...] = te
    gate_ref[...] = tg

    rows = logits.shape[0]
    sel = jnp.zeros(logits.shape, F32)
    for kx in range(TOP_K):
        sel = sel + (lane == idxs[kx]).astype(F32)
    earlier = (lax.broadcasted_iota(jnp.int32, (rows, rows), 0)
               > lax.broadcasted_iota(jnp.int32, (rows, rows), 1)).astype(BF16)
    pos = jnp.dot(earlier, sel.astype(BF16), preferred_element_type=F32) + base_ref[...]
    rk = jnp.zeros(logits.shape, jnp.int32)
    for kx in range(TOP_K):
        r_k = jnp.sum(jnp.where(lane == idxs[kx], pos, 0.0), axis=-1, keepdims=True)
        rk = jnp.where(lane == kx, r_k.astype(jnp.int32), rk)
    rank_ref[...] = rk
    total = base_ref[...] + jnp.sum(sel, axis=0, keepdims=True)
    base_ref[...] = total
    cnt_ref[...] = total


def _mix(x2d, p_glu, p_gate, yg, conv0, cnt0, prm, batch, seq, nb, tc):
    t, d = x2d.shape
    ch = conv0.shape[2]
    hist = CONV_WIDTH - 1
    nc = seq // tc
    rows = nb * tc
    r_tile = min(tc, CONV_ROW_TILE)
    win_rows = r_tile + HIST_ROWS - SUBLANES
    b_tile = min(nb, max(1, CONV_WINDOW_ROWS // win_rows))
    assert tc % r_tile == 0 and nb % b_tile == 0 and ch % CONV_LANE_TILE == 0
    row = lambda a: a.reshape(1, -1)
    const = lambda shape: pl.BlockSpec(shape, lambda b, c: (0,) * len(shape))
    tok = lambda width: pl.BlockSpec((rows, width), lambda b, c: (b * nc + c, 0))
    wr = jnp.zeros((d, LANES), F32).at[:, :N_EXPERTS].set(prm["w_router"])
    br = jnp.full((1, LANES), -1e30, F32).at[0, :N_EXPERTS].set(prm["b_router"])
    return pl.pallas_call(
        functools.partial(_mix_kernel, nb=nb, tc=tc),
        out_shape=(jax.ShapeDtypeStruct((t, d), F32),
                   jax.ShapeDtypeStruct((t, d), F32),
                   jax.ShapeDtypeStruct((t, LANES), jnp.int32),
                   jax.ShapeDtypeStruct((t, LANES), F32),
                   jax.ShapeDtypeStruct((t, LANES), jnp.int32),
                   jax.ShapeDtypeStruct((1, LANES), F32),
                   jax.ShapeDtypeStruct((batch, hist, ch), F32)),
        grid=(batch // nb, nc),
        in_specs=[tok(d), tok(2 * ch), tok(2 * d), tok(d),
                  pl.BlockSpec((nb, hist, ch), lambda b, c: (b, 0, 0)),
                  const((1, 2 * ch)), const((CONV_WIDTH, ch)), const((1, ch)), const((1, ch)),
                  const((1, ch)), const((ch, d)), const((1, d)), const((d, d)), const((d, d)),
                  const((1, d)), const((d, LANES)), const((1, LANES)), const((1, LANES))],
        out_specs=(tok(d), tok(d), tok(LANES), tok(LANES), tok(LANES), const((1, LANES)),
                   pl.BlockSpec((nb, hist, ch), lambda b, c: (b, 0, 0))),
        scratch_shapes=[pltpu.VMEM((nb, HIST_ROWS + tc, ch), F32), pltpu.VMEM((1, LANES), F32),
                        pltpu.VMEM((nb, tc, ch), F32),
                        pltpu.VMEM((b_tile, win_rows, CONV_LANE_TILE), F32)],
        compiler_params=pltpu.CompilerParams(dimension_semantics=("arbitrary", "arbitrary"),
                                             vmem_limit_bytes=VMEM_LIMIT),
        name="mix",
    )(x2d, p_glu, p_gate, yg, conv0,
      row(prm["b_glu"]), prm["w_dw"].reshape(CONV_WIDTH, ch), row(prm["b_dw"]),
      row(prm["ln_conv_g"]), row(prm["ln_conv_b"]), prm["w_pw2"].astype(BF16), row(prm["b_pw2"]),
      prm["w_out_a"].astype(BF16), prm["w_o"].astype(BF16), row(prm["norm2_g"]), wr, br, cnt0)


def _expert_kernel(blk_e_ref, nused_ref, run_par_ref, next_e_ref, tok_cur_ref, tok_nxt_ref, hn_hbm,
                   wup_hbm, bup_ref, wdn_hbm, bdn_ref, out_ref,
                   xbuf0, xbuf1, wup_f32, wdn_f32, wup_bf, wdn_bf, sem, wsem, *, n_blocks):
    i = pl.program_id(0)
    n_used = nused_ref[0]
    rows = xbuf0.shape[0]
    d_ff = wdn_bf.shape[0]
    assert wup_bf.shape[0] == SUBLANES * LANES

    def weight_copies(e, par):
        return (pltpu.make_async_copy(wup_hbm.at[e], wup_f32.at[par], wsem.at[0, par]),
                pltpu.make_async_copy(wdn_hbm.at[e], wdn_f32.at[par], wsem.at[1, par]))

    def start_gather(tok_ref, buf, s):
        for r in range(rows):
            pltpu.make_async_copy(hn_hbm.at[tok_ref[0, 0, r]], buf.at[r], sem.at[s]).start()

    def wait_gather(buf, s):
        pltpu.make_async_copy(hn_hbm.at[pl.ds(0, rows)], buf, sem.at[s]).wait()

    def token_rows(buf):
        return jnp.concatenate([buf[:, j, :] for j in range(SUBLANES)], axis=1)

    @pl.when(i == 0)
    def _():
        for cp in weight_copies(blk_e_ref[0], 0):
            cp.start(priority=WEIGHT_DMA_PRIORITY)
        start_gather(tok_cur_ref, xbuf0, 0)

    @pl.when(i >= n_used)
    def _():
        out_ref[...] = jnp.zeros(out_ref.shape, F32)

    def step(cur, s_cur, nxt, s_nxt):
        @pl.when(i <= n_used)
        def _():
            wait_gather(cur, s_cur)

        @pl.when(i < n_used)
        def _():
            e = blk_e_ref[i]
            prev_e = blk_e_ref[jnp.maximum(i - 1, 0)]
            @pl.when(jnp.logical_or(i == 0, e != prev_e))
            def _():
                par = run_par_ref[i]
                for cp in weight_copies(e, par):
                    cp.wait()
                wup_bf[...] = wup_f32[par].astype(BF16)
                wdn_bf[...] = wdn_f32[par].astype(BF16)
                next_e = next_e_ref[i]
                @pl.when(next_e >= 0)
                def _():
                    for cp in weight_copies(next_e, 1 - par):
                        cp.start(priority=WEIGHT_DMA_PRIORITY)

            start_gather(tok_nxt_ref, nxt, s_nxt)
            hb = (jnp.dot(token_rows(cur).astype(BF16), wup_bf[...], preferred_element_type=F32)
                  + bup_ref[0])
            glu = jnp.minimum(hb[:, :d_ff], SWIGLU_LIMIT)
            lin = jnp.clip(hb[:, d_ff:], -SWIGLU_LIMIT, SWIGLU_LIMIT)
            act = glu * _sigmoid(SWIGLU_ALPHA * glu) * (lin + 1.0)
            out_ref[...] = (jnp.dot(act.astype(BF16), wdn_bf[...], preferred_element_type=F32)
                            + bdn_ref[0])

            @pl.when(i == n_blocks - 1)
            def _():
                wait_gather(nxt, s_nxt)

    @pl.when(i % 2 == 0)
    def _():
        step(xbuf0, 0, xbuf1, 1)

    @pl.when(i % 2 == 1)
    def _():
        step(xbuf1, 1, xbuf0, 0)


def _experts(hn, slot_tok, blk_e, n_used, run_par, next_e, w_up, b_up, w_down, b_down):
    t, d = hn.shape
    n_blocks = blk_e.shape[0]
    rows = EXPERT_BLOCK
    n_exp, _, ff2 = w_up.shape
    d_ff = w_down.shape[1]
    grid_spec = pltpu.PrefetchScalarGridSpec(
        num_scalar_prefetch=4,
        grid=(n_blocks,),
        in_specs=[pl.BlockSpec((1, 1, rows), lambda i, *_: (i, 0, 0), memory_space=pltpu.SMEM),
                  pl.BlockSpec((1, 1, rows), lambda i, *_: (jnp.minimum(i + 1, n_blocks - 1), 0, 0),
                               memory_space=pltpu.SMEM),
                  pl.BlockSpec(memory_space=pl.ANY),
                  pl.BlockSpec(memory_space=pl.ANY),
                  pl.BlockSpec((1, 1, ff2), lambda i, be, *_: (be[i], 0, 0)),
                  pl.BlockSpec(memory_space=pl.ANY),
                  pl.BlockSpec((1, 1, d), lambda i, be, *_: (be[i], 0, 0))],
        out_specs=pl.BlockSpec((rows, d), lambda i, *_: (i, 0)),
        scratch_shapes=[pltpu.VMEM((rows, SUBLANES, LANES), F32),
                        pltpu.VMEM((rows, SUBLANES, LANES), F32),
                        pltpu.VMEM((2, d, ff2), F32),
                        pltpu.VMEM((2, d_ff, d), F32),
                        pltpu.VMEM((d, ff2), BF16),
                        pltpu.VMEM((d_ff, d), BF16),
                        pltpu.SemaphoreType.DMA((2,)),
                        pltpu.SemaphoreType.DMA((2, 2))],
    )
    slot_tok3 = slot_tok.reshape(n_blocks, 1, rows)
    return pl.pallas_call(
        functools.partial(_expert_kernel, n_blocks=n_blocks),
        out_shape=jax.ShapeDtypeStruct((n_blocks * rows, d), F32),
        grid_spec=grid_spec,
        compiler_params=pltpu.CompilerParams(dimension_semantics=("arbitrary",),
                                             vmem_limit_bytes=VMEM_LIMIT),
        name="experts",
    )(blk_e, n_used, run_par, next_e, slot_tok3, slot_tok3, hn.reshape(t, SUBLANES, LANES),
      w_up, b_up.reshape(n_exp, 1, ff2), w_down, b_down.reshape(n_exp, 1, d))


def _combine_kernel(dest_cur_ref, dest_nxt_ref, gate_ref, h_ref, ys_hbm, g_ref, out_ref, buf, sem,
                    *, n):
    i = pl.program_id(0)
    rows = h_ref.shape[0]
    slot = i % 2

    groups = rows // SUBLANES

    def start_gather(dest_ref, s):
        def body(it, carry):
            for j in range(SUBLANES):
                for kx in range(TOP_K):
                    src_row = dest_ref[0, 0, (it * SUBLANES + j) * TOP_K + kx]
                    pltpu.make_async_copy(
                        ys_hbm.at[src_row >> LOG2_SUBLANES, pl.ds(src_row & (SUBLANES - 1), 1), :],
                        buf.at[(s * TOP_K + kx) * groups + it, pl.ds(j, 1), :], sem.at[s]).start()
            return carry

        lax.fori_loop(0, groups, body, 0)

    def choice_rows(s, kx):
        return buf.at[pl.ds((s * TOP_K + kx) * groups, groups)]

    def wait_gather(s):
        for kx in range(TOP_K):
            pltpu.make_async_copy(ys_hbm.at[pl.ds(0, groups)], choice_rows(s, kx), sem.at[s]).wait()

    @pl.when(i == 0)
    def _():
        start_gather(dest_cur_ref, 0)

    @pl.when(i + 1 < n)
    def _():
        start_gather(dest_nxt_ref, 1 - slot)

    wait_gather(slot)
    gates = gate_ref[...]
    y = None
    for kx in range(TOP_K):
        term = choice_rows(slot, kx)[...].reshape(rows, -1) * gates[:, kx:kx + 1]
        y = term if y is None else y + term
    out_ref[...] = _rms_norm(h_ref[...] + y, g_ref[...])


def _combine(dest, gates, h, ys, norm_g):
    t, d = h.shape
    rows = ROW_BLOCK
    nblk = t // rows
    dest3 = dest.reshape(nblk, 1, rows * TOP_K)
    return pl.pallas_call(
        functools.partial(_combine_kernel, n=nblk),
        out_shape=jax.ShapeDtypeStruct((t, d), F32),
        grid=(nblk,),
        in_specs=[pl.BlockSpec((1, 1, rows * TOP_K), lambda i: (i, 0, 0), memory_space=pltpu.SMEM),
                  pl.BlockSpec((1, 1, rows * TOP_K), lambda i: (jnp.minimum(i + 1, nblk - 1), 0, 0),
                               memory_space=pltpu.SMEM),
                  pl.BlockSpec((rows, LANES), lambda i: (i, 0)),
                  pl.BlockSpec((rows, d), lambda i: (i, 0)),
                  pl.BlockSpec(memory_space=pl.ANY),
                  pl.BlockSpec((1, d), lambda i: (0, 0))],
        out_specs=pl.BlockSpec((rows, d), lambda i: (i, 0)),
        scratch_shapes=[pltpu.VMEM((2 * TOP_K * rows // SUBLANES, SUBLANES, d), F32),
                        pltpu.SemaphoreType.DMA((2,))],
        compiler_params=pltpu.CompilerParams(dimension_semantics=("arbitrary",),
                                             vmem_limit_bytes=VMEM_LIMIT),
        name="combine",
    )(dest3, dest3, gates, h, ys.reshape(-1, SUBLANES, d), norm_g.reshape(1, d))


def _routing_tables(top_e, rank, counts):
    t = top_e.shape[0]
    n_assign = t * TOP_K
    padded = (counts + EXPERT_BLOCK - 1) // EXPERT_BLOCK * EXPERT_BLOCK
    pad_end = jnp.cumsum(padded)
    pad_start = pad_end - padded
    experts = jnp.arange(N_EXPERTS, dtype=jnp.int32)
    start_of = jnp.sum(jnp.where(top_e[..., None] == experts, pad_start, 0), axis=-1)
    dest = (start_of + rank).astype(jnp.int32)
    n_blocks = -(-n_assign // EXPERT_BLOCK) + N_EXPERTS
    flat_tok = jnp.arange(n_assign, dtype=jnp.int32) // TOP_K
    slot_tok = jnp.zeros((n_blocks * EXPERT_BLOCK,), jnp.int32).at[dest.reshape(-1)].set(
        flat_tok, unique_indices=True, mode="promise_in_bounds")
    blk_start = jnp.arange(n_blocks, dtype=jnp.int32) * EXPERT_BLOCK
    blk_e = jnp.minimum(jnp.sum((blk_start[:, None] >= pad_end[None, :]).astype(jnp.int32), axis=1),
                        N_EXPERTS - 1)
    n_used = (pad_end[-1:] // EXPERT_BLOCK).astype(jnp.int32)
    nonempty = counts > 0
    order = jnp.cumsum(nonempty.astype(jnp.int32)) - 1
    later = jnp.where(nonempty, experts, N_EXPERTS)
    next_nonempty = jnp.concatenate(
        [lax.cummin(later[::-1], axis=0)[::-1][1:], jnp.full((1,), N_EXPERTS, jnp.int32)])
    next_nonempty = jnp.where(next_nonempty >= N_EXPERTS, -1, next_nonempty)
    run_par = (order % 2)[blk_e].astype(jnp.int32)
    next_e = next_nonempty[blk_e].astype(jnp.int32)
    return dest, slot_tok, blk_e, n_used, run_par, next_e


def _mixers(x, wkv0, shift0, conv0, cnt0, prm, w_in_bf16, chunk, seqs_per_step, nb, tc):
    batch, seq, d = x.shape
    shift_w = shift0.shape[1]
    glu_w = 2 * conv0.shape[2]
    x2d = x.reshape(batch * seq, d)
    p_rwkv, p_glu, p_gate = _in_proj(x2d, prm["norm1_g"], w_in_bf16, shift_w, glu_w)
    yg, wkv_new, shift_new = _rwkv(p_rwkv, shift0, wkv0, prm, batch, seq, chunk, seqs_per_step)
    h, hn, top_e, gates, rank, cnt, conv_new = _mix(x2d, p_glu, p_gate, yg, conv0, cnt0, prm,
                                                    batch, seq, nb, tc)
    return h, hn, top_e, gates, rank, cnt, wkv_new, shift_new, conv_new


def kernel(x_prompt, x_sample, state_wkv, state_shift, state_conv, norm1_g, w_in, b_glu, mu_shift, w0, w2, a0, a2, g2, k_k, k_a, r_k, lnx_g, lnx_b, w_out_a, w_dw, b_dw, ln_conv_g, ln_conv_b, w_pw2, b_pw2, w_o, norm2_g, w_router, b_router, w_up, b_up, w_down, b_down, norm_f_g):
    prm = dict(norm1_g=norm1_g, b_glu=b_glu, mu_shift=mu_shift, w0=w0, w2=w2, a0=a0, a2=a2,
               g2=g2, k_k=k_k, k_a=k_a, r_k=r_k, lnx_g=lnx_g, lnx_b=lnx_b, w_out_a=w_out_a,
               w_dw=w_dw, b_dw=b_dw, ln_conv_g=ln_conv_g, ln_conv_b=ln_conv_b, w_pw2=w_pw2,
               b_pw2=b_pw2, w_o=w_o, norm2_g=norm2_g, w_router=w_router, b_router=b_router)
    bp, lp, d = x_prompt.shape
    bs, ls, _ = x_sample.shape
    n_heads = state_wkv.shape[1]
    shift_w = state_shift.shape[1]
    ch = state_conv.shape[2]
    w_in_bf16 = w_in.astype(BF16)

    zeros_wkv = jnp.zeros((bp, n_heads, HEAD_DIM, HEAD_DIM), F32)
    zeros_shift = jnp.zeros((bp, shift_w), F32)
    zeros_conv = jnp.zeros((bp, CONV_WIDTH - 1, ch), F32)
    zero_cnt = jnp.zeros((1, LANES), F32)
    hp, hnp, tep, gp, rkp, cnt_p, wkv_p, shift_p, conv_p = _mixers(
        x_prompt, zeros_wkv, zeros_shift, zeros_conv, zero_cnt, prm, w_in_bf16,
        chunk=min(64, lp), seqs_per_step=PROMPT_SEQS_PER_STEP, nb=1, tc=min(ROW_BLOCK, lp))
    hs, hns, tes, gs, rks, cnt, wkv_s, shift_s, conv_s = _mixers(
        x_sample, state_wkv, state_shift, state_conv, cnt_p, prm, w_in_bf16,
        chunk=ls, seqs_per_step=SAMPLE_SEQS_PER_STEP, nb=ROW_BLOCK // ls, tc=ls)

    hn = jnp.concatenate([hnp, hns], axis=0)
    top_e = jnp.concatenate([tep[:, :TOP_K], tes[:, :TOP_K]], axis=0)
    rank = jnp.concatenate([rkp[:, :TOP_K], rks[:, :TOP_K]], axis=0)
    counts = cnt[0, :N_EXPERTS].astype(jnp.int32)
    dest, slot_tok, blk_e, n_used, run_par, next_e = _routing_tables(top_e, rank, counts)
    ys = _experts(hn, slot_tok, blk_e, n_used, run_par, next_e, w_up, b_up, w_down, b_down)
    tp = bp * lp
    y_p = _combine(dest[:tp], gp, hp, ys, norm_f_g)
    y_s = _combine(dest[tp:], gs, hs, ys, norm_f_g)
    return (y_p.reshape(bp, lp, d), y_s.reshape(bs, ls, d),
            wkv_p, shift_p, conv_p, wkv_s, shift_s, conv_s)
```

```python
import functools

import jax
import jax.numpy as jnp
from jax import lax
from jax.experimental import pallas as pl
from jax.experimental.pallas import tpu as pltpu

F32 = jnp.float32
BF16 = jnp.bfloat16

HEAD_DIM = 64
D_DECAY = 64
D_AAA = 64
D_GATE = 128
CONV_WIDTH = 31
N_EXPERTS = 32
TOP_K = 4
SWIGLU_LIMIT = 7.0
SWIGLU_ALPHA = 1.702
RMS_EPS = 1e-6
LN_EPS = 1e-5
GN_EPS = 64e-5

LANES = 128
SUBLANES = 8
LOG2_SUBLANES = 3
HIST_ROWS = 32
ROW_BLOCK = 256
EXPERT_BLOCK = 256
GATHER_ISSUE_STRIDE = 101
WEIGHT_DMA_PRIORITY = 1
GROUP_HEADS = 4
GROUP_W = GROUP_HEADS * HEAD_DIM
LOG2_HEAD_DIM = HEAD_DIM.bit_length() - 1
PROMPT_SEQS_PER_STEP = 4
SAMPLE_SEQS_PER_STEP = 8
CONV_ROW_TILE = 64
CONV_LANE_TILE = 256
CONV_WINDOW_ROWS = 128
VMEM_LIMIT = 56 * 1024 * 1024


def _dot_bf16(a, b):
    return jnp.dot(a.astype(BF16), b, preferred_element_type=F32)


def _sigmoid(x):
    return 1.0 / (1.0 + jnp.exp(-x))


def _rms_norm(x, g):
    return x * lax.rsqrt(jnp.mean(x * x, axis=-1, keepdims=True) + RMS_EPS) * g


def _inproj_kernel(x_ref, g_ref, w_ref, o_rwkv, o_glu, o_gate, *, shift_w, glu_w):
    xb = _rms_norm(x_ref[...], g_ref[...]).astype(BF16)
    o_rwkv[...] = jnp.dot(xb, w_ref[:, :shift_w], preferred_element_type=F32)
    o_glu[...] = jnp.dot(xb, w_ref[:, shift_w:shift_w + glu_w], preferred_element_type=F32)
    o_gate[...] = jnp.dot(xb, w_ref[:, shift_w + glu_w:], preferred_element_type=F32)


def _in_proj(x2d, norm_g, w_in_bf16, shift_w, glu_w):
    t, d = x2d.shape
    in_w = w_in_bf16.shape[1]
    gate_w = in_w - shift_w - glu_w
    tm = ROW_BLOCK
    return pl.pallas_call(
        functools.partial(_inproj_kernel, shift_w=shift_w, glu_w=glu_w),
        out_shape=(jax.ShapeDtypeStruct((t, shift_w), F32),
                   jax.ShapeDtypeStruct((t, glu_w), F32),
                   jax.ShapeDtypeStruct((t, gate_w), F32)),
        grid=(t // tm,),
        in_specs=[pl.BlockSpec((tm, d), lambda i: (i, 0)),
                  pl.BlockSpec((1, d), lambda i: (0, 0)),
                  pl.BlockSpec((d, in_w), lambda i: (0, 0), pipeline_mode=pl.Buffered(1))],
        out_specs=(pl.BlockSpec((tm, shift_w), lambda i: (i, 0)),
                   pl.BlockSpec((tm, glu_w), lambda i: (i, 0)),
                   pl.BlockSpec((tm, gate_w), lambda i: (i, 0))),
        compiler_params=pltpu.CompilerParams(dimension_semantics=("arbitrary",),
                                             vmem_limit_bytes=VMEM_LIMIT),
        name="in_proj",
    )(x2d, norm_g.reshape(1, d), w_in_bf16)


NN = (((1,), (0,)), ((), ()))
NT = (((1,), (1,)), ((), ()))
TN = (((0,), (0,)), ((), ()))


def _split2(x):
    hi = x.astype(BF16).astype(F32)
    return hi, x - hi


def _parts(x):
    hi, lo = _split2(x)
    return hi.astype(BF16), lo.astype(BF16)


def _mm(a, b_parts, dims=NN):
    m = a.shape[0]
    a_hi, a_lo = _split2(a)
    a_st = jnp.concatenate([a_hi, a_lo], axis=0).astype(BF16)
    b_hi, b_lo = b_parts
    r = lax.dot_general(a_st, b_hi, dims, preferred_element_type=F32)
    return r[:m] + r[m:] + lax.dot_general(a_st[:m], b_lo, dims, preferred_element_type=F32)


def _block_diag_parts(x, mask):
    hi, lo = _split2(x)
    tile = lambda q: jnp.where(mask, jnp.concatenate([q] * GROUP_HEADS, axis=0), 0.0).astype(BF16)
    return tile(hi), tile(lo)


def _rwkv_kernel(p_ref, shift0_ref, wkv0_ref, mu_ref, w0_ref, w2_ref, a0_ref, a2_ref, g2_ref,
                 kk_ref, ka_ref, rk_ref, lng_ref, lnb_ref,
                 yg_ref, wkv_ref, shift_ref,
                 pbuf, state, *, chunk, n_heads, n_chunks, nb):
    c = pl.program_id(1)
    rw = n_heads * HEAD_DIM
    n_groups = rw // GROUP_W
    cw = GROUP_HEADS * chunk
    log2c = chunk.bit_length() - 1
    n_doublings = log2c - 1

    rows = nb * chunk
    seqs = range(nb)

    @pl.when(c == 0)
    def _():
        pbuf[:, 7:8, :] = shift0_ref[...]
        state[...] = jnp.zeros(state.shape, F32)
        for bi in seqs:
            for hd in range(n_heads):
                g, j = divmod(hd, GROUP_HEADS)
                blk = slice(j * HEAD_DIM, (j + 1) * HEAD_DIM)
                state[bi * n_groups + g, blk, blk] = wkv0_ref[bi, hd]

    p3 = p_ref[...]
    pbuf[:, 8:8 + chunk, :] = p3
    prev = pbuf[:, 7:7 + chunk, :].reshape(rows, -1)
    last = p3[:, chunk - 1:chunk, :]
    pbuf[:, 7:8, :] = last
    shift_ref[...] = last
    p = p3.reshape(rows, -1)

    h = p + (prev - p) * mu_ref[...]
    r = h[:, 0:rw]
    k = h[:, rw:2 * rw]
    v = h[:, 2 * rw:3 * rw]
    o = 3 * rw
    wd = h[:, o:o + D_DECAY]
    ad = h[:, o + D_DECAY:o + D_DECAY + D_AAA]
    gd = h[:, o + D_DECAY + D_AAA:o + D_DECAY + D_AAA + D_GATE]

    def iota(shape, dim):
        return lax.broadcasted_iota(jnp.int32, shape, dim)

    tr, tc_ = iota((rows, rows), 0), iota((rows, rows), 1)
    tri = jnp.logical_and(tr >= tc_, (tr >> log2c) == (tc_ >> log2c)).astype(BF16)
    t4 = iota((chunk, cw), 0)
    s4 = jnp.bitwise_and(iota((chunk, cw), 1), chunk - 1)
    strict4 = t4 > s4
    incl4 = t4 >= s4
    eye4 = (t4 == s4).astype(F32)
    mask_v = (iota((cw, GROUP_W), 0) >> log2c) == (iota((cw, GROUP_W), 1) >> LOG2_HEAD_DIM)
    mask_p = (iota((cw, cw), 0) >> log2c) == (iota((cw, cw), 1) >> log2c)
    mask_s = ((iota((GROUP_W, GROUP_W), 0) >> LOG2_HEAD_DIM)
              == (iota((GROUP_W, GROUP_W), 1) >> LOG2_HEAD_DIM))
    ones_bd = mask_s.astype(BF16)

    def seg_sum(x):
        xs = jnp.concatenate([x[:, g * GROUP_W:(g + 1) * GROUP_W] for g in range(n_groups)], axis=0)
        hi, lo = _split2(xs)
        st = jnp.concatenate([hi, lo], axis=0).astype(BF16)
        rr = jnp.dot(st, ones_bd, preferred_element_type=F32)
        m = xs.shape[0]
        s = rr[:m] + rr[m:]
        return jnp.concatenate([s[g * rows:(g + 1) * rows] for g in range(n_groups)], axis=1)

    z = -(w0_ref[...] + _mm(jnp.tanh(wd), _parts(w2_ref[...])))
    softplus = jnp.maximum(z, 0.0) + jnp.log(1.0 + jnp.exp(-jnp.abs(z)))
    logw = -jnp.exp(-softplus - 0.5)
    l_hi, l_rest = _split2(logw)
    l_mid, l_lo = _split2(l_rest)
    cs = (jnp.dot(tri, l_hi.astype(BF16), preferred_element_type=F32)
          + jnp.dot(tri, l_mid.astype(BF16), preferred_element_type=F32)
          + jnp.dot(tri, l_lo.astype(BF16), preferred_element_type=F32))
    w_incl = jnp.exp(cs)
    w_excl = jnp.exp(cs - logw)
    w_inv = jnp.exp(-cs)
    w_last = [w_incl[(bi + 1) * chunk - 1:(bi + 1) * chunk, :] for bi in seqs]
    w_last_rows = jnp.concatenate([jnp.broadcast_to(w, (chunk, rw)) for w in w_last], axis=0)

    a = _sigmoid(a0_ref[...] + _mm(ad, _parts(a2_ref[...])))
    g = _mm(_sigmoid(gd), _parts(g2_ref[...]))
    kk = k * kk_ref[...]
    kk = kk * lax.rsqrt(jnp.maximum(seg_sum(kk * kk), 1e-24))
    k2 = k * (1.0 + (a - 1.0) * ka_ref[...])
    at_f = kk * w_excl
    rt_f = r * w_incl
    kh_f = k2 * w_inv
    bh_f = kk * a * w_inv
    khw_f = kh_f * w_last_rows
    bhw_f = bh_f * w_last_rows

    groups = range(nb * n_groups)
    rsl = [slice((gi // n_groups) * chunk, (gi // n_groups + 1) * chunk) for gi in groups]
    gsl = [slice((gi % n_groups) * GROUP_W, (gi % n_groups + 1) * GROUP_W) for gi in groups]
    v_g = [v[rsl[gi], gsl[gi]] for gi in groups]
    ar = [jnp.concatenate([at_f[rsl[gi], gsl[gi]], rt_f[rsl[gi], gsl[gi]]], axis=0)
          for gi in groups]
    gk = [_mm(ar[gi], _block_diag_parts(kh_f[rsl[gi], gsl[gi]], mask_v), NT) for gi in groups]
    gb = [_mm(ar[gi], _block_diag_parts(bh_f[rsl[gi], gsl[gi]], mask_v), NT) for gi in groups]
    s_old = [state[gi] for gi in groups]
    x0 = [_mm(ar[gi], _parts(s_old[gi]), NT) for gi in groups]
    a_ak = [jnp.where(strict4, gk[gi][:chunk], 0.0) for gi in groups]
    a_rk = [jnp.where(incl4, gk[gi][chunk:], 0.0) for gi in groups]
    a_ab = [jnp.where(strict4, gb[gi][:chunk], 0.0) for gi in groups]
    a_rb = [jnp.where(incl4, gb[gi][chunk:], 0.0) for gi in groups]
    av = [_mm(jnp.concatenate([a_ak[gi], a_rk[gi]], axis=0), _block_diag_parts(v_g[gi], mask_v))
          for gi in groups]
    rhs = [x0[gi][:chunk] + av[gi][:chunk] for gi in groups]
    nmat = [-a_ab[gi] for gi in groups]
    tinv = [eye4 + nmat[gi] for gi in groups]
    pw = [_mm(nmat[gi], _block_diag_parts(nmat[gi], mask_p)) for gi in groups]
    for i in range(n_doublings):
        pbd = [_block_diag_parts(pw[gi], mask_p) for gi in groups]
        if i == n_doublings - 1:
            tinv = [tinv[gi] + _mm(tinv[gi], pbd[gi]) for gi in groups]
        else:
            both = [_mm(jnp.concatenate([pw[gi], tinv[gi]], axis=0), pbd[gi]) for gi in groups]
            pw = [both[gi][:chunk] for gi in groups]
            tinv = [tinv[gi] + both[gi][chunk:] for gi in groups]
    u = [_mm(tinv[gi], _block_diag_parts(rhs[gi], mask_v)) for gi in groups]
    ys = [x0[gi][chunk:] + av[gi][chunk:] - _mm(a_rb[gi], _block_diag_parts(u[gi], mask_v))
          for gi in groups]
    for gi in groups:
        rs, gs = rsl[gi], gsl[gi]
        l_hi, l_lo = _split2(jnp.concatenate([v_g[gi], -u[gi]], axis=0))
        r_hi, r_lo = _split2(jnp.concatenate([khw_f[rs, gs], bhw_f[rs, gs]], axis=0))
        lhs = jnp.concatenate([l_hi, l_lo, l_hi], axis=0).astype(BF16)
        rhs3 = jnp.concatenate([r_hi, r_hi, r_lo], axis=0).astype(BF16)
        delta = lax.dot_general(lhs, rhs3, TN, preferred_element_type=F32)
        state[gi] = s_old[gi] * w_last[gi // n_groups][:, gs] + jnp.where(mask_s, delta, 0.0)

    y = jnp.concatenate([jnp.concatenate(ys[bi * n_groups:(bi + 1) * n_groups], axis=1)
                         for bi in seqs], axis=0)
    inv_n = 1.0 / HEAD_DIM
    yc = y - seg_sum(y) * inv_n
    var = seg_sum(yc * yc) * inv_n
    yn = yc * lax.rsqrt(var + GN_EPS) * lng_ref[...] + lnb_ref[...]
    bonus = seg_sum(r * k2 * rk_ref[...]) * v
    yg_ref[...] = ((yn + bonus) * g).astype(BF16).reshape(nb, chunk, rw)

    @pl.when(c == n_chunks - 1)
    def _():
        for bi in seqs:
            for hd in range(n_heads):
                gq, j = divmod(hd, GROUP_HEADS)
                blk = slice(j * HEAD_DIM, (j + 1) * HEAD_DIM)
                wkv_ref[bi, hd] = state[bi * n_groups + gq, blk, blk]


def _rwkv(p_rwkv, shift0, wkv0, prm, batch, seq, chunk, nb):
    t, shift_w = p_rwkv.shape
    n_heads = wkv0.shape[1]
    rw = n_heads * HEAD_DIM
    assert chunk >= 8 and chunk & (chunk - 1) == 0 and seq % chunk == 0 and rw % GROUP_W == 0
    assert batch % nb == 0
    nc = seq // chunk
    row = lambda a: a.reshape(1, -1)
    const = lambda shape: pl.BlockSpec(shape, lambda b, c: (0,) * len(shape))
    yg, wkv, shift = pl.pallas_call(
        functools.partial(_rwkv_kernel, chunk=chunk, n_heads=n_heads, n_chunks=nc, nb=nb),
        out_shape=(jax.ShapeDtypeStruct((batch, seq, rw), BF16),
                   jax.ShapeDtypeStruct(wkv0.shape, F32),
                   jax.ShapeDtypeStruct((batch, 1, shift_w), F32)),
        grid=(batch // nb, nc),
        in_specs=[pl.BlockSpec((nb, chunk, shift_w), lambda b, c: (b, c, 0)),
                  pl.BlockSpec((nb, 1, shift_w), lambda b, c: (b, 0, 0)),
                  pl.BlockSpec((nb, n_heads, HEAD_DIM, HEAD_DIM), lambda b, c: (b, 0, 0, 0)),
                  const((1, shift_w)), const((1, rw)), const((D_DECAY, rw)), const((1, rw)),
                  const((D_AAA, rw)), const((D_GATE, rw)), const((1, rw)), const((1, rw)),
                  const((1, rw)), const((1, rw)), const((1, rw))],
        out_specs=(pl.BlockSpec((nb, chunk, rw), lambda b, c: (b, c, 0)),
                   pl.BlockSpec((nb, n_heads, HEAD_DIM, HEAD_DIM), lambda b, c: (b, 0, 0, 0)),
                   pl.BlockSpec((nb, 1, shift_w), lambda b, c: (b, 0, 0))),
        scratch_shapes=[pltpu.VMEM((nb, 8 + chunk, shift_w), F32),
                        pltpu.VMEM((nb * rw // GROUP_W, GROUP_W, GROUP_W), F32)],
        compiler_params=pltpu.CompilerParams(dimension_semantics=("arbitrary", "arbitrary"),
                                             vmem_limit_bytes=VMEM_LIMIT),
        name="rwkv",
    )(p_rwkv.reshape(batch, seq, shift_w), shift0.reshape(batch, 1, shift_w), wkv0,
      row(prm["mu_shift"]), row(prm["w0"]), prm["w2"], row(prm["a0"]), prm["a2"], prm["g2"],
      row(prm["k_k"]), row(prm["k_a"]), row(prm["r_k"]), row(prm["lnx_g"]), row(prm["lnx_b"]))
    return yg.reshape(t, rw), wkv, shift.reshape(batch, shift_w)


def _mix_kernel(x_ref, pglu_ref, pg_ref, yg_ref, conv0_ref, bglu_ref, wdw_ref, bdw_ref,
                lncg_ref, lncb_ref, wpw2_ref, bpw2_ref, wouta_ref, wo_ref, n2g_ref, wr_ref, br_ref,
                cnt0_ref,
                h_ref, hn_ref, tope_ref, gate_ref, rank_ref, cnt_ref, convn_ref,
                full_ref, base_ref, cf_ref, win_ref, *, nb, tc):
    c = pl.program_id(1)
    ch = x_ref.shape[1]
    hist = CONV_WIDTH - 1
    lo = HIST_ROWS - hist

    @pl.when(jnp.logical_and(pl.program_id(0) == 0, c == 0))
    def _():
        base_ref[...] = cnt0_ref[...]

    @pl.when(c == 0)
    def _():
        full_ref[:, lo:HIST_ROWS, :] = conv0_ref[...]

    @pl.when(c > 0)
    def _():
        full_ref[:, 0:HIST_ROWS, :] = full_ref[:, tc:tc + HIST_ROWS, :]

    z = pglu_ref[...] + bglu_ref[...]
    u = z[:, :ch] * _sigmoid(z[:, ch:])
    full_ref[:, HIST_ROWS:HIST_ROWS + tc, :] = u.reshape(nb, tc, ch)
    convn_ref[...] = full_ref[:, tc + lo:tc + HIST_ROWS, :]

    b_tile, win_rows, _ = win_ref.shape
    r_tile = win_rows - (HIST_ROWS - SUBLANES)
    for b0 in range(0, nb, b_tile):
        for r0 in range(0, tc, r_tile):
            for l0 in range(0, ch, CONV_LANE_TILE):
                ls = slice(l0, l0 + CONV_LANE_TILE)
                acc = jnp.zeros((b_tile, r_tile, CONV_LANE_TILE), F32)
                for s in range(SUBLANES):
                    qs = [q for q in range(HIST_ROWS // SUBLANES + 1)
                          if lo <= SUBLANES * q + s <= lo + CONV_WIDTH - 1]
                    start = r0 + SUBLANES * qs[0] + s
                    n_win = r_tile + SUBLANES * (qs[-1] - qs[0])
                    win_ref[:, 0:n_win, :] = full_ref[b0:b0 + b_tile, start:start + n_win, ls]
                    for q in qs:
                        off = SUBLANES * (q - qs[0])
                        w = SUBLANES * q + s - lo
                        acc = acc + win_ref[:, off:off + r_tile, :] * wdw_ref[w:w + 1, ls]
                cf_ref[b0:b0 + b_tile, r0:r0 + r_tile, ls] = acc + bdw_ref[:, ls]
    cf = cf_ref[...].reshape(nb * tc, ch)
    mu = jnp.mean(cf, axis=-1, keepdims=True)
    cc = cf - mu
    var = jnp.mean(cc * cc, axis=-1, keepdims=True)
    cn = cc * lax.rsqrt(var + LN_EPS) * lncg_ref[...] + lncb_ref[...]
    act = cn * _sigmoid(cn)
    y_b = _dot_bf16(act, wpw2_ref[...]) + bpw2_ref[...]
    y_a = jnp.dot(yg_ref[...], wouta_ref[...], preferred_element_type=F32)
    pg = pg_ref[...]
    merged = _sigmoid(pg[:, :ch]) * y_a + _sigmoid(pg[:, ch:]) * y_b
    h = x_ref[...] + _dot_bf16(merged, wo_ref[...])
    h_ref[...] = h
    hn = _rms_norm(h, n2g_ref[...])
    hn_ref[...] = hn

    logits = _mm(hn, _parts(wr_ref[...])) + br_ref[...]
    lane = lax.broadcasted_iota(jnp.int32, logits.shape, 1)
    vals, idxs = [], []
    l = logits
    for _ in range(TOP_K):
        m = jnp.max(l, axis=-1, keepdims=True)
        idx = jnp.min(jnp.where(l == m, lane, LANES), axis=-1, keepdims=True)
        vals.append(m)
        idxs.append(idx)
        l = jnp.where(lane == idx, -jnp.inf, l)
    exps = [jnp.exp(vk - vals[0]) for vk in vals]
    denom = exps[0]
    for e in exps[1:]:
        denom = denom + e
    te = jnp.zeros(logits.shape, jnp.int32)
    tg = jnp.zeros(logits.shape, F32)
    for kx in range(TOP_K):
        te = jnp.where(lane == kx, idxs[kx], te)
        tg = jnp.where(lane == kx, exps[kx] / denom, tg)
    tope_ref[...] = te
    gate_ref[...] = tg

    rows = logits.shape[0]
    sel = jnp.zeros(logits.shape, F32)
    for kx in range(TOP_K):
        sel = sel + (lane == idxs[kx]).astype(F32)
    earlier = (lax.broadcasted_iota(jnp.int32, (rows, rows), 0)
               > lax.broadcasted_iota(jnp.int32, (rows, rows), 1)).astype(BF16)
    pos = jnp.dot(earlier, sel.astype(BF16), preferred_element_type=F32) + base_ref[...]
    rk = jnp.zeros(logits.shape, jnp.int32)
    for kx in range(TOP_K):
        r_k = jnp.sum(jnp.where(lane == idxs[kx], pos, 0.0), axis=-1, keepdims=True)
        rk = jnp.where(lane == kx, r_k.astype(jnp.int32), rk)
    rank_ref[...] = rk
    total = base_ref[...] + jnp.sum(sel, axis=0, keepdims=True)
    base_ref[...] = total
    cnt_ref[...] = total


def _mix(x2d, p_glu, p_gate, yg, conv0, cnt0, prm, batch, seq, nb, tc):
    t, d = x2d.shape
    ch = conv0.shape[2]
    hist = CONV_WIDTH - 1
    nc = seq // tc
    rows = nb * tc
    r_tile = min(tc, CONV_ROW_TILE)
    win_rows = r_tile + HIST_ROWS - SUBLANES
    b_tile = min(nb, max(1, CONV_WINDOW_ROWS // win_rows))
    assert tc % r_tile == 0 and nb % b_tile == 0 and ch % CONV_LANE_TILE == 0
    row = lambda a: a.reshape(1, -1)
    const = lambda shape: pl.BlockSpec(shape, lambda b, c: (0,) * len(shape))
    tok = lambda width: pl.BlockSpec((rows, width), lambda b, c: (b * nc + c, 0))
    wr = jnp.zeros((d, LANES), F32).at[:, :N_EXPERTS].set(prm["w_router"])
    br = jnp.full((1, LANES), -1e30, F32).at[0, :N_EXPERTS].set(prm["b_router"])
    return pl.pallas_call(
        functools.partial(_mix_kernel, nb=nb, tc=tc),
        out_shape=(jax.ShapeDtypeStruct((t, d), F32),
                   jax.ShapeDtypeStruct((t, d), F32),
                   jax.ShapeDtypeStruct((t, LANES), jnp.int32),
                   jax.ShapeDtypeStruct((t, LANES), F32),
                   jax.ShapeDtypeStruct((t, LANES), jnp.int32),
                   jax.ShapeDtypeStruct((1, LANES), F32),
                   jax.ShapeDtypeStruct((batch, hist, ch), F32)),
        grid=(batch // nb, nc),
        in_specs=[tok(d), tok(2 * ch), tok(2 * d), tok(d),
                  pl.BlockSpec((nb, hist, ch), lambda b, c: (b, 0, 0)),
                  const((1, 2 * ch)), const((CONV_WIDTH, ch)), const((1, ch)), const((1, ch)),
                  const((1, ch)), const((ch, d)), const((1, d)), const((d, d)), const((d, d)),
                  const((1, d)), const((d, LANES)), const((1, LANES)), const((1, LANES))],
        out_specs=(tok(d), tok(d), tok(LANES), tok(LANES), tok(LANES), const((1, LANES)),
                   pl.BlockSpec((nb, hist, ch), lambda b, c: (b, 0, 0))),
        scratch_shapes=[pltpu.VMEM((nb, HIST_ROWS + tc, ch), F32), pltpu.VMEM((1, LANES), F32),
                        pltpu.VMEM((nb, tc, ch), F32),
                        pltpu.VMEM((b_tile, win_rows, CONV_LANE_TILE), F32)],
        compiler_params=pltpu.CompilerParams(dimension_semantics=("arbitrary", "arbitrary"),
                                             vmem_limit_bytes=VMEM_LIMIT),
        name="mix",
    )(x2d, p_glu, p_gate, yg, conv0,
      row(prm["b_glu"]), prm["w_dw"].reshape(CONV_WIDTH, ch), row(prm["b_dw"]),
      row(prm["ln_conv_g"]), row(prm["ln_conv_b"]), prm["w_pw2"].astype(BF16), row(prm["b_pw2"]),
      prm["w_out_a"].astype(BF16), prm["w_o"].astype(BF16), row(prm["norm2_g"]), wr, br, cnt0)


def _expert_kernel(blk_e_ref, nused_ref, run_par_ref, next_e_ref, tok_cur_ref, tok_nxt_ref, hn_hbm,
                   wup_hbm, bup_ref, wdn_hbm, bdn_ref, out_ref,
                   xbuf0, xbuf1, wup_f32, wdn_f32, wup_bf, wdn_bf, sem, wsem, *, n_blocks):
    i = pl.program_id(0)
    n_used = nused_ref[0]
    rows = xbuf0.shape[0]
    d_ff = wdn_bf.shape[0]
    assert wup_bf.shape[0] == SUBLANES * LANES

    def weight_copies(e, par):
        return (pltpu.make_async_copy(wup_hbm.at[e], wup_f32.at[par], wsem.at[0, par]),
                pltpu.make_async_copy(wdn_hbm.at[e], wdn_f32.at[par], wsem.at[1, par]))

    def start_gather(tok_ref, buf, s):
        for q in range(rows):
            r = (q * GATHER_ISSUE_STRIDE) % rows
            pltpu.make_async_copy(hn_hbm.at[tok_ref[0, 0, r]], buf.at[r], sem.at[s]).start()

    def wait_gather(buf, s):
        pltpu.make_async_copy(hn_hbm.at[pl.ds(0, rows)], buf, sem.at[s]).wait()

    def token_rows(buf):
        return jnp.concatenate([buf[:, j, :] for j in range(SUBLANES)], axis=1)

    @pl.when(i == 0)
    def _():
        for cp in weight_copies(blk_e_ref[0], 0):
            cp.start(priority=WEIGHT_DMA_PRIORITY)
        start_gather(tok_cur_ref, xbuf0, 0)

    @pl.when(i >= n_used)
    def _():
        out_ref[...] = jnp.zeros(out_ref.shape, F32)

    def step(cur, s_cur, nxt, s_nxt):
        @pl.when(i <= n_used)
        def _():
            wait_gather(cur, s_cur)

        @pl.when(i < n_used)
        def _():
            e = blk_e_ref[i]
            prev_e = blk_e_ref[jnp.maximum(i - 1, 0)]
            @pl.when(jnp.logical_or(i == 0, e != prev_e))
            def _():
                par = run_par_ref[i]
                for cp in weight_copies(e, par):
                    cp.wait()
                wup_bf[...] = wup_f32[par].astype(BF16)
                wdn_bf[...] = wdn_f32[par].astype(BF16)
                next_e = next_e_ref[i]
                @pl.when(next_e >= 0)
                def _():
                    for cp in weight_copies(next_e, 1 - par):
                        cp.start(priority=WEIGHT_DMA_PRIORITY)

            start_gather(tok_nxt_ref, nxt, s_nxt)
            hb = (jnp.dot(token_rows(cur).astype(BF16), wup_bf[...], preferred_element_type=F32)
                  + bup_ref[0])
            glu = jnp.minimum(hb[:, :d_ff], SWIGLU_LIMIT)
            lin = jnp.clip(hb[:, d_ff:], -SWIGLU_LIMIT, SWIGLU_LIMIT)
            act = glu * _sigmoid(SWIGLU_ALPHA * glu) * (lin + 1.0)
            out_ref[...] = (jnp.dot(act.astype(BF16), wdn_bf[...], preferred_element_type=F32)
                            + bdn_ref[0])

            @pl.when(i == n_blocks - 1)
            def _():
                wait_gather(nxt, s_nxt)

    @pl.when(i % 2 == 0)
    def _():
        step(xbuf0, 0, xbuf1, 1)

    @pl.when(i % 2 == 1)
    def _():
        step(xbuf1, 1, xbuf0, 0)


def _experts(hn, slot_tok, blk_e, n_used, run_par, next_e, w_up, b_up, w_down, b_down):
    t, d = hn.shape
    n_blocks = blk_e.shape[0]
    rows = EXPERT_BLOCK
    n_exp, _, ff2 = w_up.shape
    d_ff = w_down.shape[1]
    grid_spec = pltpu.PrefetchScalarGridSpec(
        num_scalar_prefetch=4,
        grid=(n_blocks,),
        in_specs=[pl.BlockSpec((1, 1, rows), lambda i, *_: (i, 0, 0), memory_space=pltpu.SMEM),
                  pl.BlockSpec((1, 1, rows), lambda i, *_: (jnp.minimum(i + 1, n_blocks - 1), 0, 0),
                               memory_space=pltpu.SMEM),
                  pl.BlockSpec(memory_space=pl.ANY),
                  pl.BlockSpec(memory_space=pl.ANY),
                  pl.BlockSpec((1, 1, ff2), lambda i, be, *_: (be[i], 0, 0)),
                  pl.BlockSpec(memory_space=pl.ANY),
                  pl.BlockSpec((1, 1, d), lambda i, be, *_: (be[i], 0, 0))],
        out_specs=pl.BlockSpec((rows, d), lambda i, *_: (i, 0)),
        scratch_shapes=[pltpu.VMEM((rows, SUBLANES, LANES), F32),
                        pltpu.VMEM((rows, SUBLANES, LANES), F32),
                        pltpu.VMEM((2, d, ff2), F32),
                        pltpu.VMEM((2, d_ff, d), F32),
                        pltpu.VMEM((d, ff2), BF16),
                        pltpu.VMEM((d_ff, d), BF16),
                        pltpu.SemaphoreType.DMA((2,)),
                        pltpu.SemaphoreType.DMA((2, 2))],
    )
    slot_tok3 = slot_tok.reshape(n_blocks, 1, rows)
    return pl.pallas_call(
        functools.partial(_expert_kernel, n_blocks=n_blocks),
        out_shape=jax.ShapeDtypeStruct((n_blocks * rows, d), F32),
        grid_spec=grid_spec,
        compiler_params=pltpu.CompilerParams(dimension_semantics=("arbitrary",),
                                             vmem_limit_bytes=VMEM_LIMIT),
        name="experts",
    )(blk_e, n_used, run_par, next_e, slot_tok3, slot_tok3, hn.reshape(t, SUBLANES, LANES),
      w_up, b_up.reshape(n_exp, 1, ff2), w_down, b_down.reshape(n_exp, 1, d))


def _combine_kernel(dest_cur_ref, dest_nxt_ref, gate_ref, h_ref, ys_hbm, g_ref, out_ref, buf, sem,
                    *, n):
    i = pl.program_id(0)
    rows = h_ref.shape[0]
    slot = i % 2

    groups = rows // SUBLANES

    def start_gather(dest_ref, s):
        def body(it, carry):
            for j in range(SUBLANES):
                for kx in range(TOP_K):
                    src_row = dest_ref[0, 0, (it * SUBLANES + j) * TOP_K + kx]
                    pltpu.make_async_copy(
                        ys_hbm.at[src_row >> LOG2_SUBLANES, pl.ds(src_row & (SUBLANES - 1), 1), :],
                        buf.at[(s * TOP_K + kx) * groups + it, pl.ds(j, 1), :], sem.at[s]).start()
            return carry

        lax.fori_loop(0, groups, body, 0)

    def choice_rows(s, kx):
        return buf.at[pl.ds((s * TOP_K + kx) * groups, groups)]

    def wait_gather(s):
        for kx in range(TOP_K):
            pltpu.make_async_copy(ys_hbm.at[pl.ds(0, groups)], choice_rows(s, kx), sem.at[s]).wait()

    @pl.when(i == 0)
    def _():
        start_gather(dest_cur_ref, 0)

    @pl.when(i + 1 < n)
    def _():
        start_gather(dest_nxt_ref, 1 - slot)

    wait_gather(slot)
    gates = gate_ref[...]
    y = None
    for kx in range(TOP_K):
        term = choice_rows(slot, kx)[...].reshape(rows, -1) * gates[:, kx:kx + 1]
        y = term if y is None else y + term
    out_ref[...] = _rms_norm(h_ref[...] + y, g_ref[...])


def _combine(dest, gates, h, ys, norm_g):
    t, d = h.shape
    rows = ROW_BLOCK
    nblk = t // rows
    dest3 = dest.reshape(nblk, 1, rows * TOP_K)
    return pl.pallas_call(
        functools.partial(_combine_kernel, n=nblk),
        out_shape=jax.ShapeDtypeStruct((t, d), F32),
        grid=(nblk,),
        in_specs=[pl.BlockSpec((1, 1, rows * TOP_K), lambda i: (i, 0, 0), memory_space=pltpu.SMEM),
                  pl.BlockSpec((1, 1, rows * TOP_K), lambda i: (jnp.minimum(i + 1, nblk - 1), 0, 0),
                               memory_space=pltpu.SMEM),
                  pl.BlockSpec((rows, LANES), lambda i: (i, 0)),
                  pl.BlockSpec((rows, d), lambda i: (i, 0)),
                  pl.BlockSpec(memory_space=pl.ANY),
                  pl.BlockSpec((1, d), lambda i: (0, 0))],
        out_specs=pl.BlockSpec((rows, d), lambda i: (i, 0)),
        scratch_shapes=[pltpu.VMEM((2 * TOP_K * rows // SUBLANES, SUBLANES, d), F32),
                        pltpu.SemaphoreType.DMA((2,))],
        compiler_params=pltpu.CompilerParams(dimension_semantics=("arbitrary",),
                                             vmem_limit_bytes=VMEM_LIMIT),
        name="combine",
    )(dest3, dest3, gates, h, ys.reshape(-1, SUBLANES, d), norm_g.reshape(1, d))


def _routing_tables(top_e, rank, counts):
    t = top_e.shape[0]
    n_assign = t * TOP_K
    padded = (counts + EXPERT_BLOCK - 1) // EXPERT_BLOCK * EXPERT_BLOCK
    pad_end = jnp.cumsum(padded)
    pad_start = pad_end - padded
    experts = jnp.arange(N_EXPERTS, dtype=jnp.int32)
    start_of = jnp.sum(jnp.where(top_e[..., None] == experts, pad_start, 0), axis=-1)
    dest = (start_of + rank).astype(jnp.int32)
    n_blocks = -(-n_assign // EXPERT_BLOCK) + N_EXPERTS
    flat_tok = jnp.arange(n_assign, dtype=jnp.int32) // TOP_K
    slot_tok = jnp.zeros((n_blocks * EXPERT_BLOCK,), jnp.int32).at[dest.reshape(-1)].set(
        flat_tok, unique_indices=True, mode="promise_in_bounds")
    blk_start = jnp.arange(n_blocks, dtype=jnp.int32) * EXPERT_BLOCK
    blk_e = jnp.minimum(jnp.sum((blk_start[:, None] >= pad_end[None, :]).astype(jnp.int32), axis=1),
                        N_EXPERTS - 1)
    n_used = (pad_end[-1:] // EXPERT_BLOCK).astype(jnp.int32)
    nonempty = counts > 0
    order = jnp.cumsum(nonempty.astype(jnp.int32)) - 1
    later = jnp.where(nonempty, experts, N_EXPERTS)
    next_nonempty = jnp.concatenate(
        [lax.cummin(later[::-1], axis=0)[::-1][1:], jnp.full((1,), N_EXPERTS, jnp.int32)])
    next_nonempty = jnp.where(next_nonempty >= N_EXPERTS, -1, next_nonempty)
    run_par = (order % 2)[blk_e].astype(jnp.int32)
    next_e = next_nonempty[blk_e].astype(jnp.int32)
    return dest, slot_tok, blk_e, n_used, run_par, next_e


def _mixers(x, wkv0, shift0, conv0, cnt0, prm, w_in_bf16, chunk, seqs_per_step, nb, tc):
    batch, seq, d = x.shape
    shift_w = shift0.shape[1]
    glu_w = 2 * conv0.shape[2]
    x2d = x.reshape(batch * seq, d)
    p_rwkv, p_glu, p_gate = _in_proj(x2d, prm["norm1_g"], w_in_bf16, shift_w, glu_w)
    yg, wkv_new, shift_new = _rwkv(p_rwkv, shift0, wkv0, prm, batch, seq, chunk, seqs_per_step)
    h, hn, top_e, gates, rank, cnt, conv_new = _mix(x2d, p_glu, p_gate, yg, conv0, cnt0, prm,
                                                    batch, seq, nb, tc)
    return h, hn, top_e, gates, rank, cnt, wkv_new, shift_new, conv_new


def kernel(x_prompt, x_sample, state_wkv, state_shift, state_conv, norm1_g, w_in, b_glu, mu_shift, w0, w2, a0, a2, g2, k_k, k_a, r_k, lnx_g, lnx_b, w_out_a, w_dw, b_dw, ln_conv_g, ln_conv_b, w_pw2, b_pw2, w_o, norm2_g, w_router, b_router, w_up, b_up, w_down, b_down, norm_f_g):
    prm = dict(norm1_g=norm1_g, b_glu=b_glu, mu_shift=mu_shift, w0=w0, w2=w2, a0=a0, a2=a2,
               g2=g2, k_k=k_k, k_a=k_a, r_k=r_k, lnx_g=lnx_g, lnx_b=lnx_b, w_out_a=w_out_a,
               w_dw=w_dw, b_dw=b_dw, ln_conv_g=ln_conv_g, ln_conv_b=ln_conv_b, w_pw2=w_pw2,
               b_pw2=b_pw2, w_o=w_o, norm2_g=norm2_g, w_router=w_router, b_router=b_router)
    bp, lp, d = x_prompt.shape
    bs, ls, _ = x_sample.shape
    n_heads = state_wkv.shape[1]
    shift_w = state_shift.shape[1]
    ch = state_conv.shape[2]
    w_in_bf16 = w_in.astype(BF16)

    zeros_wkv = jnp.zeros((bp, n_heads, HEAD_DIM, HEAD_DIM), F32)
    zeros_shift = jnp.zeros((bp, shift_w), F32)
    zeros_conv = jnp.zeros((bp, CONV_WIDTH - 1, ch), F32)
    zero_cnt = jnp.zeros((1, LANES), F32)
    hp, hnp, tep, gp, rkp, cnt_p, wkv_p, shift_p, conv_p = _mixers(
        x_prompt, zeros_wkv, zeros_shift, zeros_conv, zero_cnt, prm, w_in_bf16,
        chunk=min(64, lp), seqs_per_step=PROMPT_SEQS_PER_STEP, nb=1, tc=min(ROW_BLOCK, lp))
    hs, hns, tes, gs, rks, cnt, wkv_s, shift_s, conv_s = _mixers(
        x_sample, state_wkv, state_shift, state_conv, cnt_p, prm, w_in_bf16,
        chunk=ls, seqs_per_step=SAMPLE_SEQS_PER_STEP, nb=ROW_BLOCK // ls, tc=ls)

    hn = jnp.concatenate([hnp, hns], axis=0)
    top_e = jnp.concatenate([tep[:, :TOP_K], tes[:, :TOP_K]], axis=0)
    rank = jnp.concatenate([rkp[:, :TOP_K], rks[:, :TOP_K]], axis=0)
    counts = cnt[0, :N_EXPERTS].astype(jnp.int32)
    dest, slot_tok, blk_e, n_used, run_par, next_e = _routing_tables(top_e, rank, counts)
    ys = _experts(hn, slot_tok, blk_e, n_used, run_par, next_e, w_up, b_up, w_down, b_down)
    tp = bp * lp
    y_p = _combine(dest[:tp], gp, hp, ys, norm_f_g)
    y_s = _combine(dest[tp:], gs, hs, ys, norm_f_g)
    return (y_p.reshape(bp, lp, d), y_s.reshape(bs, ls, d),
            wkv_p, shift_p, conv_p, wkv_s, shift_s, conv_s)
```
